```python
import jax
import jax.numpy as jnp
from jax import lax
import numpy as np

D_MODEL = 1024
BATCH = 1
SEQ = 16384
DEPTH = 1
DEC_BATCH = 8
DEC_SEQ = 8192
PAST_LEN = 128

N_HEADS_A = 8
D_A = D_MODEL // 2
HEAD_DIM = D_A // N_HEADS_A
N_GROUPS_B = 4
D_B = D_MODEL // 4
GROUP_DIM_B = D_B // N_GROUPS_B
N_HEADS_C = 4
D_C = D_MODEL // 4
HEAD_DIM_C = D_C // N_HEADS_C
D_MIX = D_A + D_B + D_C
D_IN = 3 * D_A + 2 * D_B + D_C
DILATED_PATTERNS = ((128, 1), (512, 4), (2048, 16))
Q_BLOCK = 128
CHUNK_B = 128
N_MEM = 256
N_EXPERTS = 16
EC_CAPACITY_FACTOR = 2
D_EXPERT = D_MODEL
ROPE_THETA = 10000.0
EPS = 1e-6
NEG_INF = -1e30

kernel_name = 'hybrid_dilated_gmlp_ec_encoder'


def rmsnorm(x, g):
    xf = x.astype(jnp.float32)
    y = xf * lax.rsqrt(jnp.mean(xf * xf, axis=-1, keepdims=True) + EPS)
    return (y * g.astype(jnp.float32)).astype(x.dtype)


def layernorm(x, g, b):
    xf = x.astype(jnp.float32)
    mu = jnp.mean(xf, axis=-1, keepdims=True)
    var = jnp.mean(jnp.square(xf - mu), axis=-1, keepdims=True)
    y = (xf - mu) * lax.rsqrt(var + EPS)
    return (y * g.astype(jnp.float32) + b.astype(jnp.float32)).astype(x.dtype)


def rope(x):
    s, dh = x.shape[1], x.shape[-1]
    half = dh // 2
    inv_freq = jnp.power(ROPE_THETA, -jnp.arange(half, dtype=jnp.float32) * 2.0 / dh)
    ang = jnp.arange(s, dtype=jnp.float32)[:, None] * inv_freq[None, :]
    cos = jnp.cos(ang)[None, :, None, :]
    sin = jnp.sin(ang)[None, :, None, :]
    xf = x.astype(jnp.float32)
    x1, x2 = xf[..., :half], xf[..., half:]
    return jnp.concatenate([x1 * cos - x2 * sin, x2 * cos + x1 * sin], axis=-1).astype(x.dtype)


def banded_attention(q, k, v, radius):
    n, l, h, dh = q.shape
    nb = -(-l // Q_BLOCK)
    lp = nb * Q_BLOCK
    qb = jnp.pad(q, ((0, 0), (0, lp - l), (0, 0), (0, 0))).reshape(n, nb, Q_BLOCK, h, dh)
    kv_pad = ((0, 0), (radius, lp + Q_BLOCK - l - radius), (0, 0), (0, 0))
    kp = jnp.pad(k, kv_pad).reshape(n, nb + 1, Q_BLOCK, h, dh)
    vp = jnp.pad(v, kv_pad).reshape(n, nb + 1, Q_BLOCK, h, dh)
    kb = jnp.concatenate([kp[:, :-1], kp[:, 1:]], axis=2)
    vb = jnp.concatenate([vp[:, :-1], vp[:, 1:]], axis=2)
    t = jnp.arange(Q_BLOCK)[:, None]
    s = jnp.arange(2 * Q_BLOCK)[None, :]
    kpos = jnp.arange(nb)[:, None, None] * Q_BLOCK + s[None] - radius
    mask = (jnp.abs(s - radius - t)[None] <= radius) & (kpos >= 0) & (kpos < l)
    scores = jnp.einsum('nbqhd,nbkhd->nbhqk', qb, kb, preferred_element_type=jnp.float32) * (dh ** -0.5)
    scores = jnp.where(mask[None, :, None], scores, NEG_INF)
    m = jnp.max(scores, axis=-1, keepdims=True)
    p = jnp.exp(scores - m)
    den = jnp.sum(p, axis=-1)
    o = jnp.einsum('nbhqk,nbkhd->nbqhd', p.astype(v.dtype), vb, preferred_element_type=jnp.float32)
    o = o / jnp.swapaxes(den, -1, -2)[..., None]
    lse = jnp.swapaxes(m[..., 0] + jnp.log(den), -1, -2)
    return o.reshape(n, lp, h, dh)[:, :l], lse.reshape(n, lp, h)[:, :l]


def dilated_attention(q, k, v):
    b, s, h, dh = q.shape
    outs, lses = [], []
    for window, dil in DILATED_PATTERNS:
        radius = window // (2 * dil)
        sub = s // dil
        def by_residue(t):
            return t.reshape(b, sub, dil, h, dh).transpose(0, 2, 1, 3, 4).reshape(b * dil, sub, h, dh)
        o, lse = banded_attention(by_residue(q), by_residue(k), by_residue(v), radius)
        outs.append(o.reshape(b, dil, sub, h, dh).transpose(0, 2, 1, 3, 4).reshape(b, s, h, dh))
        lses.append(lse.reshape(b, dil, sub, h).transpose(0, 2, 1, 3).reshape(b, s, h))
    w = jax.nn.softmax(jnp.stack(lses), axis=0)
    return jnp.einsum('pbsh,pbshd->bshd', w, jnp.stack(outs))


def spatial_gating(uv, ln_g, ln_b, w_s, b_s):
    b, s, _ = uv.shape
    uv = jax.nn.gelu(uv)
    u, v = uv[..., :D_B], uv[..., D_B:]
    v = layernorm(v, ln_g, ln_b).reshape(b, s // CHUNK_B, CHUNK_B, N_GROUPS_B, GROUP_DIM_B)
    v = jnp.einsum('gts,bcsgd->bctgd', w_s, v) + b_s.T[None, None, :, :, None]
    return u * v.reshape(b, s, D_B)


def memory_attention(q, mem, g_mem, w_kv):
    b, s, _ = q.shape
    m = mem.shape[1]
    kv = rmsnorm(mem, g_mem) @ w_kv
    k = kv[..., :D_C].reshape(b, m, N_HEADS_C, HEAD_DIM_C)
    v = kv[..., D_C:].reshape(b, m, N_HEADS_C, HEAD_DIM_C)
    q = q.reshape(b, s, N_HEADS_C, HEAD_DIM_C)
    scores = jnp.einsum('bshd,bmhd->bhsm', q, k, preferred_element_type=jnp.float32) * (HEAD_DIM_C ** -0.5)
    p = jax.nn.softmax(scores, axis=-1)
    o = jnp.einsum('bhsm,bmhd->bshd', p.astype(v.dtype), v)
    return o.reshape(b, s, D_C)


def expert_choice(xn, w_router, w_gate, w_up, w_down):
    shp = xn.shape
    xt = xn.reshape(-1, shp[-1])
    t = xt.shape[0]
    cap = EC_CAPACITY_FACTOR * t // N_EXPERTS
    aff = jax.nn.softmax((xt @ w_router).astype(jnp.float32), axis=-1)
    g, idx = lax.top_k(aff.T, cap)
    xe = xt[idx]
    hdn = jax.nn.silu(jnp.einsum('ecd,edf->ecf', xe, w_gate)) * jnp.einsum('ecd,edf->ecf', xe, w_up)
    ye = jnp.einsum('ecf,efd->ecd', hdn, w_down)
    ye = ye * g[..., None].astype(ye.dtype)
    y = jnp.zeros_like(xt).at[idx.reshape(-1)].add(ye.reshape(-1, shp[-1]))
    return y.reshape(shp)


def encoder(x, mem, g_mix, w_in, ln_b_g, ln_b_b, w_spatial, b_spatial, g_mem, w_mem_kv,
            g_out, w_out, g_ffn, w_router, w_gate, w_up, w_down, g_final):
    b, s, _ = x.shape
    for l in range(DEPTH):
        h = rmsnorm(x, g_mix[l])
        z = h @ w_in[l]
        q_a = rope(z[..., :D_A].reshape(b, s, N_HEADS_A, HEAD_DIM))
        k_a = rope(z[..., D_A:2 * D_A].reshape(b, s, N_HEADS_A, HEAD_DIM))
        v_a = z[..., 2 * D_A:3 * D_A].reshape(b, s, N_HEADS_A, HEAD_DIM)
        uv_b = z[..., 3 * D_A:3 * D_A + 2 * D_B]
        q_c = z[..., 3 * D_A + 2 * D_B:]
        o_a = dilated_attention(q_a, k_a, v_a).astype(x.dtype).reshape(b, s, D_A)
        o_b = spatial_gating(uv_b, ln_b_g[l], ln_b_b[l], w_spatial[l], b_spatial[l])
        o_c = memory_attention(q_c, mem, g_mem[l], w_mem_kv[l])
        go = g_out[l]
        o = jnp.concatenate([rmsnorm(o_a, go[:D_A]),
                             rmsnorm(o_b, go[D_A:D_A + D_B]),
                             rmsnorm(o_c, go[D_A + D_B:])], axis=-1)
        x = x + o @ w_out[l]
        x = x + expert_choice(rmsnorm(x, g_ffn[l]), w_router[l], w_gate[l], w_up[l], w_down[l])
    return rmsnorm(x, g_final)


def setup_inputs(seed: int = 0) -> dict:
    key = jax.random.key(seed)
    ks = jax.random.split(key, 20)
    f32 = jnp.float32

    def nrm(k, shape, scale):
        return jax.random.normal(k, shape, f32) * scale

    def gain(k, shape):
        return 1.0 + 0.05 * jax.random.normal(k, shape, f32)

    return {
        'x_prompt': nrm(ks[0], (BATCH, SEQ, D_MODEL), 1.0),
        'x_sample': nrm(ks[1], (DEC_BATCH, DEC_SEQ, D_MODEL), 1.0),
        'mem_prompt': nrm(ks[2], (BATCH, N_MEM, D_MODEL), 1.0),
        'mem_sample': nrm(ks[3], (DEC_BATCH, N_MEM, D_MODEL), 1.0),
        'g_mix': gain(ks[4], (DEPTH, D_MODEL)),
        'w_in': nrm(ks[5], (DEPTH, D_MODEL, D_IN), D_MODEL ** -0.5),
        'ln_b_g': gain(ks[6], (DEPTH, D_B)),
        'ln_b_b': nrm(ks[7], (DEPTH, D_B), 0.02),
        'w_spatial': nrm(ks[8], (DEPTH, N_GROUPS_B, CHUNK_B, CHUNK_B), CHUNK_B ** -0.5),
        'b_spatial': gain(ks[9], (DEPTH, N_GROUPS_B, CHUNK_B)),
        'g_mem': gain(ks[10], (DEPTH, D_MODEL)),
        'w_mem_kv': nrm(ks[11], (DEPTH, D_MODEL, 2 * D_C), D_MODEL ** -0.5),
        'g_out': gain(ks[12], (DEPTH, D_MIX)),
        'w_out': nrm(ks[13], (DEPTH, D_MIX, D_MODEL), D_MIX ** -0.5),
        'g_ffn': gain(ks[14], (DEPTH, D_MODEL)),
        'w_router': nrm(ks[15], (DEPTH, D_MODEL, N_EXPERTS), D_MODEL ** -0.5),
        'w_gate': nrm(ks[16], (DEPTH, N_EXPERTS, D_MODEL, D_EXPERT), D_MODEL ** -0.5),
        'w_up': nrm(ks[17], (DEPTH, N_EXPERTS, D_MODEL, D_EXPERT), D_MODEL ** -0.5),
        'w_down': nrm(ks[18], (DEPTH, N_EXPERTS, D_EXPERT, D_MODEL), D_EXPERT ** -0.5),
        'g_final': gain(ks[19], (D_MODEL,)),
    }


def reference(x_prompt, x_sample, mem_prompt, mem_sample, g_mix, w_in, ln_b_g, ln_b_b,
              w_spatial, b_spatial, g_mem, w_mem_kv, g_out, w_out, g_ffn, w_router,
              w_gate, w_up, w_down, g_final):
    y_prompt = encoder(x_prompt, mem_prompt, g_mix, w_in, ln_b_g, ln_b_b, w_spatial, b_spatial,
                       g_mem, w_mem_kv, g_out, w_out, g_ffn, w_router, w_gate, w_up, w_down, g_final)
    y_sample = encoder(x_sample, mem_sample, g_mix, w_in, ln_b_g, ln_b_b, w_spatial, b_spatial,
                       g_mem, w_mem_kv, g_out, w_out, g_ffn, w_router, w_gate, w_up, w_down, g_final)
    return (y_prompt, y_sample)
```

```python
import functools

import jax
import jax.numpy as jnp
from jax import lax
from jax.experimental import pallas as pl
from jax.experimental.pallas import tpu as pltpu

F32 = jnp.float32
BF16 = jnp.bfloat16

D_MODEL = 1024
N_HEADS_A = 8
HEAD_DIM = 64
D_A = 512
D_B = 256
N_GROUPS_B = 4
CHUNK_B = 128
D_C = 256
N_HEADS_C = 4
N_MEM = 256
N_EXPERTS = 16
EC_CAPACITY_FACTOR = 2
DILATIONS = (1, 4, 16)
RADIUS = 64
Q_BLOCK = 128
ROPE_THETA = 10000.0
EPS = 1e-6
NEG_INF = -1e30

LANES = 128
BF16_ROWS = 16
TOKEN_TILE = 512
ATTN_TILE = 512
MOE_TILE = 256
SLOT_WIN = 64
FFN_BLOCK = 512
UNSELECTED = -4096.0


def _rms(x, g):
    return x * lax.rsqrt(jnp.mean(x * x, axis=-1, keepdims=True) + EPS) * g


def _mem_kv_kernel(mem_ref, g_ref, w_ref, k_ref, v_ref):
    h = _rms(mem_ref[0], g_ref[...]).astype(BF16)
    kv = jnp.dot(h, w_ref[...], preferred_element_type=F32)
    k_ref[0] = kv[:, :D_C].astype(BF16)
    v_ref[0] = kv[:, D_C:].astype(BF16)


def _mem_kv(mem, g_mem, w_kv):
    b = mem.shape[0]
    return pl.pallas_call(
        _mem_kv_kernel,
        grid=(b,),
        in_specs=[
            pl.BlockSpec((1, N_MEM, D_MODEL), lambda i: (i, 0, 0)),
            pl.BlockSpec((1, D_MODEL), lambda i: (0, 0)),
            pl.BlockSpec((D_MODEL, 2 * D_C), lambda i: (0, 0)),
        ],
        out_specs=[
            pl.BlockSpec((1, N_MEM, D_C), lambda i: (i, 0, 0)),
            pl.BlockSpec((1, N_MEM, D_C), lambda i: (i, 0, 0)),
        ],
        out_shape=[jax.ShapeDtypeStruct((b, N_MEM, D_C), BF16)] * 2,
        name="mem_kv",
    )(mem, g_mem, w_kv)


def _in_proj_kernel(x_ref, g_ref, w_ref, cos_ref, sin_ref, lng_ref, lnb_ref,
                    q_ref, k_ref, v_ref, u_ref, vb_ref, qc_ref):
    tm = x_ref.shape[0]
    h = _rms(x_ref[...], g_ref[...]).astype(BF16)
    cos = jnp.concatenate([cos_ref[...]] * (D_A // LANES), axis=1)
    sin = jnp.concatenate([sin_ref[...]] * (D_A // LANES), axis=1)
    lane = lax.broadcasted_iota(jnp.int32, (tm, D_A), 1)
    first_half = (lane & (HEAD_DIM - 1)) < (HEAD_DIM // 2)

    def rope(z):
        rot = jnp.where(first_half, pltpu.roll(z, D_A - HEAD_DIM // 2, 1),
                        pltpu.roll(z, HEAD_DIM // 2, 1))
        return z * cos + rot * sin

    def proj(lo, hi):
        return jnp.dot(h, w_ref[:, lo:hi], preferred_element_type=F32)

    q_ref[...] = rope(proj(0, D_A)).astype(BF16)
    k_ref[...] = rope(proj(D_A, 2 * D_A)).astype(BF16)
    v_ref[...] = proj(2 * D_A, 3 * D_A).astype(BF16)
    u_ref[...] = jax.nn.gelu(proj(3 * D_A, 3 * D_A + D_B)).astype(BF16)
    vb = jax.nn.gelu(proj(3 * D_A + D_B, 3 * D_A + 2 * D_B))
    mu = jnp.mean(vb, axis=-1, keepdims=True)
    var = jnp.mean(jnp.square(vb - mu), axis=-1, keepdims=True)
    vb_ref[...] = ((vb - mu) * lax.rsqrt(var + EPS) * lng_ref[...] + lnb_ref[...]).astype(BF16)
    qc_ref[...] = proj(3 * D_A + 2 * D_B, 3 * D_A + 2 * D_B + D_C).astype(BF16)


def _in_proj(x2, g_mix, w_in, cos_t, sin_t, ln_g, ln_b, seq):
    t = x2.shape[0]
    tm = TOKEN_TILE
    d_in = w_in.shape[1]
    pos_blocks = seq // tm
    row = lambda i: (i, 0)
    const = lambda i: (0, 0)
    pos = lambda i: (i % pos_blocks, 0)
    widths = (D_A, D_A, D_A, D_B, D_B, D_C)
    return pl.pallas_call(
        _in_proj_kernel,
        grid=(t // tm,),
        in_specs=[
            pl.BlockSpec((tm, D_MODEL), row),
            pl.BlockSpec((1, D_MODEL), const),
            pl.BlockSpec((D_MODEL, d_in), const),
            pl.BlockSpec((tm, LANES), pos),
            pl.BlockSpec((tm, LANES), pos),
            pl.BlockSpec((1, D_B), const),
            pl.BlockSpec((1, D_B), const),
        ],
        out_specs=[pl.BlockSpec((tm, w), row) for w in widths],
        out_shape=[jax.ShapeDtypeStruct((t, w), BF16) for w in widths],
        name="in_proj",
    )(x2, g_mix, w_in, cos_t, sin_t, ln_g, ln_b)


def _attn_kernel(q_ref, kp_ref, kc_ref, kn_ref, vp_ref, vc_ref, vn_ref,
                 o_ref, lse_ref, kwin, vwin, *, tq, sub):
    j = pl.program_id(2)
    kwin[0:RADIUS] = kp_ref[0]
    kwin[RADIUS:RADIUS + tq] = kc_ref[0]
    kwin[RADIUS + tq:] = kn_ref[0]
    vwin[0:RADIUS] = vp_ref[0]
    vwin[RADIUS:RADIUS + tq] = vc_ref[0]
    vwin[RADIUS + tq:] = vn_ref[0]

    t_idx = lax.broadcasted_iota(jnp.int32, (Q_BLOCK, 2 * Q_BLOCK), 0)
    s_idx = lax.broadcasted_iota(jnp.int32, (Q_BLOCK, 2 * Q_BLOCK), 1)
    band = jnp.abs(s_idx - RADIUS - t_idx) <= RADIUS
    lane = lax.broadcasted_iota(jnp.int32, (Q_BLOCK, LANES), 1)
    low_half = lane < HEAD_DIM

    def block(i, carry):
        r0 = pl.multiple_of(i * Q_BLOCK, Q_BLOCK)
        kpos = j * tq + r0 + s_idx - RADIUS
        bias = jnp.where(band & (kpos >= 0) & (kpos < sub), 0.0, NEG_INF)
        lse_blk = jnp.zeros((Q_BLOCK, LANES), F32)
        for hp in range(N_HEADS_A // 2):
            cols = slice(hp * LANES, (hp + 1) * LANES)
            q_pair = q_ref[0, pl.ds(r0, Q_BLOCK), cols]
            k_pair = kwin[pl.ds(r0, 2 * Q_BLOCK), cols]
            v_pair = vwin[pl.ds(r0, 2 * Q_BLOCK), cols]
            halves = []
            for half in range(2):
                keep = low_half if half == 0 else jnp.logical_not(low_half)
                qm = jnp.where(keep, q_pair, jnp.zeros_like(q_pair))
                s = lax.dot_general(qm, k_pair, (((1,), (1,)), ((), ())),
                                    preferred_element_type=F32) + bias
                m = jnp.max(s, axis=-1, keepdims=True)
                p = jnp.exp(s - m)
                den = jnp.sum(p, axis=-1, keepdims=True)
                pv = jnp.dot(p.astype(BF16), v_pair, preferred_element_type=F32)
                halves.append(pv / den)
                lse_blk = jnp.where(lane == 2 * hp + half, m + jnp.log(den), lse_blk)
            o_ref[0, pl.ds(r0, Q_BLOCK), cols] = jnp.where(low_half, halves[0], halves[1]).astype(BF16)
        lse_ref[0, pl.ds(r0, Q_BLOCK), :] = lse_blk
        return carry

    lax.fori_loop(0, tq // Q_BLOCK, block, 0)


def _banded_attention(q, k, v, b, seq, dil):
    sub = seq // dil
    tq = min(ATTN_TILE, sub)
    halo_per_tile = tq // RADIUS
    last_halo = sub // RADIUS - 1
    view = lambda a: a.reshape(b, sub, dil * D_A)
    qv, kv, vv = view(q), view(k), view(v)
    cur = pl.BlockSpec((1, tq, D_A), lambda bi, r, j: (bi, j, r))
    prev = pl.BlockSpec((1, RADIUS, D_A),
                        lambda bi, r, j: (bi, jnp.maximum(j * halo_per_tile - 1, 0), r))
    nxt = pl.BlockSpec((1, RADIUS, D_A),
                       lambda bi, r, j: (bi, jnp.minimum((j + 1) * halo_per_tile, last_halo), r))
    o, lse = pl.pallas_call(
        functools.partial(_attn_kernel, tq=tq, sub=sub),
        grid=(b, dil, sub // tq),
        in_specs=[cur, prev, cur, nxt, prev, cur, nxt],
        out_specs=[
            pl.BlockSpec((1, tq, D_A), lambda bi, r, j: (bi, j, r)),
            pl.BlockSpec((1, tq, LANES), lambda bi, r, j: (bi, j, r)),
        ],
        out_shape=[
            jax.ShapeDtypeStruct((b, sub, dil * D_A), BF16),
            jax.ShapeDtypeStruct((b, sub, dil * LANES), F32),
        ],
        scratch_shapes=[pltpu.VMEM((tq + 2 * RADIUS, D_A), BF16)] * 2,
        name=f"attn_d{dil}",
    )(qv, kv, kv, kv, vv, vv, vv)
    return o.reshape(b * seq, D_A), lse.reshape(b * seq, LANES)


def _split_bf16(x):
    hi = x.astype(BF16)
    lo = (x - hi.astype(F32)).astype(BF16)
    return hi, lo


def _mix_kernel(x_ref, o1_ref, o2_ref, o3_ref, l1_ref, l2_ref, l3_ref, u_ref, vb_ref, qc_ref,
                mk_ref, mv_ref, ws_ref, bs_ref, expand_ref, go_ref, wo_ref, gf_ref, wr_ref,
                x1_ref, xn_ref, aff_ref, afft_ref):
    tm = x_ref.shape[0]
    l1, l2, l3 = l1_ref[...], l2_ref[...], l3_ref[...]
    mx = jnp.maximum(jnp.maximum(l1, l2), l3)
    e1, e2, e3 = jnp.exp(l1 - mx), jnp.exp(l2 - mx), jnp.exp(l3 - mx)
    tot = e1 + e2 + e3
    o_a = jnp.zeros((tm, D_A), F32)
    for e, o_ref in ((e1, o1_ref), (e2, o2_ref), (e3, o3_ref)):
        hi, lo = _split_bf16(e / tot)
        w = (jnp.dot(hi, expand_ref[...], preferred_element_type=F32)
             + jnp.dot(lo, expand_ref[...], preferred_element_type=F32))
        o_a = o_a + w * o_ref[...].astype(F32)

    lane_b = lax.broadcasted_iota(jnp.int32, (CHUNK_B, D_B), 1)
    group_w = D_B // N_GROUPS_B
    gated = []
    for c in range(tm // CHUNK_B):
        vchunk = vb_ref[c * CHUNK_B:(c + 1) * CHUNK_B, :]
        acc = bs_ref[...]
        for g in range(N_GROUPS_B):
            y = jnp.dot(ws_ref[g], vchunk, preferred_element_type=F32)
            acc = acc + jnp.where(lane_b // group_w == g, y, 0.0)
        gated.append(acc)
    o_b = u_ref[...].astype(F32) * jnp.concatenate(gated, axis=0)

    lane_c = lax.broadcasted_iota(jnp.int32, (tm, LANES), 1)
    low_half = lane_c < HEAD_DIM
    oc_parts = []
    for hp in range(N_HEADS_C // 2):
        cols = slice(hp * LANES, (hp + 1) * LANES)
        q_pair = qc_ref[:, cols]
        k_pair = mk_ref[0, :, cols]
        v_pair = mv_ref[0, :, cols]
        halves = []
        for half in range(2):
            keep = low_half if half == 0 else jnp.logical_not(low_half)
            qm = jnp.where(keep, q_pair, jnp.zeros_like(q_pair))
            s = lax.dot_general(qm, k_pair, (((1,), (1,)), ((), ())), preferred_element_type=F32)
            m = jnp.max(s, axis=-1, keepdims=True)
            p = jnp.exp(s - m)
            p = p / jnp.sum(p, axis=-1, keepdims=True)
            halves.append(jnp.dot(p.astype(BF16), v_pair, preferred_element_type=F32))
        oc_parts.append(jnp.where(low_half, halves[0], halves[1]))
    o_c = jnp.concatenate(oc_parts, axis=1)

    go = go_ref[...]
    o = jnp.concatenate([_rms(o_a, go[:, :D_A]), _rms(o_b, go[:, D_A:D_A + D_B]),
                         _rms(o_c, go[:, D_A + D_B:])], axis=1).astype(BF16)
    x1 = x_ref[...] + jnp.dot(o, wo_ref[...], preferred_element_type=F32)
    x1_ref[...] = x1

    xn = _rms(x1, gf_ref[...]).astype(BF16)
    xn_ref[...] = xn
    logits = jnp.dot(xn, wr_ref[...], preferred_element_type=F32)
    is_expert = lane_c < N_EXPERTS
    logits = jnp.where(is_expert, logits, NEG_INF)
    m = jnp.max(logits, axis=-1, keepdims=True)
    ex = jnp.where(is_expert, jnp.exp(logits - m), 0.0)
    aff = ex / jnp.sum(ex, axis=-1, keepdims=True)
    aff_ref[...] = aff
    afft_ref[...] = jnp.transpose(aff)[:N_EXPERTS, :]


def _mix(x2, o_pats, lse_pats, u, vb, qc, mem_k, mem_v, w_sp, b_sp, expand, g_out, w_out,
         g_ffn, w_router, seq):
    t = x2.shape[0]
    tm = TOKEN_TILE
    tiles_per_seq = seq // tm
    row = lambda i: (i, 0)
    const2 = lambda i: (0, 0)
    const3 = lambda i: (0, 0, 0)
    batch = lambda i: (i // tiles_per_seq, 0, 0)
    return pl.pallas_call(
        _mix_kernel,
        grid=(t // tm,),
        in_specs=[
            pl.BlockSpec((tm, D_MODEL), row),
            pl.BlockSpec((tm, D_A), row), pl.BlockSpec((tm, D_A), row), pl.BlockSpec((tm, D_A), row),
            pl.BlockSpec((tm, LANES), row), pl.BlockSpec((tm, LANES), row), pl.BlockSpec((tm, LANES), row),
            pl.BlockSpec((tm, D_B), row), pl.BlockSpec((tm, D_B), row), pl.BlockSpec((tm, D_C), row),
            pl.BlockSpec((1, N_MEM, D_C), batch), pl.BlockSpec((1, N_MEM, D_C), batch),
            pl.BlockSpec((N_GROUPS_B, CHUNK_B, CHUNK_B), const3),
            pl.BlockSpec((CHUNK_B, D_B), const2),
            pl.BlockSpec((LANES, D_A), const2),
            pl.BlockSpec((1, D_MODEL), const2),
            pl.BlockSpec((D_MODEL, D_MODEL), const2),
            pl.BlockSpec((1, D_MODEL), const2),
            pl.BlockSpec((D_MODEL, LANES), const2),
        ],
        out_specs=[
            pl.BlockSpec((tm, D_MODEL), row),
            pl.BlockSpec((tm, D_MODEL), row),
            pl.BlockSpec((tm, LANES), row),
            pl.BlockSpec((N_EXPERTS, tm), lambda i: (0, i)),
        ],
        out_shape=[
            jax.ShapeDtypeStruct((t, D_MODEL), F32),
            jax.ShapeDtypeStruct((t, D_MODEL), BF16),
            jax.ShapeDtypeStruct((t, LANES), F32),
            jax.ShapeDtypeStruct((N_EXPERTS, t), F32),
        ],
        name="mix",
    )(x2, *o_pats, *lse_pats, u, vb, qc, mem_k, mem_v, w_sp, b_sp, expand, g_out, w_out,
      g_ffn, w_router)


def _threshold_kernel(afft_ref, thr_col_ref, quota_col_ref, thr_row_ref, quota_row_ref, *, cap):
    aff = afft_ref[...]
    bits = lax.bitcast_convert_type(aff, jnp.int32)

    def count_ge(cand):
        return jnp.sum(jnp.where(bits >= cand, 1.0, 0.0), axis=-1, keepdims=True)

    def step(it, thr):
        cand = thr | jnp.left_shift(jnp.int32(1), 30 - it)
        return jnp.where(count_ge(cand) >= cap, cand, thr)

    thr = lax.fori_loop(0, 31, step, jnp.zeros((N_EXPERTS, 1), jnp.int32))
    n_gt = jnp.sum(jnp.where(bits > thr, 1.0, 0.0), axis=-1, keepdims=True)
    quota = cap - n_gt
    thr_f = jnp.min(jnp.where(bits >= thr, aff, jnp.inf), axis=-1, keepdims=True)
    thr_col = jnp.broadcast_to(thr_f, (N_EXPERTS, LANES))
    quota_col = jnp.broadcast_to(quota, (N_EXPERTS, LANES))
    thr_col_ref[...] = thr_col
    quota_col_ref[...] = quota_col
    diag = (lax.broadcasted_iota(jnp.int32, (N_EXPERTS, LANES), 0)
            == lax.broadcasted_iota(jnp.int32, (N_EXPERTS, LANES), 1))
    thr_row = jnp.sum(jnp.where(diag, thr_col, 0.0), axis=0, keepdims=True)
    quota_row = jnp.sum(jnp.where(diag, quota_col, 0.0), axis=0, keepdims=True)
    thr_row_ref[...] = jnp.broadcast_to(thr_row, (8, LANES))
    quota_row_ref[...] = jnp.broadcast_to(quota_row, (8, LANES))


def _threshold(afft, cap):
    t = afft.shape[1]
    full = lambda shape: pl.BlockSpec(shape, lambda: (0,) * len(shape))
    return pl.pallas_call(
        functools.partial(_threshold_kernel, cap=float(cap)),
        in_specs=[full((N_EXPERTS, t))],
        out_specs=[full((N_EXPERTS, LANES)), full((N_EXPERTS, LANES)),
                   full((8, LANES)), full((8, LANES))],
        out_shape=[jax.ShapeDtypeStruct((N_EXPERTS, LANES), F32)] * 2
                  + [jax.ShapeDtypeStruct((8, LANES), F32)] * 2,
        name="threshold",
    )(afft)


def _positions_kernel(afft_ref, aff_ref, thr_col_ref, quota_col_ref, thr_row_ref, quota_row_ref,
                      lp_ref, lpt_ref, gate_ref, start_ref, total_ref,
                      eq_col, eq_row, off_col):
    i = pl.program_id(0)
    tm = aff_ref.shape[0]

    @pl.when(i == 0)
    def _():
        eq_col[...] = jnp.zeros_like(eq_col)
        eq_row[...] = jnp.zeros_like(eq_row)
        off_col[...] = jnp.zeros_like(off_col)

    r = lax.broadcasted_iota(jnp.int32, (tm, tm), 0)
    c = lax.broadcasted_iota(jnp.int32, (tm, tm), 1)
    before = jnp.where(r < c, 1.0, 0.0).astype(BF16)
    after = jnp.where(c < r, 1.0, 0.0).astype(BF16)

    a = afft_ref[...]
    thr, quota = thr_col_ref[:, 0:1], quota_col_ref[:, 0:1]
    eq = a == thr
    eq_f = jnp.where(eq, 1.0, 0.0)
    eq_rank = jnp.dot(eq_f.astype(BF16), before, preferred_element_type=F32) + eq_col[:, 0:1]
    sel = (a > thr) | (eq & (eq_rank < quota))
    sel_f = jnp.where(sel, 1.0, 0.0)
    lp = jnp.dot(sel_f.astype(BF16), before, preferred_element_type=F32)
    lp_ref[...] = jnp.where(sel, lp, UNSELECTED)
    count = jnp.sum(sel_f, axis=-1, keepdims=True)
    start = off_col[...]
    start_ref[0] = start
    padded = jnp.ceil(count / BF16_ROWS) * BF16_ROWS
    new_off = start + padded
    off_col[...] = new_off
    total_ref[...] = new_off
    eq_col[...] = eq_col[...] + jnp.sum(eq_f, axis=-1, keepdims=True)

    at = aff_ref[...]
    thr_r, quota_r = thr_row_ref[0:1, :], quota_row_ref[0:1, :]
    is_expert = lax.broadcasted_iota(jnp.int32, (tm, LANES), 1) < N_EXPERTS
    eq_t = (at == thr_r) & is_expert
    eq_tf = jnp.where(eq_t, 1.0, 0.0)
    eq_rank_t = jnp.dot(after, eq_tf.astype(BF16), preferred_element_type=F32) + eq_row[0:1, :]
    sel_t = ((at > thr_r) & is_expert) | (eq_t & (eq_rank_t < quota_r))
    sel_tf = jnp.where(sel_t, 1.0, 0.0)
    lp_t = jnp.dot(after, sel_tf.astype(BF16), preferred_element_type=F32)
    lpt_ref[...] = jnp.where(sel_t, lp_t, UNSELECTED)
    gate_ref[...] = jnp.where(sel_t, at, 0.0)
    eq_row[...] = eq_row[...] + jnp.sum(eq_tf, axis=0, keepdims=True)


def _positions(afft, aff, thr_col, quota_col, thr_row, quota_row):
    t = aff.shape[0]
    tm = MOE_TILE
    n_tiles = t // tm
    c2 = lambda i: (0, 0)
    return pl.pallas_call(
        _positions_kernel,
        grid=(n_tiles,),
        in_specs=[
            pl.BlockSpec((N_EXPERTS, tm), lambda i: (0, i)),
            pl.BlockSpec((tm, LANES), lambda i: (i, 0)),
            pl.BlockSpec((N_EXPERTS, LANES), c2), pl.BlockSpec((N_EXPERTS, LANES), c2),
            pl.BlockSpec((8, LANES), c2), pl.BlockSpec((8, LANES), c2),
        ],
        out_specs=[
            pl.BlockSpec((N_EXPERTS, tm), lambda i: (0, i)),
            pl.BlockSpec((tm, LANES), lambda i: (i, 0)),
            pl.BlockSpec((tm, LANES), lambda i: (i, 0)),
            pl.BlockSpec((1, N_EXPERTS, LANES), lambda i: (i, 0, 0)),
            pl.BlockSpec((N_EXPERTS, LANES), c2),
        ],
        out_shape=[
            jax.ShapeDtypeStruct((N_EXPERTS, t), F32),
            jax.ShapeDtypeStruct((t, LANES), F32),
            jax.ShapeDtypeStruct((t, LANES), F32),
            jax.ShapeDtypeStruct((n_tiles, N_EXPERTS, LANES), F32),
            jax.ShapeDtypeStruct((N_EXPERTS, LANES), F32),
        ],
        scratch_shapes=[pltpu.VMEM((N_EXPERTS, LANES), F32), pltpu.VMEM((8, LANES), F32),
                        pltpu.VMEM((N_EXPERTS, LANES), F32)],
        compiler_params=pltpu.CompilerParams(dimension_semantics=("arbitrary",)),
        name="positions",
    )(afft, aff, thr_col, quota_col, thr_row, quota_row)


def _one_hot_rows(lp_ref, round_base):
    tm = lp_ref.shape[1]
    q = lax.broadcasted_iota(jnp.int32, (SLOT_WIN, tm), 0).astype(F32) + round_base
    rows = [jnp.where(lp_ref[e:e + 1, :] == q, 1.0, 0.0).astype(BF16) for e in range(N_EXPERTS)]
    return jnp.concatenate(rows, axis=0)


def _dispatch_copy(stack, xe_ref, sem, slot, e, row0):
    return pltpu.make_async_copy(
        stack.at[slot, pl.ds(e * SLOT_WIN, SLOT_WIN), :],
        xe_ref.at[e, pl.ds(row0, SLOT_WIN), :],
        sem.at[slot])


def _dispatch_kernel(start_ref, count_ref, xn_ref, lp_ref, xe_ref, stack, extra, sem, sem_extra):
    i = pl.program_id(0)
    n = pl.num_programs(0)
    slot = i % 2

    def first_slot(tile, e):
        return pl.multiple_of(start_ref[tile * N_EXPERTS + e], BF16_ROWS)

    rows = jnp.dot(_one_hot_rows(lp_ref, 0.0), xn_ref[...], preferred_element_type=F32)
    stack[slot] = rows.astype(BF16)

    @pl.when(i > 0)
    def _():
        for e in range(N_EXPERTS):
            _dispatch_copy(stack, xe_ref, sem, 1 - slot, e, first_slot(i - 1, e)).wait()

    for e in range(N_EXPERTS):
        _dispatch_copy(stack, xe_ref, sem, slot, e, first_slot(i, e)).start()

    max_count = count_ref[i * N_EXPERTS]
    for e in range(1, N_EXPERTS):
        max_count = jnp.maximum(max_count, count_ref[i * N_EXPERTS + e])
    n_rounds = (max_count + SLOT_WIN - 1) // SLOT_WIN

    def extra_round(k, carry):
        base = (k * SLOT_WIN).astype(F32)
        rows_k = jnp.dot(_one_hot_rows(lp_ref, base), xn_ref[...], preferred_element_type=F32)
        extra[...] = rows_k.astype(BF16)
        for e in range(N_EXPERTS):
            @pl.when(count_ref[i * N_EXPERTS + e] > k * SLOT_WIN)
            def _():
                cp = pltpu.make_async_copy(
                    extra.at[pl.ds(e * SLOT_WIN, SLOT_WIN), :],
                    xe_ref.at[e, pl.ds(first_slot(i, e) + k * SLOT_WIN, SLOT_WIN), :],
                    sem_extra.at[0])
                cp.start()
                cp.wait()
        return carry

    lax.fori_loop(1, n_rounds, extra_round, 0)

    @pl.when(i == n - 1)
    def _():
        for e in range(N_EXPERTS):
            _dispatch_copy(stack, xe_ref, sem, slot, e, first_slot(i, e)).wait()
        extra[...] = jnp.zeros_like(extra)
        cap_rows = xe_ref.shape[1]

        def fill(e, row0, rows):
            return pltpu.make_async_copy(extra.at[pl.ds(0, rows), :],
                                         xe_ref.at[e, pl.ds(row0, rows), :], sem_extra.at[0])

        for e in range(N_EXPERTS):
            rounds = jnp.maximum((count_ref[i * N_EXPERTS + e] + SLOT_WIN - 1) // SLOT_WIN, 1)
            w_end = first_slot(i, e) + rounds * SLOT_WIN
            n_small = ((-w_end) & (SLOT_WIN - 1)) // BF16_ROWS
            base = w_end + n_small * BF16_ROWS
            n_big = (cap_rows - base) // SLOT_WIN

            def small_row(k):
                return pl.multiple_of(w_end + k * BF16_ROWS, BF16_ROWS)

            def big_row(k):
                return pl.multiple_of(base + k * SLOT_WIN, SLOT_WIN)

            lax.fori_loop(0, n_small, lambda k, c: (fill(e, small_row(k), BF16_ROWS).start(), c)[1], 0)
            lax.fori_loop(0, n_big, lambda k, c: (fill(e, big_row(k), SLOT_WIN).start(), c)[1], 0)
            lax.fori_loop(0, n_small, lambda k, c: (fill(e, small_row(k), BF16_ROWS).wait(), c)[1], 0)
            lax.fori_loop(0, n_big, lambda k, c: (fill(e, big_row(k), SLOT_WIN).wait(), c)[1], 0)


def _dispatch(start, count, xn, lp, cap_rows):
    t = xn.shape[0]
    tm = MOE_TILE
    return pl.pallas_call(
        _dispatch_kernel,
        grid_spec=pltpu.PrefetchScalarGridSpec(
            num_scalar_prefetch=2,
            grid=(t // tm,),
            in_specs=[
                pl.BlockSpec((tm, D_MODEL), lambda i, s, c: (i, 0)),
                pl.BlockSpec((N_EXPERTS, tm), lambda i, s, c: (0, i)),
            ],
            out_specs=pl.BlockSpec(memory_space=pl.ANY),
            scratch_shapes=[pltpu.VMEM((2, N_EXPERTS * SLOT_WIN, D_MODEL), BF16),
                            pltpu.VMEM((N_EXPERTS * SLOT_WIN, D_MODEL), BF16),
                            pltpu.SemaphoreType.DMA((2,)),
                            pltpu.SemaphoreType.DMA((1,))],
        ),
        out_shape=jax.ShapeDtypeStruct((N_EXPERTS, cap_rows, D_MODEL), BF16),
        compiler_params=pltpu.CompilerParams(dimension_semantics=("arbitrary",)),
        name="dispatch",
    )(start, count, xn, lp)


def _ffn_kernel(used_ref, xe_ref, wg_ref, wu_ref, wd_ref, ye_ref):
    e, j = pl.program_id(0), pl.program_id(1)
    bs = xe_ref.shape[1]
    n_valid = used_ref[e] - j * bs

    @pl.when(n_valid > 0)
    def _():
        row = lax.broadcasted_iota(jnp.int32, (bs, D_MODEL), 0)
        x = jnp.where(row < n_valid, xe_ref[0], jnp.zeros((bs, D_MODEL), BF16))
        gate = jnp.dot(x, wg_ref[0], preferred_element_type=F32)
        up = jnp.dot(x, wu_ref[0], preferred_element_type=F32)
        hdn = (jax.nn.silu(gate) * up).astype(BF16)
        ye_ref[0] = jnp.dot(hdn, wd_ref[0], preferred_element_type=F32).astype(BF16)

    @pl.when(n_valid <= 0)
    def _():
        ye_ref[0] = jnp.zeros((bs, D_MODEL), BF16)


def _ffn(used, xe, w_gate, w_up, w_down):
    cap_rows = xe.shape[1]
    bs = FFN_BLOCK

    def x_map(e, j, used_ref):
        last = jnp.maximum((used_ref[e] + bs - 1) // bs - 1, 0)
        return (e, jnp.minimum(j, last), 0)

    w_map = lambda e, j, used_ref: (e, 0, 0)
    return pl.pallas_call(
        _ffn_kernel,
        grid_spec=pltpu.PrefetchScalarGridSpec(
            num_scalar_prefetch=1,
            grid=(N_EXPERTS, cap_rows // bs),
            in_specs=[
                pl.BlockSpec((1, bs, D_MODEL), x_map),
                pl.BlockSpec((1, D_MODEL, D_MODEL), w_map),
                pl.BlockSpec((1, D_MODEL, D_MODEL), w_map),
                pl.BlockSpec((1, D_MODEL, D_MODEL), w_map),
            ],
            out_specs=pl.BlockSpec((1, bs, D_MODEL), lambda e, j, used_ref: (e, j, 0)),
        ),
        out_shape=jax.ShapeDtypeStruct((N_EXPERTS, cap_rows, D_MODEL), BF16),
        name="expert_ffn",
    )(used, xe, w_gate, w_up, w_down)


def _combine_copy(ye_ref, stack, sem, slot, e, row0):
    return pltpu.make_async_copy(
        ye_ref.at[e, pl.ds(row0, SLOT_WIN), :],
        stack.at[slot, pl.ds(e * SLOT_WIN, SLOT_WIN), :],
        sem.at[slot])


def _gate_matrix(lpt_ref, gate_ref, round_base):
    tm = lpt_ref.shape[0]
    lane = lax.broadcasted_iota(jnp.int32, (tm, LANES), 1).astype(F32)
    per_vreg = LANES // SLOT_WIN
    cols = []
    for v in range(N_EXPERTS // per_vreg):
        acc = jnp.zeros((tm, LANES), F32)
        for k in range(per_vreg):
            e = v * per_vreg + k
            local = lpt_ref[:, e:e + 1] - round_base
            ok = (local >= 0.0) & (local < float(SLOT_WIN))
            hit = ok & ((local + float(k * SLOT_WIN)) == lane)
            acc = acc + jnp.where(hit, gate_ref[:, e:e + 1], 0.0)
        cols.append(acc.astype(BF16))
    return jnp.concatenate(cols, axis=1)


def _combine_kernel(start_ref, count_ref, x1_ref, lpt_ref, gate_ref, gfin_ref, ye_ref, y_ref,
                    stack, extra, sem, sem_extra):
    i = pl.program_id(0)
    n = pl.num_programs(0)
    slot = i % 2

    def first_slot(tile, e):
        return pl.multiple_of(start_ref[tile * N_EXPERTS + e], BF16_ROWS)

    @pl.when(i == 0)
    def _():
        for e in range(N_EXPERTS):
            _combine_copy(ye_ref, stack, sem, 0, e, first_slot(0, e)).start()

    @pl.when(i + 1 < n)
    def _():
        for e in range(N_EXPERTS):
            _combine_copy(ye_ref, stack, sem, 1 - slot, e, first_slot(i + 1, e)).start()

    for e in range(N_EXPERTS):
        _combine_copy(ye_ref, stack, sem, slot, e, first_slot(i, e)).wait()

    moe = jnp.dot(_gate_matrix(lpt_ref, gate_ref, 0.0), stack[slot], preferred_element_type=F32)

    max_count = count_ref[i * N_EXPERTS]
    for e in range(1, N_EXPERTS):
        max_count = jnp.maximum(max_count, count_ref[i * N_EXPERTS + e])
    n_rounds = (max_count + SLOT_WIN - 1) // SLOT_WIN

    def extra_round(k, acc):
        for e in range(N_EXPERTS):
            cp = pltpu.make_async_copy(
                ye_ref.at[e, pl.ds(first_slot(i, e) + k * SLOT_WIN, SLOT_WIN), :],
                extra.at[pl.ds(e * SLOT_WIN, SLOT_WIN), :], sem_extra.at[0])
            cp.start()
            cp.wait()
        base = (k * SLOT_WIN).astype(F32)
        return acc + jnp.dot(_gate_matrix(lpt_ref, gate_ref, base), extra[...],
                             preferred_element_type=F32)

    moe = lax.fori_loop(1, n_rounds, extra_round, moe)
    y_ref[...] = _rms(x1_ref[...] + moe, gfin_ref[...])


def _combine(start, count, x1, lpt, gate, g_final, ye):
    t = x1.shape[0]
    tm = MOE_TILE
    row = lambda i, s, c: (i, 0)
    return pl.pallas_call(
        _combine_kernel,
        grid_spec=pltpu.PrefetchScalarGridSpec(
            num_scalar_prefetch=2,
            grid=(t // tm,),
            in_specs=[
                pl.BlockSpec((tm, D_MODEL), row),
                pl.BlockSpec((tm, LANES), row),
                pl.BlockSpec((tm, LANES), row),
                pl.BlockSpec((1, D_MODEL), lambda i, s, c: (0, 0)),
                pl.BlockSpec(memory_space=pl.ANY),
            ],
            out_specs=pl.BlockSpec((tm, D_MODEL), row),
            scratch_shapes=[pltpu.VMEM((2, N_EXPERTS * SLOT_WIN, D_MODEL), BF16),
                            pltpu.VMEM((N_EXPERTS * SLOT_WIN, D_MODEL), BF16),
                            pltpu.SemaphoreType.DMA((2,)),
                            pltpu.SemaphoreType.DMA((1,))],
        ),
        out_shape=jax.ShapeDtypeStruct((t, D_MODEL), F32),
        compiler_params=pltpu.CompilerParams(dimension_semantics=("arbitrary",)),
        name="combine",
    )(start, count, x1, lpt, gate, g_final, ye)


def _rope_tables(seq):
    half = HEAD_DIM // 2
    inv_freq = jnp.power(ROPE_THETA, -jnp.arange(half, dtype=F32) * 2.0 / HEAD_DIM)
    ang = jnp.arange(seq, dtype=F32)[:, None] * inv_freq[None, :]
    cos, sin = jnp.cos(ang), jnp.sin(ang)
    reps = LANES // HEAD_DIM
    return (jnp.tile(jnp.concatenate([cos, cos], axis=1), (1, reps)),
            jnp.tile(jnp.concatenate([-sin, sin], axis=1), (1, reps)))


def _prepare_weights(g_mix, w_in, ln_b_g, ln_b_b, w_spatial, b_spatial, g_mem, w_mem_kv,
                     g_out, w_out, g_ffn, w_router, w_gate, w_up, w_down, g_final):
    scale = jnp.ones((w_in.shape[-1],), F32)
    scale = scale.at[:D_A].set(HEAD_DIM ** -0.5)
    scale = scale.at[3 * D_A + 2 * D_B:].set((D_C // N_HEADS_C) ** -0.5)
    head_of_lane = jnp.arange(D_A) // HEAD_DIM
    return dict(
        g_mix=g_mix[0][None], w_in=(w_in[0] * scale).astype(BF16),
        ln_g=ln_b_g[0][None], ln_b=ln_b_b[0][None],
        w_sp=w_spatial[0].astype(BF16),
        b_sp=jnp.repeat(b_spatial[0].T, D_B // N_GROUPS_B, axis=1),
        g_mem=g_mem[0][None], w_kv=w_mem_kv[0].astype(BF16),
        expand=(jnp.arange(LANES)[:, None] == head_of_lane[None, :]).astype(BF16),
        g_out=g_out[0][None], w_out=w_out[0].astype(BF16),
        g_ffn=g_ffn[0][None],
        w_router=jnp.pad(w_router[0], ((0, 0), (0, LANES - N_EXPERTS))).astype(BF16),
        w_gate=w_gate[0].astype(BF16), w_up=w_up[0].astype(BF16), w_down=w_down[0].astype(BF16),
        g_final=g_final[None],
    )


def _encoder(x, mem, w):
    b, seq, _ = x.shape
    t = b * seq
    x2 = x.reshape(t, D_MODEL)
    cos_t, sin_t = _rope_tables(seq)
    mem_k, mem_v = _mem_kv(mem, w["g_mem"], w["w_kv"])
    q, k, v, u, vb, qc = _in_proj(x2, w["g_mix"], w["w_in"], cos_t, sin_t, w["ln_g"], w["ln_b"], seq)
    pats = [_banded_attention(q, k, v, b, seq, d) for d in DILATIONS]
    x1, xn, aff, afft = _mix(x2, [p[0] for p in pats], [p[1] for p in pats], u, vb, qc,
                             mem_k, mem_v, w["w_sp"], w["b_sp"], w["expand"], w["g_out"],
                             w["w_out"], w["g_ffn"], w["w_router"], seq)

    cap = EC_CAPACITY_FACTOR * t // N_EXPERTS
    n_tiles = t // MOE_TILE
    thr_col, quota_col, thr_row, quota_row = _threshold(afft, cap)
    lp, lpt, gate, start, total = _positions(afft, aff, thr_col, quota_col, thr_row, quota_row)
    start_i = start[:, :, 0].astype(jnp.int32).reshape(-1)
    used = total[:, 0].astype(jnp.int32)
    nxt = jnp.concatenate([start[1:, :, 0], total[None, :, 0]], axis=0)
    count_i = (nxt - start[:, :, 0]).astype(jnp.int32).reshape(-1)
    worst_pad = (BF16_ROWS - 1) * n_tiles + SLOT_WIN
    cap_rows = -(-(cap + worst_pad) // FFN_BLOCK) * FFN_BLOCK
    xe = _dispatch(start_i, count_i, xn, lp, cap_rows)
    ye = _ffn(used, xe, w["w_gate"], w["w_up"], w["w_down"])
    y = _combine(start_i, count_i, x1, lpt, gate, w["g_final"], ye)
    return y.reshape(b, seq, D_MODEL)


def kernel(x_prompt, x_sample, mem_prompt, mem_sample, g_mix, w_in, ln_b_g, ln_b_b, w_spatial,
           b_spatial, g_mem, w_mem_kv, g_out, w_out, g_ffn, w_router, w_gate, w_up, w_down, g_final):
    w = _prepare_weights(g_mix, w_in, ln_b_g, ln_b_b, w_spatial, b_spatial, g_mem, w_mem_kv,
                         g_out, w_out, g_ffn, w_router, w_gate, w_up, w_down, g_final)
    return (_encoder(x_prompt, mem_prompt, w), _encoder(x_sample, mem_sample, w))
```

```python
import functools

import jax
import jax.numpy as jnp
from jax import lax
from jax.experimental import pallas as pl
from jax.experimental.pallas import tpu as pltpu

F32 = jnp.float32
BF16 = jnp.bfloat16

D_MODEL = 1024
N_HEADS_A = 8
HEAD_DIM = 64
D_A = 512
D_B = 256
N_GROUPS_B = 4
CHUNK_B = 128
D_C = 256
N_HEADS_C = 4
N_MEM = 256
N_EXPERTS = 16
EC_CAPACITY_FACTOR = 2
DILATIONS = (1, 4, 16)
RADIUS = 64
Q_BLOCK = 128
ROPE_THETA = 10000.0
EPS = 1e-6
NEG_INF = -1e30

LANES = 128
BF16_ROWS = 16
TOKEN_TILE = 512
ATTN_TILE = 512
MOE_TILE = 256
SLOT_WIN = 64
FFN_BLOCK = 512
UNSELECTED = -4096.0
BINADE_STEPS = (64, 32, 16, 8, 4, 2, 1)
MANTISSA_STEPS = 52


def _rms(x, g):
    return x * lax.rsqrt(jnp.mean(x * x, axis=-1, keepdims=True) + EPS) * g


def _mem_kv_kernel(mem_ref, g_ref, w_ref, k_ref, v_ref):
    h = _rms(mem_ref[0], g_ref[...]).astype(BF16)
    kv = jnp.dot(h, w_ref[...], preferred_element_type=F32)
    k_ref[0] = kv[:, :D_C].astype(BF16)
    v_ref[0] = kv[:, D_C:].astype(BF16)


def _mem_kv(mem, g_mem, w_kv):
    b = mem.shape[0]
    return pl.pallas_call(
        _mem_kv_kernel,
        grid=(b,),
        in_specs=[
            pl.BlockSpec((1, N_MEM, D_MODEL), lambda i: (i, 0, 0)),
            pl.BlockSpec((1, D_MODEL), lambda i: (0, 0)),
            pl.BlockSpec((D_MODEL, 2 * D_C), lambda i: (0, 0)),
        ],
        out_specs=[
            pl.BlockSpec((1, N_MEM, D_C), lambda i: (i, 0, 0)),
            pl.BlockSpec((1, N_MEM, D_C), lambda i: (i, 0, 0)),
        ],
        out_shape=[jax.ShapeDtypeStruct((b, N_MEM, D_C), BF16)] * 2,
        name="mem_kv",
    )(mem, g_mem, w_kv)


def _in_proj_kernel(x_ref, g_ref, w_ref, cos_ref, sin_ref, lng_ref, lnb_ref,
                    q4_ref, k4_ref, v4_ref, q16_ref, k16_ref, v16_ref, u_ref, vb_ref, qc_ref, chunks):
    tm = x_ref.shape[0]

    def emit_by_residue(z, out4_ref, out16_ref):
        for c in range(D_A // LANES):
            chunks[c] = z[:, c * LANES:(c + 1) * LANES]
        for dil, out_ref in ((DILATIONS[1], out4_ref), (DILATIONS[2], out16_ref)):
            for r in range(dil):
                rows = [chunks[c, pl.ds(r, tm // dil, stride=dil), :] for c in range(D_A // LANES)]
                out_ref[0, r] = jnp.concatenate(rows, axis=1).astype(BF16)

    h = _rms(x_ref[...], g_ref[...]).astype(BF16)
    cos = jnp.concatenate([cos_ref[...]] * (D_A // LANES), axis=1)
    sin = jnp.concatenate([sin_ref[...]] * (D_A // LANES), axis=1)
    lane = lax.broadcasted_iota(jnp.int32, (tm, D_A), 1)
    first_half = (lane & (HEAD_DIM - 1)) < (HEAD_DIM // 2)

    def rope(z):
        rot = jnp.where(first_half, pltpu.roll(z, D_A - HEAD_DIM // 2, 1),
                        pltpu.roll(z, HEAD_DIM // 2, 1))
        return z * cos + rot * sin

    def proj(lo, hi):
        return jnp.dot(h, w_ref[:, lo:hi], preferred_element_type=F32)

    emit_by_residue(rope(proj(0, D_A)), q4_ref, q16_ref)
    emit_by_residue(rope(proj(D_A, 2 * D_A)), k4_ref, k16_ref)
    emit_by_residue(proj(2 * D_A, 3 * D_A), v4_ref, v16_ref)
    u_ref[...] = jax.nn.gelu(proj(3 * D_A, 3 * D_A + D_B)).astype(BF16)
    vb = jax.nn.gelu(proj(3 * D_A + D_B, 3 * D_A + 2 * D_B))
    mu = jnp.mean(vb, axis=-1, keepdims=True)
    var = jnp.mean(jnp.square(vb - mu), axis=-1, keepdims=True)
    vb_ref[...] = ((vb - mu) * lax.rsqrt(var + EPS) * lng_ref[...] + lnb_ref[...]).astype(BF16)
    qc_ref[...] = proj(3 * D_A + 2 * D_B, 3 * D_A + 2 * D_B + D_C).astype(BF16)


def _residue_spec(dil, tm, tiles_per_seq, width):
    return pl.BlockSpec((1, dil, tm // dil, width),
                        lambda i: (i // tiles_per_seq, 0, i % tiles_per_seq, 0))


def _in_proj(x2, g_mix, w_in, cos_t, sin_t, ln_g, ln_b, b, seq):
    t = x2.shape[0]
    tm = TOKEN_TILE
    d_in = w_in.shape[1]
    tiles_per_seq = seq // tm
    row = lambda i: (i, 0)
    const = lambda i: (0, 0)
    pos = lambda i: (i % tiles_per_seq, 0)
    d4, d16 = DILATIONS[1], DILATIONS[2]
    res4 = _residue_spec(d4, tm, tiles_per_seq, D_A)
    res16 = _residue_spec(d16, tm, tiles_per_seq, D_A)
    shape4 = jax.ShapeDtypeStruct((b, d4, seq // d4, D_A), BF16)
    shape16 = jax.ShapeDtypeStruct((b, d16, seq // d16, D_A), BF16)
    return pl.pallas_call(
        _in_proj_kernel,
        grid=(t // tm,),
        in_specs=[
            pl.BlockSpec((tm, D_MODEL), row),
            pl.BlockSpec((1, D_MODEL), const),
            pl.BlockSpec((D_MODEL, d_in), const),
            pl.BlockSpec((tm, LANES), pos),
            pl.BlockSpec((tm, LANES), pos),
            pl.BlockSpec((1, D_B), const),
            pl.BlockSpec((1, D_B), const),
        ],
        out_specs=[res4] * 3 + [res16] * 3 + [pl.BlockSpec((tm, w), row) for w in (D_B, D_B, D_C)],
        out_shape=[shape4] * 3 + [shape16] * 3
                  + [jax.ShapeDtypeStruct((t, w), BF16) for w in (D_B, D_B, D_C)],
        scratch_shapes=[pltpu.VMEM((D_A // LANES, tm, LANES), F32)],
        name="in_proj",
    )(x2, g_mix, w_in, cos_t, sin_t, ln_g, ln_b)


def _attn_kernel(q_ref, kp_ref, kc_ref, kn_ref, vp_ref, vc_ref, vn_ref,
                 o_ref, lse_ref, kwin, vwin, *, nres, tr, sub):
    j = pl.program_id(2)
    halo = RADIUS // nres
    qb = Q_BLOCK // nres
    for win, (p_ref, c_ref, n_ref) in ((kwin, (kp_ref, kc_ref, kn_ref)), (vwin, (vp_ref, vc_ref, vn_ref))):
        win[:, 0:halo] = p_ref[0]
        win[:, halo:halo + tr] = c_ref[0]
        win[:, halo + tr:] = n_ref[0]

    t_idx = lax.broadcasted_iota(jnp.int32, (Q_BLOCK, 2 * Q_BLOCK), 0)
    s_idx = lax.broadcasted_iota(jnp.int32, (Q_BLOCK, 2 * Q_BLOCK), 1)
    q_res, q_row = t_idx >> (qb.bit_length() - 1), t_idx & (qb - 1)
    k_res, k_row = s_idx >> qb.bit_length(), s_idx & (2 * qb - 1)
    band = jnp.abs(nres * (q_row - k_row + halo) + q_res - k_res) <= RADIUS
    k_elem = nres * (k_row - halo) + k_res
    lane = lax.broadcasted_iota(jnp.int32, (Q_BLOCK, LANES), 1)
    low_half = lane < HEAD_DIM

    def block(i, carry):
        r0 = pl.multiple_of(i * qb, qb)
        kpos = k_elem + nres * (j * tr + r0)
        bias = jnp.where(band & (kpos >= 0) & (kpos < sub), 0.0, NEG_INF)
        lse_blk = jnp.zeros((Q_BLOCK, LANES), F32)
        for hp in range(N_HEADS_A // 2):
            cols = slice(hp * LANES, (hp + 1) * LANES)
            stack = lambda pieces: pieces[0] if nres == 1 else jnp.concatenate(pieces, axis=0)
            q_pair = stack([q_ref[0, r, pl.ds(r0, qb), cols] for r in range(nres)])
            k_pair = stack([kwin[r, pl.ds(r0, 2 * qb), cols] for r in range(nres)])
            v_pair = stack([vwin[r, pl.ds(r0, 2 * qb), cols] for r in range(nres)])
            halves = []
            for half in range(2):
                keep = low_half if half == 0 else jnp.logical_not(low_half)
                qm = jnp.where(keep, q_pair, jnp.zeros_like(q_pair))
                s = lax.dot_general(qm, k_pair, (((1,), (1,)), ((), ())),
                                    preferred_element_type=F32) + bias
                m = jnp.max(s, axis=-1, keepdims=True)
                p = jnp.exp(s - m)
                den = jnp.sum(p, axis=-1, keepdims=True)
                pv = jnp.dot(p.astype(BF16), v_pair, preferred_element_type=F32)
                halves.append(pv / den)
                lse_blk = jnp.where(lane == 2 * hp + half, m + jnp.log(den), lse_blk)
            out = jnp.where(low_half, halves[0], halves[1]).astype(BF16)
            for r in range(nres):
                o_ref[0, r, pl.ds(r0, qb), cols] = out[r * qb:(r + 1) * qb]
        for r in range(nres):
            lse_ref[0, r, pl.ds(r0, qb), :] = lse_blk[r * qb:(r + 1) * qb]
        return carry

    lax.fori_loop(0, tr // qb, block, 0)


def _banded_attention(q, k, v, nres, name):
    b, n_res, rows, _ = q.shape
    tr = min(ATTN_TILE // nres, rows)
    halo = RADIUS // nres
    halo_per_tile = tr // halo
    last_halo = rows // halo - 1
    if nres == 1:
        grid = (b, n_res, rows // tr)
        at = lambda row_block: (lambda bi, r, j: (bi, r, row_block(j), 0))
    else:
        grid = (b, 1, rows // tr)
        at = lambda row_block: (lambda bi, r, j: (bi, 0, row_block(j), 0))
    cur = pl.BlockSpec((1, nres, tr, D_A), at(lambda j: j))
    prev = pl.BlockSpec((1, nres, halo, D_A), at(lambda j: jnp.maximum(j * halo_per_tile - 1, 0)))
    nxt = pl.BlockSpec((1, nres, halo, D_A),
                       at(lambda j: jnp.minimum((j + 1) * halo_per_tile, last_halo)))
    return pl.pallas_call(
        functools.partial(_attn_kernel, nres=nres, tr=tr, sub=nres * rows),
        grid=grid,
        in_specs=[cur, prev, cur, nxt, prev, cur, nxt],
        out_specs=[cur, pl.BlockSpec((1, nres, tr, LANES), at(lambda j: j))],
        out_shape=[
            jax.ShapeDtypeStruct((b, n_res, rows, D_A), BF16),
            jax.ShapeDtypeStruct((b, n_res, rows, LANES), F32),
        ],
        scratch_shapes=[pltpu.VMEM((nres, tr + 2 * halo, D_A), BF16)] * 2,
        name=name,
    )(q, k, k, k, v, v, v)


def _split_bf16(x):
    hi = x.astype(BF16)
    lo = (x - hi.astype(F32)).astype(BF16)
    return hi, lo


def _mix_kernel(x_ref, o1_ref, o2_ref, o3_ref, l1_ref, l2_ref, l3_ref, u_ref, vb_ref, qc_ref,
                mk_ref, mv_ref, ws_ref, bs_ref, expand_ref, go_ref, wo_ref, gf_ref, wr_ref,
                x1_ref, xn_ref, aff_ref, afft_ref, obuf1, obuf2, obuf3, lbuf1, lbuf2, lbuf3):
    tm = x_ref.shape[0]

    def token_order(ref, buf):
        dil, width = ref.shape[1], ref.shape[3]
        for r in range(dil):
            piece = ref[0, r].astype(F32)
            for c in range(width // LANES):
                buf[c, pl.ds(r, tm // dil, stride=dil), :] = piece[:, c * LANES:(c + 1) * LANES]
        return jnp.concatenate([buf[c] for c in range(width // LANES)], axis=1)

    l1, l2, l3 = token_order(l1_ref, lbuf1), token_order(l2_ref, lbuf2), token_order(l3_ref, lbuf3)
    mx = jnp.maximum(jnp.maximum(l1, l2), l3)
    e1, e2, e3 = jnp.exp(l1 - mx), jnp.exp(l2 - mx), jnp.exp(l3 - mx)
    tot = e1 + e2 + e3
    o_a = jnp.zeros((tm, D_A), F32)
    for e, o_ref, buf in ((e1, o1_ref, obuf1), (e2, o2_ref, obuf2), (e3, o3_ref, obuf3)):
        hi, lo = _split_bf16(e / tot)
        w = (jnp.dot(hi, expand_ref[...], preferred_element_type=F32)
             + jnp.dot(lo, expand_ref[...], preferred_element_type=F32))
        o_a = o_a + w * token_order(o_ref, buf)

    lane_b = lax.broadcasted_iota(jnp.int32, (CHUNK_B, D_B), 1)
    group_w = D_B // N_GROUPS_B
    gated = []
    for c in range(tm // CHUNK_B):
        vchunk = vb_ref[c * CHUNK_B:(c + 1) * CHUNK_B, :]
        acc = bs_ref[...]
        for g in range(N_GROUPS_B):
            y = jnp.dot(ws_ref[g], vchunk, preferred_element_type=F32)
            acc = acc + jnp.where(lane_b // group_w == g, y, 0.0)
        gated.append(acc)
    o_b = u_ref[...].astype(F32) * jnp.concatenate(gated, axis=0)

    lane_c = lax.broadcasted_iota(jnp.int32, (tm, LANES), 1)
    low_half = lane_c < HEAD_DIM
    oc_parts = []
    for hp in range(N_HEADS_C // 2):
        cols = slice(hp * LANES, (hp + 1) * LANES)
        q_pair = qc_ref[:, cols]
        k_pair = mk_ref[0, :, cols]
        v_pair = mv_ref[0, :, cols]
        halves = []
        for half in range(2):
            keep = low_half if half == 0 else jnp.logical_not(low_half)
            qm = jnp.where(keep, q_pair, jnp.zeros_like(q_pair))
            s = lax.dot_general(qm, k_pair, (((1,), (1,)), ((), ())), preferred_element_type=F32)
            m = jnp.max(s, axis=-1, keepdims=True)
            p = jnp.exp(s - m)
            p = p / jnp.sum(p, axis=-1, keepdims=True)
            halves.append(jnp.dot(p.astype(BF16), v_pair, preferred_element_type=F32))
        oc_parts.append(jnp.where(low_half, halves[0], halves[1]))
    o_c = jnp.concatenate(oc_parts, axis=1)

    go = go_ref[...]
    o = jnp.concatenate([_rms(o_a, go[:, :D_A]), _rms(o_b, go[:, D_A:D_A + D_B]),
                         _rms(o_c, go[:, D_A + D_B:])], axis=1).astype(BF16)
    x1 = x_ref[...] + jnp.dot(o, wo_ref[...], preferred_element_type=F32)
    x1_ref[...] = x1

    xn = _rms(x1, gf_ref[...]).astype(BF16)
    xn_ref[...] = xn
    logits = jnp.dot(xn, wr_ref[...], preferred_element_type=F32)
    is_expert = lane_c < N_EXPERTS
    logits = jnp.where(is_expert, logits, NEG_INF)
    m = jnp.max(logits, axis=-1, keepdims=True)
    ex = jnp.where(is_expert, jnp.exp(logits - m), 0.0)
    aff = ex / jnp.sum(ex, axis=-1, keepdims=True)
    aff_ref[...] = aff
    afft_ref[...] = jnp.transpose(aff)[:N_EXPERTS, :]


def _mix(x2, o_pats, lse_pats, u, vb, qc, mem_k, mem_v, w_sp, b_sp, expand, g_out, w_out,
         g_ffn, w_router, seq):
    t = x2.shape[0]
    tm = TOKEN_TILE
    tiles_per_seq = seq // tm
    row = lambda i: (i, 0)
    const2 = lambda i: (0, 0)
    const3 = lambda i: (0, 0, 0)
    batch = lambda i: (i // tiles_per_seq, 0, 0)
    dils = [o.shape[1] for o in o_pats]
    return pl.pallas_call(
        _mix_kernel,
        grid=(t // tm,),
        in_specs=[
            pl.BlockSpec((tm, D_MODEL), row),
            *[_residue_spec(d, tm, tiles_per_seq, D_A) for d in dils],
            *[_residue_spec(d, tm, tiles_per_seq, LANES) for d in dils],
            pl.BlockSpec((tm, D_B), row), pl.BlockSpec((tm, D_B), row), pl.BlockSpec((tm, D_C), row),
            pl.BlockSpec((1, N_MEM, D_C), batch), pl.BlockSpec((1, N_MEM, D_C), batch),
            pl.BlockSpec((N_GROUPS_B, CHUNK_B, CHUNK_B), const3),
            pl.BlockSpec((CHUNK_B, D_B), const2),
            pl.BlockSpec((LANES, D_A), const2),
            pl.BlockSpec((1, D_MODEL), const2),
            pl.BlockSpec((D_MODEL, D_MODEL), const2),
            pl.BlockSpec((1, D_MODEL), const2),
            pl.BlockSpec((D_MODEL, LANES), const2),
        ],
        out_specs=[
            pl.BlockSpec((tm, D_MODEL), row),
            pl.BlockSpec((tm, D_MODEL), row),
            pl.BlockSpec((tm, LANES), row),
            pl.BlockSpec((N_EXPERTS, tm), lambda i: (0, i)),
        ],
        out_shape=[
            jax.ShapeDtypeStruct((t, D_MODEL), F32),
            jax.ShapeDtypeStruct((t, D_MODEL), BF16),
            jax.ShapeDtypeStruct((t, LANES), F32),
            jax.ShapeDtypeStruct((N_EXPERTS, t), F32),
        ],
        scratch_shapes=[pltpu.VMEM((D_A // LANES, tm, LANES), F32)] * 3
                       + [pltpu.VMEM((1, tm, LANES), F32)] * 3,
        name="mix",
    )(x2, *o_pats, *lse_pats, u, vb, qc, mem_k, mem_v, w_sp, b_sp, expand, g_out, w_out,
      g_ffn, w_router)


def _threshold_kernel(afft_ref, thr_col_ref, quota_col_ref, thr_row_ref, quota_row_ref, *, cap):
    aff = afft_ref[...]

    def enough(v):
        return jnp.sum(jnp.where(aff >= v, 1.0, 0.0), axis=-1, keepdims=True) >= cap

    hi = jnp.full((N_EXPERTS, 1), 2.0, F32)
    for shift in BINADE_STEPS:
        cand = hi * (2.0 ** -shift)
        hi = jnp.where(enough(cand), hi, cand)
    lo = jnp.where(enough(hi * 0.5), hi * 0.5, 0.0)

    def bisect(_, bracket):
        lo, hi = bracket
        mid = (lo + hi) * 0.5
        ok = enough(mid)
        return jnp.where(ok, mid, lo), jnp.where(ok, hi, mid)

    lo, hi = lax.fori_loop(0, MANTISSA_STEPS, bisect, (lo, hi))
    thr_f = jnp.min(jnp.where(aff >= lo, aff, jnp.inf), axis=-1, keepdims=True)
    n_gt = jnp.sum(jnp.where(aff > thr_f, 1.0, 0.0), axis=-1, keepdims=True)
    quota = cap - n_gt
    thr_col = jnp.broadcast_to(thr_f, (N_EXPERTS, LANES))
    quota_col = jnp.broadcast_to(quota, (N_EXPERTS, LANES))
    thr_col_ref[...] = thr_col
    quota_col_ref[...] = quota_col
    diag = (lax.broadcasted_iota(jnp.int32, (N_EXPERTS, LANES), 0)
            == lax.broadcasted_iota(jnp.int32, (N_EXPERTS, LANES), 1))
    thr_row = jnp.sum(jnp.where(diag, thr_col, 0.0), axis=0, keepdims=True)
    quota_row = jnp.sum(jnp.where(diag, quota_col, 0.0), axis=0, keepdims=True)
    thr_row_ref[...] = jnp.broadcast_to(thr_row, (8, LANES))
    quota_row_ref[...] = jnp.broadcast_to(quota_row, (8, LANES))


def _threshold(afft, cap):
    t = afft.shape[1]
    full = lambda shape: pl.BlockSpec(shape, lambda: (0,) * len(shape))
    return pl.pallas_call(
        functools.partial(_threshold_kernel, cap=float(cap)),
        in_specs=[full((N_EXPERTS, t))],
        out_specs=[full((N_EXPERTS, LANES)), full((N_EXPERTS, LANES)),
                   full((8, LANES)), full((8, LANES))],
        out_shape=[jax.ShapeDtypeStruct((N_EXPERTS, LANES), F32)] * 2
                  + [jax.ShapeDtypeStruct((8, LANES), F32)] * 2,
        name="threshold",
    )(afft)


def _positions_kernel(afft_ref, aff_ref, thr_col_ref, quota_col_ref, thr_row_ref, quota_row_ref,
                      lp_ref, lpt_ref, gate_ref, start_ref, total_ref,
                      eq_col, eq_row, off_col):
    i = pl.program_id(0)
    tm = aff_ref.shape[0]

    @pl.when(i == 0)
    def _():
        eq_col[...] = jnp.zeros_like(eq_col)
        eq_row[...] = jnp.zeros_like(eq_row)
        off_col[...] = jnp.zeros_like(off_col)

    r = lax.broadcasted_iota(jnp.int32, (tm, tm), 0)
    c = lax.broadcasted_iota(jnp.int32, (tm, tm), 1)
    before = jnp.where(r < c, 1.0, 0.0).astype(BF16)
    after = jnp.where(c < r, 1.0, 0.0).astype(BF16)

    a = afft_ref[...]
    thr, quota = thr_col_ref[:, 0:1], quota_col_ref[:, 0:1]
    eq = a == thr
    eq_f = jnp.where(eq, 1.0, 0.0)
    eq_rank = jnp.dot(eq_f.astype(BF16), before, preferred_element_type=F32) + eq_col[:, 0:1]
    sel = (a > thr) | (eq & (eq_rank < quota))
    sel_f = jnp.where(sel, 1.0, 0.0)
    lp = jnp.dot(sel_f.astype(BF16), before, preferred_element_type=F32)
    lp_ref[...] = jnp.where(sel, lp, UNSELECTED)
    count = jnp.sum(sel_f, axis=-1, keepdims=True)
    start = off_col[...]
    start_ref[0] = start
    padded = jnp.ceil(count / BF16_ROWS) * BF16_ROWS
    new_off = start + padded
    off_col[...] = new_off
    total_ref[...] = new_off
    eq_col[...] = eq_col[...] + jnp.sum(eq_f, axis=-1, keepdims=True)

    at = aff_ref[...]
    thr_r, quota_r = thr_row_ref[0:1, :], quota_row_ref[0:1, :]
    is_expert = lax.broadcasted_iota(jnp.int32, (tm, LANES), 1) < N_EXPERTS
    eq_t = (at == thr_r) & is_expert
    eq_tf = jnp.where(eq_t, 1.0, 0.0)
    eq_rank_t = jnp.dot(after, eq_tf.astype(BF16), preferred_element_type=F32) + eq_row[0:1, :]
    sel_t = ((at > thr_r) & is_expert) | (eq_t & (eq_rank_t < quota_r))
    sel_tf = jnp.where(sel_t, 1.0, 0.0)
    lp_t = jnp.dot(after, sel_tf.astype(BF16), preferred_element_type=F32)
    lpt_ref[...] = jnp.where(sel_t, lp_t, UNSELECTED)
    gate_ref[...] = jnp.where(sel_t, at, 0.0)
    eq_row[...] = eq_row[...] + jnp.sum(eq_tf, axis=0, keepdims=True)


def _positions(afft, aff, thr_col, quota_col, thr_row, quota_row):
    t = aff.shape[0]
    tm = MOE_TILE
    n_tiles = t // tm
    c2 = lambda i: (0, 0)
    return pl.pallas_call(
        _positions_kernel,
        grid=(n_tiles,),
        in_specs=[
            pl.BlockSpec((N_EXPERTS, tm), lambda i: (0, i)),
            pl.BlockSpec((tm, LANES), lambda i: (i, 0)),
            pl.BlockSpec((N_EXPERTS, LANES), c2), pl.BlockSpec((N_EXPERTS, LANES), c2),
            pl.BlockSpec((8, LANES), c2), pl.BlockSpec((8, LANES), c2),
        ],
        out_specs=[
            pl.BlockSpec((N_EXPERTS, tm), lambda i: (0, i)),
            pl.BlockSpec((tm, LANES), lambda i: (i, 0)),
            pl.BlockSpec((tm, LANES), lambda i: (i, 0)),
            pl.BlockSpec((1, N_EXPERTS, LANES), lambda i: (i, 0, 0)),
            pl.BlockSpec((N_EXPERTS, LANES), c2),
        ],
        out_shape=[
            jax.ShapeDtypeStruct((N_EXPERTS, t), F32),
            jax.ShapeDtypeStruct((t, LANES), F32),
            jax.ShapeDtypeStruct((t, LANES), F32),
            jax.ShapeDtypeStruct((n_tiles, N_EXPERTS, LANES), F32),
            jax.ShapeDtypeStruct((N_EXPERTS, LANES), F32),
        ],
        scratch_shapes=[pltpu.VMEM((N_EXPERTS, LANES), F32), pltpu.VMEM((8, LANES), F32),
                        pltpu.VMEM((N_EXPERTS, LANES), F32)],
        compiler_params=pltpu.CompilerParams(dimension_semantics=("arbitrary",)),
        name="positions",
    )(afft, aff, thr_col, quota_col, thr_row, quota_row)


def _one_hot_rows(lp_ref, round_base):
    tm = lp_ref.shape[1]
    q = lax.broadcasted_iota(jnp.int32, (SLOT_WIN, tm), 0).astype(F32) + round_base
    rows = [jnp.where(lp_ref[e:e + 1, :] == q, 1.0, 0.0).astype(BF16) for e in range(N_EXPERTS)]
    return jnp.concatenate(rows, axis=0)


def _dispatch_copy(stack, xe_ref, sem, slot, e, row0):
    return pltpu.make_async_copy(
        stack.at[slot, pl.ds(e * SLOT_WIN, SLOT_WIN), :],
        xe_ref.at[e, pl.ds(row0, SLOT_WIN), :],
        sem.at[slot])


def _dispatch_kernel(start_ref, count_ref, xn_ref, lp_ref, xe_ref, stack, extra, sem, sem_extra):
    i = pl.program_id(0)
    n = pl.num_programs(0)
    slot = i % 2

    def first_slot(tile, e):
        return pl.multiple_of(start_ref[tile * N_EXPERTS + e], BF16_ROWS)

    rows = jnp.dot(_one_hot_rows(lp_ref, 0.0), xn_ref[...], preferred_element_type=F32)
    stack[slot] = rows.astype(BF16)

    @pl.when(i > 0)
    def _():
        for e in range(N_EXPERTS):
            _dispatch_copy(stack, xe_ref, sem, 1 - slot, e, first_slot(i - 1, e)).wait()

    for e in range(N_EXPERTS):
        _dispatch_copy(stack, xe_ref, sem, slot, e, first_slot(i, e)).start()

    max_count = count_ref[i * N_EXPERTS]
    for e in range(1, N_EXPERTS):
        max_count = jnp.maximum(max_count, count_ref[i * N_EXPERTS + e])
    n_rounds = (max_count + SLOT_WIN - 1) // SLOT_WIN

    def extra_round(k, carry):
        base = (k * SLOT_WIN).astype(F32)
        rows_k = jnp.dot(_one_hot_rows(lp_ref, base), xn_ref[...], preferred_element_type=F32)
        extra[...] = rows_k.astype(BF16)
        for e in range(N_EXPERTS):
            @pl.when(count_ref[i * N_EXPERTS + e] > k * SLOT_WIN)
            def _():
                cp = pltpu.make_async_copy(
                    extra.at[pl.ds(e * SLOT_WIN, SLOT_WIN), :],
                    xe_ref.at[e, pl.ds(first_slot(i, e) + k * SLOT_WIN, SLOT_WIN), :],
                    sem_extra.at[0])
                cp.start()
                cp.wait()
        return carry

    lax.fori_loop(1, n_rounds, extra_round, 0)

    @pl.when(i == n - 1)
    def _():
        for e in range(N_EXPERTS):
            _dispatch_copy(stack, xe_ref, sem, slot, e, first_slot(i, e)).wait()
        extra[...] = jnp.zeros_like(extra)
        cap_rows = xe_ref.shape[1]

        def fill(e, row0, rows):
            return pltpu.make_async_copy(extra.at[pl.ds(0, rows), :],
                                         xe_ref.at[e, pl.ds(row0, rows), :], sem_extra.at[0])

        for e in range(N_EXPERTS):
            rounds = jnp.maximum((count_ref[i * N_EXPERTS + e] + SLOT_WIN - 1) // SLOT_WIN, 1)
            w_end = first_slot(i, e) + rounds * SLOT_WIN
            n_small = ((-w_end) & (SLOT_WIN - 1)) // BF16_ROWS
            base = w_end + n_small * BF16_ROWS
            n_big = (cap_rows - base) // SLOT_WIN

            def small_row(k):
                return pl.multiple_of(w_end + k * BF16_ROWS, BF16_ROWS)

            def big_row(k):
                return pl.multiple_of(base + k * SLOT_WIN, SLOT_WIN)

            lax.fori_loop(0, n_small, lambda k, c: (fill(e, small_row(k), BF16_ROWS).start(), c)[1], 0)
            lax.fori_loop(0, n_big, lambda k, c: (fill(e, big_row(k), SLOT_WIN).start(), c)[1], 0)
            lax.fori_loop(0, n_small, lambda k, c: (fill(e, small_row(k), BF16_ROWS).wait(), c)[1], 0)
            lax.fori_loop(0, n_big, lambda k, c: (fill(e, big_row(k), SLOT_WIN).wait(), c)[1], 0)


def _dispatch(start, count, xn, lp, cap_rows):
    t = xn.shape[0]
    tm = MOE_TILE
    return pl.pallas_call(
        _dispatch_kernel,
        grid_spec=pltpu.PrefetchScalarGridSpec(
            num_scalar_prefetch=2,
            grid=(t // tm,),
            in_specs=[
                pl.BlockSpec((tm, D_MODEL), lambda i, s, c: (i, 0)),
                pl.BlockSpec((N_EXPERTS, tm), lambda i, s, c: (0, i)),
            ],
            out_specs=pl.BlockSpec(memory_space=pl.ANY),
            scratch_shapes=[pltpu.VMEM((2, N_EXPERTS * SLOT_WIN, D_MODEL), BF16),
                            pltpu.VMEM((N_EXPERTS * SLOT_WIN, D_MODEL), BF16),
                            pltpu.SemaphoreType.DMA((2,)),
                            pltpu.SemaphoreType.DMA((1,))],
        ),
        out_shape=jax.ShapeDtypeStruct((N_EXPERTS, cap_rows, D_MODEL), BF16),
        compiler_params=pltpu.CompilerParams(dimension_semantics=("arbitrary",)),
        name="dispatch",
    )(start, count, xn, lp)


def _ffn_kernel(used_ref, xe_ref, wg_ref, wu_ref, wd_ref, ye_ref):
    e, j = pl.program_id(0), pl.program_id(1)
    bs = xe_ref.shape[1]
    n_valid = used_ref[e] - j * bs

    @pl.when(n_valid > 0)
    def _():
        row = lax.broadcasted_iota(jnp.int32, (bs, D_MODEL), 0)
        x = jnp.where(row < n_valid, xe_ref[0], jnp.zeros((bs, D_MODEL), BF16))
        gate = jnp.dot(x, wg_ref[0], preferred_element_type=F32)
        up = jnp.dot(x, wu_ref[0], preferred_element_type=F32)
        hdn = (jax.nn.silu(gate) * up).astype(BF16)
        ye_ref[0] = jnp.dot(hdn, wd_ref[0], preferred_element_type=F32).astype(BF16)

    @pl.when(n_valid <= 0)
    def _():
        ye_ref[0] = jnp.zeros((bs, D_MODEL), BF16)


def _ffn(used, xe, w_gate, w_up, w_down):
    cap_rows = xe.shape[1]
    bs = FFN_BLOCK

    def x_map(e, j, used_ref):
        last = jnp.maximum((used_ref[e] + bs - 1) // bs - 1, 0)
        return (e, jnp.minimum(j, last), 0)

    w_map = lambda e, j, used_ref: (e, 0, 0)
    return pl.pallas_call(
        _ffn_kernel,
        grid_spec=pltpu.PrefetchScalarGridSpec(
            num_scalar_prefetch=1,
            grid=(N_EXPERTS, cap_rows // bs),
            in_specs=[
                pl.BlockSpec((1, bs, D_MODEL), x_map),
                pl.BlockSpec((1, D_MODEL, D_MODEL), w_map),
                pl.BlockSpec((1, D_MODEL, D_MODEL), w_map),
                pl.BlockSpec((1, D_MODEL, D_MODEL), w_map),
            ],
            out_specs=pl.BlockSpec((1, bs, D_MODEL), lambda e, j, used_ref: (e, j, 0)),
        ),
        out_shape=jax.ShapeDtypeStruct((N_EXPERTS, cap_rows, D_MODEL), BF16),
        name="expert_ffn",
    )(used, xe, w_gate, w_up, w_down)


def _combine_copy(ye_ref, stack, sem, slot, e, row0):
    return pltpu.make_async_copy(
        ye_ref.at[e, pl.ds(row0, SLOT_WIN), :],
        stack.at[slot, pl.ds(e * SLOT_WIN, SLOT_WIN), :],
        sem.at[slot])


def _gate_matrix(lpt_ref, gate_ref, round_base):
    tm = lpt_ref.shape[0]
    lane = lax.broadcasted_iota(jnp.int32, (tm, LANES), 1).astype(F32)
    per_vreg = LANES // SLOT_WIN
    cols = []
    for v in range(N_EXPERTS // per_vreg):
        acc = jnp.zeros((tm, LANES), F32)
        for k in range(per_vreg):
            e = v * per_vreg + k
            local = lpt_ref[:, e:e + 1] - round_base
            ok = (local >= 0.0) & (local < float(SLOT_WIN))
            hit = ok & ((local + float(k * SLOT_WIN)) == lane)
            acc = acc + jnp.where(hit, gate_ref[:, e:e + 1], 0.0)
        cols.append(acc.astype(BF16))
    return jnp.concatenate(cols, axis=1)


def _combine_kernel(start_ref, count_ref, x1_ref, lpt_ref, gate_ref, gfin_ref, ye_ref, y_ref,
                    stack, extra, sem, sem_extra):
    i = pl.program_id(0)
    n = pl.num_programs(0)
    slot = i % 2

    def first_slot(tile, e):
        return pl.multiple_of(start_ref[tile * N_EXPERTS + e], BF16_ROWS)

    @pl.when(i == 0)
    def _():
        for e in range(N_EXPERTS):
            _combine_copy(ye_ref, stack, sem, 0, e, first_slot(0, e)).start()

    @pl.when(i + 1 < n)
    def _():
        for e in range(N_EXPERTS):
            _combine_copy(ye_ref, stack, sem, 1 - slot, e, first_slot(i + 1, e)).start()

    for e in range(N_EXPERTS):
        _combine_copy(ye_ref, stack, sem, slot, e, first_slot(i, e)).wait()

    moe = jnp.dot(_gate_matrix(lpt_ref, gate_ref, 0.0), stack[slot], preferred_element_type=F32)

    max_count = count_ref[i * N_EXPERTS]
    for e in range(1, N_EXPERTS):
        max_count = jnp.maximum(max_count, count_ref[i * N_EXPERTS + e])
    n_rounds = (max_count + SLOT_WIN - 1) // SLOT_WIN

    def extra_round(k, acc):
        for e in range(N_EXPERTS):
            cp = pltpu.make_async_copy(
                ye_ref.at[e, pl.ds(first_slot(i, e) + k * SLOT_WIN, SLOT_WIN), :],
                extra.at[pl.ds(e * SLOT_WIN, SLOT_WIN), :], sem_extra.at[0])
            cp.start()
            cp.wait()
        base = (k * SLOT_WIN).astype(F32)
        return acc + jnp.dot(_gate_matrix(lpt_ref, gate_ref, base), extra[...],
                             preferred_element_type=F32)

    moe = lax.fori_loop(1, n_rounds, extra_round, moe)
    y_ref[...] = _rms(x1_ref[...] + moe, gfin_ref[...])


def _combine(start, count, x1, lpt, gate, g_final, ye):
    t = x1.shape[0]
    tm = MOE_TILE
    row = lambda i, s, c: (i, 0)
    return pl.pallas_call(
        _combine_kernel,
        grid_spec=pltpu.PrefetchScalarGridSpec(
            num_scalar_prefetch=2,
            grid=(t // tm,),
            in_specs=[
                pl.BlockSpec((tm, D_MODEL), row),
                pl.BlockSpec((tm, LANES), row),
                pl.BlockSpec((tm, LANES), row),
                pl.BlockSpec((1, D_MODEL), lambda i, s, c: (0, 0)),
                pl.BlockSpec(memory_space=pl.ANY),
            ],
            out_specs=pl.BlockSpec((tm, D_MODEL), row),
            scratch_shapes=[pltpu.VMEM((2, N_EXPERTS * SLOT_WIN, D_MODEL), BF16),
                            pltpu.VMEM((N_EXPERTS * SLOT_WIN, D_MODEL), BF16),
                            pltpu.SemaphoreType.DMA((2,)),
                            pltpu.SemaphoreType.DMA((1,))],
        ),
        out_shape=jax.ShapeDtypeStruct((t, D_MODEL), F32),
        compiler_params=pltpu.CompilerParams(dimension_semantics=("arbitrary",)),
        name="combine",
    )(start, count, x1, lpt, gate, g_final, ye)


def _rope_tables(seq):
    half = HEAD_DIM // 2
    inv_freq = jnp.power(ROPE_THETA, -jnp.arange(half, dtype=F32) * 2.0 / HEAD_DIM)
    ang = jnp.arange(seq, dtype=F32)[:, None] * inv_freq[None, :]
    cos, sin = jnp.cos(ang), jnp.sin(ang)
    reps = LANES // HEAD_DIM
    return (jnp.tile(jnp.concatenate([cos, cos], axis=1), (1, reps)),
            jnp.tile(jnp.concatenate([-sin, sin], axis=1), (1, reps)))


def _prepare_weights(g_mix, w_in, ln_b_g, ln_b_b, w_spatial, b_spatial, g_mem, w_mem_kv,
                     g_out, w_out, g_ffn, w_router, w_gate, w_up, w_down, g_final):
    scale = jnp.ones((w_in.shape[-1],), F32)
    scale = scale.at[:D_A].set(HEAD_DIM ** -0.5)
    scale = scale.at[3 * D_A + 2 * D_B:].set((D_C // N_HEADS_C) ** -0.5)
    head_of_lane = jnp.arange(D_A) // HEAD_DIM
    return dict(
        g_mix=g_mix[0][None], w_in=(w_in[0] * scale).astype(BF16),
        ln_g=ln_b_g[0][None], ln_b=ln_b_b[0][None],
        w_sp=w_spatial[0].astype(BF16),
        b_sp=jnp.repeat(b_spatial[0].T, D_B // N_GROUPS_B, axis=1),
        g_mem=g_mem[0][None], w_kv=w_mem_kv[0].astype(BF16),
        expand=(jnp.arange(LANES)[:, None] == head_of_lane[None, :]).astype(BF16),
        g_out=g_out[0][None], w_out=w_out[0].astype(BF16),
        g_ffn=g_ffn[0][None],
        w_router=jnp.pad(w_router[0], ((0, 0), (0, LANES - N_EXPERTS))).astype(BF16),
        w_gate=w_gate[0].astype(BF16), w_up=w_up[0].astype(BF16), w_down=w_down[0].astype(BF16),
        g_final=g_final[None],
    )


def _encoder(x, mem, w):
    b, seq, _ = x.shape
    t = b * seq
    x2 = x.reshape(t, D_MODEL)
    cos_t, sin_t = _rope_tables(seq)
    mem_k, mem_v = _mem_kv(mem, w["g_mem"], w["w_kv"])
    q4, k4, v4, q16, k16, v16, u, vb, qc = _in_proj(x2, w["g_mix"], w["w_in"], cos_t, sin_t,
                                                    w["ln_g"], w["ln_b"], b, seq)
    pats = [_banded_attention(q4, k4, v4, DILATIONS[1], "attn_d1"),
            _banded_attention(q4, k4, v4, 1, "attn_d4"),
            _banded_attention(q16, k16, v16, 1, "attn_d16")]
    x1, xn, aff, afft = _mix(x2, [p[0] for p in pats], [p[1] for p in pats], u, vb, qc,
                             mem_k, mem_v, w["w_sp"], w["b_sp"], w["expand"], w["g_out"],
                             w["w_out"], w["g_ffn"], w["w_router"], seq)

    cap = EC_CAPACITY_FACTOR * t // N_EXPERTS
    n_tiles = t // MOE_TILE
    thr_col, quota_col, thr_row, quota_row = _threshold(afft, cap)
    lp, lpt, gate, start, total = _positions(afft, aff, thr_col, quota_col, thr_row, quota_row)
    start_i = start[:, :, 0].astype(jnp.int32).reshape(-1)
    used = total[:, 0].astype(jnp.int32)
    nxt = jnp.concatenate([start[1:, :, 0], total[None, :, 0]], axis=0)
    count_i = (nxt - start[:, :, 0]).astype(jnp.int32).reshape(-1)
    worst_pad = (BF16_ROWS - 1) * n_tiles + SLOT_WIN
    n_blocks = -(-(cap + worst_pad) // FFN_BLOCK)
    n_blocks += 1 - n_blocks % 2
    cap_rows = n_blocks * FFN_BLOCK
    xe = _dispatch(start_i, count_i, xn, lp, cap_rows)
    ye = _ffn(used, xe, w["w_gate"], w["w_up"], w["w_down"])
    y = _combine(start_i, count_i, x1, lpt, gate, w["g_final"], ye)
    return y.reshape(b, seq, D_MODEL)


def kernel(x_prompt, x_sample, mem_prompt, mem_sample, g_mix, w_in, ln_b_g, ln_b_b, w_spatial,
           b_spatial, g_mem, w_mem_kv, g_out, w_out, g_ffn, w_router, w_gate, w_up, w_down, g_final):
    w = _prepare_weights(g_mix, w_in, ln_b_g, ln_b_b, w_spatial, b_spatial, g_mem, w_mem_kv,
                         g_out, w_out, g_ffn, w_router, w_gate, w_up, w_down, g_final)
    return (_encoder(x_prompt, mem_prompt, w), _encoder(x_sample, mem_sample, w))
```

```python
import functools

import jax
import jax.numpy as jnp
from jax import lax
from jax.experimental import pallas as pl
from jax.experimental.pallas import tpu as pltpu

F32 = jnp.float32
BF16 = jnp.bfloat16

D_MODEL = 1024
N_HEADS_A = 8
HEAD_DIM = 64
D_A = 512
D_B = 256
N_GROUPS_B = 4
CHUNK_B = 128
D_C = 256
N_HEADS_C = 4
N_MEM = 256
N_EXPERTS = 16
EC_CAPACITY_FACTOR = 2
DILATIONS = (1, 4, 16)
RADIUS = 64
Q_BLOCK = 128
ROPE_THETA = 10000.0
EPS = 1e-6
NEG_INF = -1e30
LOG2_E = 1.4426950408889634

LANES = 128
BF16_ROWS = 16
TOKEN_TILE = 512
ATTN_TILE = 512
MOE_TILE = 256
POSITION_TILES = 4
STACK_ROWS = 768
FILL_ROWS = 64
FFN_BLOCK = 512
UNSELECTED = -4096.0
BINADE_STEPS = (64, 32, 16, 8, 4, 2, 1)
MANTISSA_STEPS = 52


def _rms(x, g):
    return x * lax.rsqrt(jnp.mean(x * x, axis=-1, keepdims=True) + EPS) * g


def _mem_kv_kernel(mem_ref, g_ref, w_ref, k_ref, v_ref):
    h = _rms(mem_ref[0], g_ref[...]).astype(BF16)
    kv = jnp.dot(h, w_ref[...], preferred_element_type=F32)
    k_ref[0] = kv[:, :D_C].astype(BF16)
    v_ref[0] = kv[:, D_C:].astype(BF16)


def _mem_kv(mem, g_mem, w_kv):
    b = mem.shape[0]
    return pl.pallas_call(
        _mem_kv_kernel,
        grid=(b,),
        in_specs=[
            pl.BlockSpec((1, N_MEM, D_MODEL), lambda i: (i, 0, 0)),
            pl.BlockSpec((1, D_MODEL), lambda i: (0, 0)),
            pl.BlockSpec((D_MODEL, 2 * D_C), lambda i: (0, 0)),
        ],
        out_specs=[
            pl.BlockSpec((1, N_MEM, D_C), lambda i: (i, 0, 0)),
            pl.BlockSpec((1, N_MEM, D_C), lambda i: (i, 0, 0)),
        ],
        out_shape=[jax.ShapeDtypeStruct((b, N_MEM, D_C), BF16)] * 2,
        name="mem_kv",
    )(mem, g_mem, w_kv)


def _in_proj_kernel(x_ref, g_ref, w_ref, cos_ref, sin_ref, lng_ref, lnb_ref,
                    q4_ref, k4_ref, v4_ref, q16_ref, k16_ref, v16_ref, u_ref, vb_ref, qc_ref,
                    chunks, chunks4):
    tm = x_ref.shape[0]
    d4 = DILATIONS[1]
    n_chunks = D_A // LANES

    def emit_by_residue(z, out4_ref, out16_ref):
        for c in range(n_chunks):
            chunks[c] = z[:, c * LANES:(c + 1) * LANES]
        quarter = tm // d4
        for r in range(d4):
            rows = [chunks[c, pl.ds(r, quarter, stride=d4), :] for c in range(n_chunks)]
            out4_ref[0, r] = jnp.concatenate(rows, axis=1).astype(BF16)
            for c in range(n_chunks):
                chunks4[c, r * quarter:(r + 1) * quarter, :] = rows[c]
        for r in range(d4):
            for a in range(d4):
                rows = [chunks4[c, pl.ds(r * quarter + a, quarter // d4, stride=d4), :]
                        for c in range(n_chunks)]
                out16_ref[0, r + d4 * a] = jnp.concatenate(rows, axis=1).astype(BF16)

    h = _rms(x_ref[...], g_ref[...]).astype(BF16)
    cos = jnp.concatenate([cos_ref[...]] * (D_A // LANES), axis=1)
    sin = jnp.concatenate([sin_ref[...]] * (D_A // LANES), axis=1)
    lane = lax.broadcasted_iota(jnp.int32, (tm, D_A), 1)
    first_half = (lane & (HEAD_DIM - 1)) < (HEAD_DIM // 2)

    def rope(z):
        rot = jnp.where(first_half, pltpu.roll(z, D_A - HEAD_DIM // 2, 1),
                        pltpu.roll(z, HEAD_DIM // 2, 1))
        return z * cos + rot * sin

    def proj(lo, hi):
        return jnp.dot(h, w_ref[:, lo:hi], preferred_element_type=F32)

    emit_by_residue(rope(proj(0, D_A)), q4_ref, q16_ref)
    emit_by_residue(rope(proj(D_A, 2 * D_A)), k4_ref, k16_ref)
    emit_by_residue(proj(2 * D_A, 3 * D_A), v4_ref, v16_ref)
    u_ref[...] = jax.nn.gelu(proj(3 * D_A, 3 * D_A + D_B)).astype(BF16)
    vb = jax.nn.gelu(proj(3 * D_A + D_B, 3 * D_A + 2 * D_B))
    mu = jnp.mean(vb, axis=-1, keepdims=True)
    var = jnp.mean(jnp.square(vb - mu), axis=-1, keepdims=True)
    vb_ref[...] = ((vb - mu) * lax.rsqrt(var + EPS) * lng_ref[...] + lnb_ref[...]).astype(BF16)
    qc_ref[...] = proj(3 * D_A + 2 * D_B, 3 * D_A + 2 * D_B + D_C).astype(BF16)


def _residue_spec(dil, tm, tiles_per_seq, width):
    return pl.BlockSpec((1, dil, tm // dil, width),
                        lambda i: (i // tiles_per_seq, 0, i % tiles_per_seq, 0))


def _in_proj(x2, g_mix, w_in, cos_t, sin_t, ln_g, ln_b, b, seq):
    t = x2.shape[0]
    tm = TOKEN_TILE
    d_in = w_in.shape[1]
    tiles_per_seq = seq // tm
    row = lambda i: (i, 0)
    const = lambda i: (0, 0)
    pos = lambda i: (i % tiles_per_seq, 0)
    d4, d16 = DILATIONS[1], DILATIONS[2]
    res4 = _residue_spec(d4, tm, tiles_per_seq, D_A)
    res16 = _residue_spec(d16, tm, tiles_per_seq, D_A)
    shape4 = jax.ShapeDtypeStruct((b, d4, seq // d4, D_A), BF16)
    shape16 = jax.ShapeDtypeStruct((b, d16, seq // d16, D_A), BF16)
    return pl.pallas_call(
        _in_proj_kernel,
        grid=(t // tm,),
        in_specs=[
            pl.BlockSpec((tm, D_MODEL), row),
            pl.BlockSpec((1, D_MODEL), const),
            pl.BlockSpec((D_MODEL, d_in), const),
            pl.BlockSpec((tm, LANES), pos),
            pl.BlockSpec((tm, LANES), pos),
            pl.BlockSpec((1, D_B), const),
            pl.BlockSpec((1, D_B), const),
        ],
        out_specs=[res4] * 3 + [res16] * 3 + [pl.BlockSpec((tm, w), row) for w in (D_B, D_B, D_C)],
        out_shape=[shape4] * 3 + [shape16] * 3
                  + [jax.ShapeDtypeStruct((t, w), BF16) for w in (D_B, D_B, D_C)],
        scratch_shapes=[pltpu.VMEM((D_A // LANES, tm, LANES), F32)] * 2,
        name="in_proj",
    )(x2, g_mix, w_in, cos_t, sin_t, ln_g, ln_b)


def _lse_lane(head):
    return head + HEAD_DIM * (1 - head % 2)


def _attn_kernel(q_ref, kp_ref, kc_ref, kn_ref, vp_ref, vc_ref, vn_ref,
                 o_ref, lse_ref, kwin, vwin, bias, *, nres, tr, sub):
    j = pl.program_id(2)
    halo = RADIUS // nres
    qb = Q_BLOCK // nres
    for win, (p_ref, c_ref, n_ref) in ((kwin, (kp_ref, kc_ref, kn_ref)), (vwin, (vp_ref, vc_ref, vn_ref))):
        win[:, 0:halo] = p_ref[0]
        win[:, halo:halo + tr] = c_ref[0]
        win[:, halo + tr:] = n_ref[0]

    t_idx = lax.broadcasted_iota(jnp.int32, (Q_BLOCK, 2 * Q_BLOCK), 0)
    s_idx = lax.broadcasted_iota(jnp.int32, (Q_BLOCK, 2 * Q_BLOCK), 1)
    q_res, q_row = t_idx >> (qb.bit_length() - 1), t_idx & (qb - 1)
    k_res, k_row = s_idx >> qb.bit_length(), s_idx & (2 * qb - 1)
    band = jnp.abs(nres * (q_row - k_row + halo) + q_res - k_res) <= RADIUS
    k_elem = nres * (k_row - halo) + k_res
    lane = lax.broadcasted_iota(jnp.int32, (Q_BLOCK, LANES), 1)
    low_half = lane < HEAD_DIM

    def block(i, carry):
        r0 = pl.multiple_of(i * qb, qb)
        kpos = k_elem + nres * (j * tr + r0)
        bias[...] = jnp.where(band & (kpos >= 0) & (kpos < sub), 0.0, NEG_INF)
        lse_blk = jnp.zeros((Q_BLOCK, LANES), F32)
        for hp in range(N_HEADS_A // 2):
            cols = slice(hp * LANES, (hp + 1) * LANES)
            stack = lambda pieces: pieces[0] if nres == 1 else jnp.concatenate(pieces, axis=0)
            q_pair = stack([q_ref[0, r, pl.ds(r0, qb), cols] for r in range(nres)])
            k_pair = stack([kwin[r, pl.ds(r0, 2 * qb), cols] for r in range(nres)])
            v_pair = stack([vwin[r, pl.ds(r0, 2 * qb), cols] for r in range(nres)])
            halves = []
            for half in range(2):
                keep = low_half if half == 0 else jnp.logical_not(low_half)
                qm = jnp.where(keep, q_pair, jnp.zeros_like(q_pair))
                s = lax.dot_general(qm, k_pair, (((1,), (1,)), ((), ())),
                                    preferred_element_type=F32) + bias[...]
                m = jnp.max(s, axis=-1, keepdims=True)
                p = jnp.exp2(s - m)
                den = jnp.sum(p, axis=-1, keepdims=True)
                pv = jnp.dot(p.astype(BF16), v_pair, preferred_element_type=F32)
                halves.append(pv / den)
                lane0 = _lse_lane(2 * hp + half)
                hit = (lane == lane0) | (lane == lane0 + N_HEADS_A)
                lse_blk = jnp.where(hit, m + jnp.log(den) * LOG2_E, lse_blk)
            out = jnp.where(low_half, halves[0], halves[1]).astype(BF16)
            for r in range(nres):
                o_ref[0, r, pl.ds(r0, qb), cols] = out[r * qb:(r + 1) * qb]
        for r in range(nres):
            lse_ref[0, r, pl.ds(r0, qb), :] = lse_blk[r * qb:(r + 1) * qb]
        return carry

    lax.fori_loop(0, tr // qb, block, 0)


def _banded_attention(q, k, v, nres, name):
    b, n_res, rows, _ = q.shape
    tr = min(ATTN_TILE // nres, rows)
    halo = RADIUS // nres
    halo_per_tile = tr // halo
    last_halo = rows // halo - 1
    if nres == 1:
        grid = (b, n_res, rows // tr)
        at = lambda row_block: (lambda bi, r, j: (bi, r, row_block(j), 0))
    else:
        grid = (b, 1, rows // tr)
        at = lambda row_block: (lambda bi, r, j: (bi, 0, row_block(j), 0))
    cur = pl.BlockSpec((1, nres, tr, D_A), at(lambda j: j))
    prev = pl.BlockSpec((1, nres, halo, D_A), at(lambda j: jnp.maximum(j * halo_per_tile - 1, 0)))
    nxt = pl.BlockSpec((1, nres, halo, D_A),
                       at(lambda j: jnp.minimum((j + 1) * halo_per_tile, last_halo)))
    return pl.pallas_call(
        functools.partial(_attn_kernel, nres=nres, tr=tr, sub=nres * rows),
        grid=grid,
        in_specs=[cur, prev, cur, nxt, prev, cur, nxt],
        out_specs=[cur, pl.BlockSpec((1, nres, tr, LANES), at(lambda j: j))],
        out_shape=[
            jax.ShapeDtypeStruct((b, n_res, rows, D_A), BF16),
            jax.ShapeDtypeStruct((b, n_res, rows, LANES), F32),
        ],
        scratch_shapes=[pltpu.VMEM((nres, tr + 2 * halo, D_A), BF16)] * 2
                       + [pltpu.VMEM((Q_BLOCK, 2 * Q_BLOCK), F32)],
        name=name,
    )(q, k, k, k, v, v, v)


def _split_bf16(x):
    hi = x.astype(BF16)
    lo = (x - hi.astype(F32)).astype(BF16)
    return hi, lo


def _mix_kernel(x_ref, o1_ref, o2_ref, o3_ref, l1_ref, l2_ref, l3_ref, u_ref, vb_ref, qc_ref,
                mk_ref, mv_ref, ws_ref, bs_ref, expand_ref, go_ref, wo_ref, gf_ref, wr_ref,
                x1_ref, xn_ref, aff_ref, afft_ref, obuf1, obuf2, obuf3, lbuf1, lbuf2, lbuf3):
    tm = x_ref.shape[0]

    def token_order(ref, buf):
        dil, width = ref.shape[1], ref.shape[3]
        for r in range(dil):
            piece = ref[0, r].astype(F32)
            for c in range(width // LANES):
                buf[c, pl.ds(r, tm // dil, stride=dil), :] = piece[:, c * LANES:(c + 1) * LANES]
        return jnp.concatenate([buf[c] for c in range(width // LANES)], axis=1)

    l1, l2, l3 = token_order(l1_ref, lbuf1), token_order(l2_ref, lbuf2), token_order(l3_ref, lbuf3)
    mx = jnp.maximum(jnp.maximum(l1, l2), l3)
    e1, e2, e3 = jnp.exp2(l1 - mx), jnp.exp2(l2 - mx), jnp.exp2(l3 - mx)
    tot = e1 + e2 + e3
    is_hi_lane = (lax.broadcasted_iota(jnp.int32, (tm, LANES), 1) & N_HEADS_A) == 0
    o_a = jnp.zeros((tm, D_A), F32)
    for e, o_ref, buf in ((e1, o1_ref, obuf1), (e2, o2_ref, obuf2), (e3, o3_ref, obuf3)):
        hi, lo = _split_bf16(e / tot)
        w = jnp.dot(jnp.where(is_hi_lane, hi, lo), expand_ref[...], preferred_element_type=F32)
        o_a = o_a + w * token_order(o_ref, buf)

    lane_b = lax.broadcasted_iota(jnp.int32, (CHUNK_B, D_B), 1)
    group_w = D_B // N_GROUPS_B
    gated = []
    for c in range(tm // CHUNK_B):
        vchunk = vb_ref[c * CHUNK_B:(c + 1) * CHUNK_B, :]
        acc = bs_ref[...]
        for g in range(N_GROUPS_B):
            y = jnp.dot(ws_ref[g], vchunk, preferred_element_type=F32)
            acc = acc + jnp.where(lane_b // group_w == g, y, 0.0)
        gated.append(acc)
    o_b = u_ref[...].astype(F32) * jnp.concatenate(gated, axis=0)

    lane_c = lax.broadcasted_iota(jnp.int32, (tm, LANES), 1)
    low_half = lane_c < HEAD_DIM
    oc_parts = []
    for hp in range(N_HEADS_C // 2):
        cols = slice(hp * LANES, (hp + 1) * LANES)
        q_pair = qc_ref[:, cols]
        k_pair = mk_ref[0, :, cols]
        v_pair = mv_ref[0, :, cols]
        halves = []
        for half in range(2):
            keep = low_half if half == 0 else jnp.logical_not(low_half)
            qm = jnp.where(keep, q_pair, jnp.zeros_like(q_pair))
            s = lax.dot_general(qm, k_pair, (((1,), (1,)), ((), ())), preferred_element_type=F32)
            m = jnp.max(s, axis=-1, keepdims=True)
            p = jnp.exp(s - m)
            p = p / jnp.sum(p, axis=-1, keepdims=True)
            halves.append(jnp.dot(p.astype(BF16), v_pair, preferred_element_type=F32))
        oc_parts.append(jnp.where(low_half, halves[0], halves[1]))
    o_c = jnp.concatenate(oc_parts, axis=1)

    go = go_ref[...]
    o = jnp.concatenate([_rms(o_a, go[:, :D_A]), _rms(o_b, go[:, D_A:D_A + D_B]),
                         _rms(o_c, go[:, D_A + D_B:])], axis=1).astype(BF16)
    x1 = x_ref[...] + jnp.dot(o, wo_ref[...], preferred_element_type=F32)
    x1_ref[...] = x1

    xn = _rms(x1, gf_ref[...]).astype(BF16)
    xn_ref[...] = xn
    logits = jnp.dot(xn, wr_ref[...], preferred_element_type=F32)
    is_expert = lane_c < N_EXPERTS
    logits = jnp.where(is_expert, logits, NEG_INF)
    m = jnp.max(logits, axis=-1, keepdims=True)
    ex = jnp.where(is_expert, jnp.exp(logits - m), 0.0)
    aff = ex / jnp.sum(ex, axis=-1, keepdims=True)
    aff_ref[...] = aff
    afft_ref[...] = jnp.transpose(aff)[:N_EXPERTS, :]


def _mix(x2, o_pats, lse_pats, u, vb, qc, mem_k, mem_v, w_sp, b_sp, expand, g_out, w_out,
         g_ffn, w_router, seq):
    t = x2.shape[0]
    tm = TOKEN_TILE
    tiles_per_seq = seq // tm
    row = lambda i: (i, 0)
    const2 = lambda i: (0, 0)
    const3 = lambda i: (0, 0, 0)
    batch = lambda i: (i // tiles_per_seq, 0, 0)
    dils = [o.shape[1] for o in o_pats]
    return pl.pallas_call(
        _mix_kernel,
        grid=(t // tm,),
        in_specs=[
            pl.BlockSpec((tm, D_MODEL), row),
            *[_residue_spec(d, tm, tiles_per_seq, D_A) for d in dils],
            *[_residue_spec(d, tm, tiles_per_seq, LANES) for d in dils],
            pl.BlockSpec((tm, D_B), row), pl.BlockSpec((tm, D_B), row), pl.BlockSpec((tm, D_C), row),
            pl.BlockSpec((1, N_MEM, D_C), batch), pl.BlockSpec((1, N_MEM, D_C), batch),
            pl.BlockSpec((N_GROUPS_B, CHUNK_B, CHUNK_B), const3),
            pl.BlockSpec((CHUNK_B, D_B), const2),
            pl.BlockSpec((LANES, D_A), const2),
            pl.BlockSpec((1, D_MODEL), const2),
            pl.BlockSpec((D_MODEL, D_MODEL), const2),
            pl.BlockSpec((1, D_MODEL), const2),
            pl.BlockSpec((D_MODEL, LANES), const2),
        ],
        out_specs=[
            pl.BlockSpec((tm, D_MODEL), row),
            pl.BlockSpec((tm, D_MODEL), row),
            pl.BlockSpec((tm, LANES), row),
            pl.BlockSpec((N_EXPERTS, tm), lambda i: (0, i)),
        ],
        out_shape=[
            jax.ShapeDtypeStruct((t, D_MODEL), F32),
            jax.ShapeDtypeStruct((t, D_MODEL), BF16),
            jax.ShapeDtypeStruct((t, LANES), F32),
            jax.ShapeDtypeStruct((N_EXPERTS, t), F32),
        ],
        scratch_shapes=[pltpu.VMEM((D_A // LANES, tm, LANES), F32)] * 3
                       + [pltpu.VMEM((1, tm, LANES), F32)] * 3,
        name="mix",
    )(x2, *o_pats, *lse_pats, u, vb, qc, mem_k, mem_v, w_sp, b_sp, expand, g_out, w_out,
      g_ffn, w_router)


def _threshold_kernel(afft_ref, thr_col_ref, quota_col_ref, thr_row_ref, quota_row_ref, *, cap):
    aff = afft_ref[...]

    def enough(v):
        return jnp.sum(jnp.where(aff >= v, 1.0, 0.0), axis=-1, keepdims=True) >= cap

    hi = jnp.full((N_EXPERTS, 1), 2.0, F32)
    for shift in BINADE_STEPS:
        cand = hi * (2.0 ** -shift)
        hi = jnp.where(enough(cand), hi, cand)
    lo = jnp.where(enough(hi * 0.5), hi * 0.5, 0.0)

    def bisect(_, bracket):
        lo, hi = bracket
        mid = (lo + hi) * 0.5
        ok = enough(mid)
        return jnp.where(ok, mid, lo), jnp.where(ok, hi, mid)

    lo, hi = lax.fori_loop(0, MANTISSA_STEPS, bisect, (lo, hi))
    thr_f = jnp.min(jnp.where(aff >= lo, aff, jnp.inf), axis=-1, keepdims=True)
    n_gt = jnp.sum(jnp.where(aff > thr_f, 1.0, 0.0), axis=-1, keepdims=True)
    quota = cap - n_gt
    thr_col = jnp.broadcast_to(thr_f, (N_EXPERTS, LANES))
    quota_col = jnp.broadcast_to(quota, (N_EXPERTS, LANES))
    thr_col_ref[...] = thr_col
    quota_col_ref[...] = quota_col
    diag = (lax.broadcasted_iota(jnp.int32, (N_EXPERTS, LANES), 0)
            == lax.broadcasted_iota(jnp.int32, (N_EXPERTS, LANES), 1))
    thr_row = jnp.sum(jnp.where(diag, thr_col, 0.0), axis=0, keepdims=True)
    quota_row = jnp.sum(jnp.where(diag, quota_col, 0.0), axis=0, keepdims=True)
    thr_row_ref[...] = jnp.broadcast_to(thr_row, (8, LANES))
    quota_row_ref[...] = jnp.broadcast_to(quota_row, (8, LANES))


def _threshold(afft, cap):
    t = afft.shape[1]
    full = lambda shape: pl.BlockSpec(shape, lambda: (0,) * len(shape))
    return pl.pallas_call(
        functools.partial(_threshold_kernel, cap=float(cap)),
        in_specs=[full((N_EXPERTS, t))],
        out_specs=[full((N_EXPERTS, LANES)), full((N_EXPERTS, LANES)),
                   full((8, LANES)), full((8, LANES))],
        out_shape=[jax.ShapeDtypeStruct((N_EXPERTS, LANES), F32)] * 2
                  + [jax.ShapeDtypeStruct((8, LANES), F32)] * 2,
        name="threshold",
    )(afft)


def _positions_kernel(afft_ref, aff_ref, thr_col_ref, quota_col_ref, thr_row_ref, quota_row_ref,
                      lp_ref, lpt_ref, gate_ref, start_ref, off_ref, padded_ref, total_ref,
                      eq_col, eq_row, slot_col):
    i = pl.program_id(0)
    tm = MOE_TILE

    @pl.when(i == 0)
    def _():
        eq_col[...] = jnp.zeros_like(eq_col)
        eq_row[...] = jnp.zeros_like(eq_row)
        slot_col[...] = jnp.zeros_like(slot_col)

    r = lax.broadcasted_iota(jnp.int32, (tm, tm), 0)
    c = lax.broadcasted_iota(jnp.int32, (tm, tm), 1)
    before = jnp.where(r < c, 1.0, 0.0).astype(BF16)
    after = jnp.where(c < r, 1.0, 0.0).astype(BF16)
    thr, quota = thr_col_ref[:, 0:1], quota_col_ref[:, 0:1]
    thr_r, quota_r = thr_row_ref[0:1, :], quota_row_ref[0:1, :]
    is_expert = lax.broadcasted_iota(jnp.int32, (tm, LANES), 1) < N_EXPERTS
    earlier_expert = (lax.broadcasted_iota(jnp.int32, (N_EXPERTS, LANES), 1)
                      < lax.broadcasted_iota(jnp.int32, (N_EXPERTS, LANES), 0))
    pad = lambda n: jnp.ceil(n / BF16_ROWS) * BF16_ROWS

    eq_seen_col, eq_seen_row, first_slot = eq_col[:, 0:1], eq_row[0:1, :], slot_col[...]
    for s in range(aff_ref.shape[0] // tm):
        a = afft_ref[:, s * tm:(s + 1) * tm]
        eq = a == thr
        eq_f = jnp.where(eq, 1.0, 0.0)
        eq_rank = jnp.dot(eq_f.astype(BF16), before, preferred_element_type=F32) + eq_seen_col
        sel = (a > thr) | (eq & (eq_rank < quota))
        sel_f = jnp.where(sel, 1.0, 0.0)
        lp = jnp.dot(sel_f.astype(BF16), before, preferred_element_type=F32)
        lp_ref[:, s * tm:(s + 1) * tm] = jnp.where(sel, lp, UNSELECTED)
        padded = pad(jnp.sum(sel_f, axis=-1, keepdims=True))
        start_ref[s] = first_slot
        padded_ref[s] = jnp.broadcast_to(padded, (N_EXPERTS, LANES))
        first_slot = first_slot + padded
        eq_seen_col = eq_seen_col + jnp.sum(eq_f, axis=-1, keepdims=True)

        at = aff_ref[s * tm:(s + 1) * tm, :]
        eq_t = (at == thr_r) & is_expert
        eq_tf = jnp.where(eq_t, 1.0, 0.0)
        eq_rank_t = jnp.dot(after, eq_tf.astype(BF16), preferred_element_type=F32) + eq_seen_row
        sel_t = ((at > thr_r) & is_expert) | (eq_t & (eq_rank_t < quota_r))
        sel_tf = jnp.where(sel_t, 1.0, 0.0)
        lp_t = jnp.dot(after, sel_tf.astype(BF16), preferred_element_type=F32)
        lpt_ref[s * tm:(s + 1) * tm, :] = jnp.where(sel_t, lp_t, UNSELECTED)
        gate_ref[s * tm:(s + 1) * tm, :] = jnp.where(sel_t, at, 0.0)
        eq_seen_row = eq_seen_row + jnp.sum(eq_tf, axis=0, keepdims=True)
        padded_row = pad(jnp.sum(sel_tf, axis=0, keepdims=True))
        off = jnp.sum(jnp.where(earlier_expert, padded_row, 0.0), axis=-1, keepdims=True)
        off_ref[s] = jnp.broadcast_to(off, (N_EXPERTS, LANES))

    eq_col[...] = jnp.broadcast_to(eq_seen_col, eq_col.shape)
    eq_row[...] = jnp.broadcast_to(eq_seen_row, eq_row.shape)
    slot_col[...] = first_slot
    total_ref[...] = first_slot


def _positions(afft, aff, thr_col, quota_col, thr_row, quota_row):
    t = aff.shape[0]
    tm = MOE_TILE * POSITION_TILES
    n_tiles = t // MOE_TILE
    c2 = lambda i: (0, 0)
    per_tile = pl.BlockSpec((POSITION_TILES, N_EXPERTS, LANES), lambda i: (i, 0, 0))
    per_tile_shape = jax.ShapeDtypeStruct((n_tiles, N_EXPERTS, LANES), F32)
    return pl.pallas_call(
        _positions_kernel,
        grid=(t // tm,),
        in_specs=[
            pl.BlockSpec((N_EXPERTS, tm), lambda i: (0, i)),
            pl.BlockSpec((tm, LANES), lambda i: (i, 0)),
            pl.BlockSpec((N_EXPERTS, LANES), c2), pl.BlockSpec((N_EXPERTS, LANES), c2),
            pl.BlockSpec((8, LANES), c2), pl.BlockSpec((8, LANES), c2),
        ],
        out_specs=[
            pl.BlockSpec((N_EXPERTS, tm), lambda i: (0, i)),
            pl.BlockSpec((tm, LANES), lambda i: (i, 0)),
            pl.BlockSpec((tm, LANES), lambda i: (i, 0)),
            per_tile, per_tile, per_tile,
            pl.BlockSpec((N_EXPERTS, LANES), c2),
        ],
        out_shape=[
            jax.ShapeDtypeStruct((N_EXPERTS, t), F32),
            jax.ShapeDtypeStruct((t, LANES), F32),
            jax.ShapeDtypeStruct((t, LANES), F32),
            per_tile_shape,
            per_tile_shape,
            per_tile_shape,
            jax.ShapeDtypeStruct((N_EXPERTS, LANES), F32),
        ],
        scratch_shapes=[pltpu.VMEM((N_EXPERTS, LANES), F32), pltpu.VMEM((8, LANES), F32),
                        pltpu.VMEM((N_EXPERTS, LANES), F32)],
        compiler_params=pltpu.CompilerParams(dimension_semantics=("arbitrary",)),
        name="positions",
    )(afft, aff, thr_col, quota_col, thr_row, quota_row)


class _TileTables:
    def __init__(self, start_ref, off_ref, rows_ref):
        self.start_ref, self.off_ref, self.rows_ref = start_ref, off_ref, rows_ref

    def slot(self, tile, e, k):
        return pl.multiple_of(self.start_ref[tile * N_EXPERTS + e] + k * BF16_ROWS, BF16_ROWS)

    def stack_rows(self, tile):
        last = tile * N_EXPERTS + N_EXPERTS - 1
        return self.off_ref[last] + self.rows_ref[last]

    def fits(self, tile):
        return self.stack_rows(tile) <= STACK_ROWS

    def for_each_group(self, tile, base, fn):
        for e in range(N_EXPERTS):
            off = self.off_ref[tile * N_EXPERTS + e]
            n_groups = self.rows_ref[tile * N_EXPERTS + e] // BF16_ROWS

            def body(k, carry, e=e, off=off):
                s = off + k * BF16_ROWS - base

                @pl.when((s >= 0) & (s < STACK_ROWS))
                def _():
                    fn(e, k, pl.multiple_of(s, BF16_ROWS))
                return carry

            lax.fori_loop(0, n_groups, body, 0)


def _dispatch_kernel(start_ref, off_ref, rows_ref, xn_ref, lp_ref, xe_ref,
                     onehot, stack, zeros, sem, sem_fill):
    i = pl.program_id(0)
    n = pl.num_programs(0)
    slot = i % 2
    tm = xn_ref.shape[0]
    tables = _TileTables(start_ref, off_ref, rows_ref)
    slot_iota = lax.broadcasted_iota(jnp.int32, (BF16_ROWS, tm), 0).astype(F32)

    def copy(tile, buf, e, k, s):
        return pltpu.make_async_copy(stack.at[buf, pl.ds(s, BF16_ROWS), :],
                                     xe_ref.at[e, pl.ds(tables.slot(tile, e, k), BF16_ROWS), :],
                                     sem.at[buf])

    def compact(base):
        onehot[...] = jnp.zeros_like(onehot)

        def mark(e, k, s):
            hit = lp_ref[e:e + 1, :] == slot_iota + (k * BF16_ROWS).astype(F32)
            onehot[pl.ds(s, BF16_ROWS), :] = jnp.where(hit, 1.0, 0.0).astype(BF16)

        tables.for_each_group(i, base, mark)
        stack[slot] = jnp.dot(onehot[...], xn_ref[...], preferred_element_type=F32).astype(BF16)

    @pl.when((i >= 2) & tables.fits(jnp.maximum(i - 2, 0)))
    def _():
        tables.for_each_group(i - 2, 0, lambda e, k, s: copy(i - 2, slot, e, k, s).wait())

    compact(0)
    tables.for_each_group(i, 0, lambda e, k, s: copy(i, slot, e, k, s).start())

    @pl.when(jnp.logical_not(tables.fits(i)))
    def _():
        tables.for_each_group(i, 0, lambda e, k, s: copy(i, slot, e, k, s).wait())
        n_rounds = (tables.stack_rows(i) + STACK_ROWS - 1) // STACK_ROWS

        def extra_round(r, carry):
            base = r * STACK_ROWS
            compact(base)
            tables.for_each_group(i, base, lambda e, k, s: copy(i, slot, e, k, s).start())
            tables.for_each_group(i, base, lambda e, k, s: copy(i, slot, e, k, s).wait())
            return carry

        lax.fori_loop(1, n_rounds, extra_round, 0)

    @pl.when(i == n - 1)
    def _():
        @pl.when((i >= 1) & tables.fits(jnp.maximum(i - 1, 0)))
        def _():
            tables.for_each_group(i - 1, 0, lambda e, k, s: copy(i - 1, 1 - slot, e, k, s).wait())

        @pl.when(tables.fits(i))
        def _():
            tables.for_each_group(i, 0, lambda e, k, s: copy(i, slot, e, k, s).wait())

        zeros[...] = jnp.zeros_like(zeros)
        cap_rows = xe_ref.shape[1]
        big = zeros.shape[0]

        def fill(e, row0, rows):
            return pltpu.make_async_copy(zeros.at[pl.ds(0, rows), :],
                                         xe_ref.at[e, pl.ds(row0, rows), :], sem_fill.at[0])

        for e in range(N_EXPERTS):
            used = start_ref[i * N_EXPERTS + e] + rows_ref[i * N_EXPERTS + e]
            n_small = ((-used) & (big - 1)) // BF16_ROWS
            base = used + n_small * BF16_ROWS
            n_big = (cap_rows - base) // big

            def small_row(k):
                return pl.multiple_of(used + k * BF16_ROWS, BF16_ROWS)

            def big_row(k):
                return pl.multiple_of(base + k * big, big)

            lax.fori_loop(0, n_small, lambda k, c: (fill(e, small_row(k), BF16_ROWS).start(), c)[1], 0)
            lax.fori_loop(0, n_big, lambda k, c: (fill(e, big_row(k), big).start(), c)[1], 0)
            lax.fori_loop(0, n_small, lambda k, c: (fill(e, small_row(k), BF16_ROWS).wait(), c)[1], 0)
            lax.fori_loop(0, n_big, lambda k, c: (fill(e, big_row(k), big).wait(), c)[1], 0)


def _dispatch(start, off, rows, xn, lp, cap_rows):
    t = xn.shape[0]
    tm = MOE_TILE
    return pl.pallas_call(
        _dispatch_kernel,
        grid_spec=pltpu.PrefetchScalarGridSpec(
            num_scalar_prefetch=3,
            grid=(t // tm,),
            in_specs=[
                pl.BlockSpec((tm, D_MODEL), lambda i, *_: (i, 0)),
                pl.BlockSpec((N_EXPERTS, tm), lambda i, *_: (0, i)),
            ],
            out_specs=pl.BlockSpec(memory_space=pl.ANY),
            scratch_shapes=[pltpu.VMEM((STACK_ROWS, tm), BF16),
                            pltpu.VMEM((2, STACK_ROWS, D_MODEL), BF16),
                            pltpu.VMEM((FILL_ROWS, D_MODEL), BF16),
                            pltpu.SemaphoreType.DMA((2,)),
                            pltpu.SemaphoreType.DMA((1,))],
        ),
        out_shape=jax.ShapeDtypeStruct((N_EXPERTS, cap_rows, D_MODEL), BF16),
        compiler_params=pltpu.CompilerParams(dimension_semantics=("arbitrary",)),
        name="dispatch",
    )(start, off, rows, xn, lp)


def _ffn_kernel(used_ref, xe_ref, wg_ref, wu_ref, wd_ref, ye_ref):
    e, j = pl.program_id(0), pl.program_id(1)
    bs = xe_ref.shape[1]
    n_valid = used_ref[e] - j * bs

    @pl.when(n_valid > 0)
    def _():
        row = lax.broadcasted_iota(jnp.int32, (bs, D_MODEL), 0)
        x = jnp.where(row < n_valid, xe_ref[0], jnp.zeros((bs, D_MODEL), BF16))
        gate = jnp.dot(x, wg_ref[0], preferred_element_type=F32)
        up = jnp.dot(x, wu_ref[0], preferred_element_type=F32)
        hdn = (jax.nn.silu(gate) * up).astype(BF16)
        ye_ref[0] = jnp.dot(hdn, wd_ref[0], preferred_element_type=F32).astype(BF16)

    @pl.when(n_valid <= 0)
    def _():
        ye_ref[0] = jnp.zeros((bs, D_MODEL), BF16)


def _ffn(used, xe, w_gate, w_up, w_down):
    cap_rows = xe.shape[1]
    bs = FFN_BLOCK

    def x_map(e, j, used_ref):
        last = jnp.maximum((used_ref[e] + bs - 1) // bs - 1, 0)
        return (e, jnp.minimum(j, last), 0)

    w_map = lambda e, j, used_ref: (e, 0, 0)
    return pl.pallas_call(
        _ffn_kernel,
        grid_spec=pltpu.PrefetchScalarGridSpec(
            num_scalar_prefetch=1,
            grid=(N_EXPERTS, cap_rows // bs),
            in_specs=[
                pl.BlockSpec((1, bs, D_MODEL), x_map),
                pl.BlockSpec((1, D_MODEL, D_MODEL), w_map),
                pl.BlockSpec((1, D_MODEL, D_MODEL), w_map),
                pl.BlockSpec((1, D_MODEL, D_MODEL), w_map),
            ],
            out_specs=pl.BlockSpec((1, bs, D_MODEL), lambda e, j, used_ref: (e, j, 0)),
        ),
        out_shape=jax.ShapeDtypeStruct((N_EXPERTS, cap_rows, D_MODEL), BF16),
        name="expert_ffn",
    )(used, xe, w_gate, w_up, w_down)


def _gate_matrix(lpt_ref, gate_ref, offv_ref, rowsv_ref, base):
    off = offv_ref[0][:, 0:1]
    end = off + rowsv_ref[0][:, 0:1]
    row = lax.broadcasted_iota(jnp.int32, (N_EXPERTS, STACK_ROWS), 1).astype(F32) + base
    owner = (row >= off) & (row < end)
    owner_pad = jnp.concatenate(
        [jnp.where(owner, 1.0, 0.0), jnp.zeros((LANES - N_EXPERTS, STACK_ROWS), F32)], axis=0).astype(BF16)
    slot_of_row = row[0:1, :] - jnp.sum(jnp.where(owner, off, 0.0), axis=0, keepdims=True)
    slot_of_token = jnp.dot(lpt_ref[...].astype(BF16), owner_pad, preferred_element_type=F32)
    gate_of_token = jnp.dot(gate_ref[...].astype(BF16), owner_pad, preferred_element_type=F32)
    return jnp.where(slot_of_token == slot_of_row, gate_of_token, 0.0).astype(BF16)


def _combine_kernel(start_ref, off_ref, rows_ref, x1_ref, lpt_ref, gate_ref, offv_ref, rowsv_ref,
                    gfin_ref, ye_ref, y_ref, stack, extra, sem, sem_extra):
    i = pl.program_id(0)
    n = pl.num_programs(0)
    slot = i % 2
    tables = _TileTables(start_ref, off_ref, rows_ref)

    def fetch(tile, buf, e, k, s):
        return pltpu.make_async_copy(ye_ref.at[e, pl.ds(tables.slot(tile, e, k), BF16_ROWS), :],
                                     stack.at[buf, pl.ds(s, BF16_ROWS), :], sem.at[buf])

    @pl.when(i == 0)
    def _():
        stack[...] = jnp.zeros_like(stack)
        extra[...] = jnp.zeros_like(extra)
        tables.for_each_group(0, 0, lambda e, k, s: fetch(0, 0, e, k, s).start())

    @pl.when(i + 1 < n)
    def _():
        tables.for_each_group(i + 1, 0, lambda e, k, s: fetch(i + 1, 1 - slot, e, k, s).start())

    tables.for_each_group(i, 0, lambda e, k, s: fetch(i, slot, e, k, s).wait())
    moe = jnp.dot(_gate_matrix(lpt_ref, gate_ref, offv_ref, rowsv_ref, 0.0), stack[slot],
                  preferred_element_type=F32)

    n_rounds = (tables.stack_rows(i) + STACK_ROWS - 1) // STACK_ROWS

    def extra_round(r, acc):
        base = r * STACK_ROWS

        def fetch_extra(e, k, s):
            return pltpu.make_async_copy(ye_ref.at[e, pl.ds(tables.slot(i, e, k), BF16_ROWS), :],
                                         extra.at[pl.ds(s, BF16_ROWS), :], sem_extra.at[0])

        tables.for_each_group(i, base, lambda e, k, s: fetch_extra(e, k, s).start())
        tables.for_each_group(i, base, lambda e, k, s: fetch_extra(e, k, s).wait())
        g = _gate_matrix(lpt_ref, gate_ref, offv_ref, rowsv_ref, base.astype(F32))
        return acc + jnp.dot(g, extra[...], preferred_element_type=F32)

    moe = lax.fori_loop(1, n_rounds, extra_round, moe)
    y_ref[...] = _rms(x1_ref[...] + moe, gfin_ref[...])


def _combine(start, off, rows, x1, lpt, gate, off_vec, rows_vec, g_final, ye):
    t = x1.shape[0]
    tm = MOE_TILE
    row = lambda i, *_: (i, 0)
    per_tile = pl.BlockSpec((1, N_EXPERTS, LANES), lambda i, *_: (i, 0, 0))
    return pl.pallas_call(
        _combine_kernel,
        grid_spec=pltpu.PrefetchScalarGridSpec(
            num_scalar_prefetch=3,
            grid=(t // tm,),
            in_specs=[
                pl.BlockSpec((tm, D_MODEL), row),
                pl.BlockSpec((tm, LANES), row),
                pl.BlockSpec((tm, LANES), row),
                per_tile, per_tile,
                pl.BlockSpec((1, D_MODEL), lambda i, *_: (0, 0)),
                pl.BlockSpec(memory_space=pl.ANY),
            ],
            out_specs=pl.BlockSpec((tm, D_MODEL), row),
            scratch_shapes=[pltpu.VMEM((2, STACK_ROWS, D_MODEL), BF16),
                            pltpu.VMEM((STACK_ROWS, D_MODEL), BF16),
                            pltpu.SemaphoreType.DMA((2,)),
                            pltpu.SemaphoreType.DMA((1,))],
        ),
        out_shape=jax.ShapeDtypeStruct((t, D_MODEL), F32),
        compiler_params=pltpu.CompilerParams(dimension_semantics=("arbitrary",)),
        name="combine",
    )(start, off, rows, x1, lpt, gate, off_vec, rows_vec, g_final, ye)


def _rope_tables(seq):
    half = HEAD_DIM // 2
    inv_freq = jnp.power(ROPE_THETA, -jnp.arange(half, dtype=F32) * 2.0 / HEAD_DIM)
    ang = jnp.arange(seq, dtype=F32)[:, None] * inv_freq[None, :]
    cos, sin = jnp.cos(ang), jnp.sin(ang)
    reps = LANES // HEAD_DIM
    return (jnp.tile(jnp.concatenate([cos, cos], axis=1), (1, reps)),
            jnp.tile(jnp.concatenate([-sin, sin], axis=1), (1, reps)))


def _prepare_weights(g_mix, w_in, ln_b_g, ln_b_b, w_spatial, b_spatial, g_mem, w_mem_kv,
                     g_out, w_out, g_ffn, w_router, w_gate, w_up, w_down, g_final):
    scale = jnp.ones((w_in.shape[-1],), F32)
    scale = scale.at[:D_A].set(HEAD_DIM ** -0.5 * LOG2_E)
    scale = scale.at[3 * D_A + 2 * D_B:].set((D_C // N_HEADS_C) ** -0.5)
    head_of_lane = jnp.arange(D_A) // HEAD_DIM
    return dict(
        g_mix=g_mix[0][None], w_in=(w_in[0] * scale).astype(BF16),
        ln_g=ln_b_g[0][None], ln_b=ln_b_b[0][None],
        w_sp=w_spatial[0].astype(BF16),
        b_sp=jnp.repeat(b_spatial[0].T, D_B // N_GROUPS_B, axis=1),
        g_mem=g_mem[0][None], w_kv=w_mem_kv[0].astype(BF16),
        expand=sum((jnp.arange(LANES)[:, None] == (_lse_lane(h) + dup))
                   & (head_of_lane[None, :] == h)
                   for h in range(N_HEADS_A) for dup in (0, N_HEADS_A)).astype(BF16),
        g_out=g_out[0][None], w_out=w_out[0].astype(BF16),
        g_ffn=g_ffn[0][None],
        w_router=jnp.pad(w_router[0], ((0, 0), (0, LANES - N_EXPERTS))).astype(BF16),
        w_gate=w_gate[0].astype(BF16), w_up=w_up[0].astype(BF16), w_down=w_down[0].astype(BF16),
        g_final=g_final[None],
    )


def _encoder(x, mem, w):
    b, seq, _ = x.shape
    t = b * seq
    x2 = x.reshape(t, D_MODEL)
    cos_t, sin_t = _rope_tables(seq)
    mem_k, mem_v = _mem_kv(mem, w["g_mem"], w["w_kv"])
    q4, k4, v4, q16, k16, v16, u, vb, qc = _in_proj(x2, w["g_mix"], w["w_in"], cos_t, sin_t,
                                                    w["ln_g"], w["ln_b"], b, seq)
    pats = [_banded_attention(q4, k4, v4, DILATIONS[1], "attn_d1"),
            _banded_attention(q4, k4, v4, 1, "attn_d4"),
            _banded_attention(q16, k16, v16, 1, "attn_d16")]
    x1, xn, aff, afft = _mix(x2, [p[0] for p in pats], [p[1] for p in pats], u, vb, qc,
                             mem_k, mem_v, w["w_sp"], w["b_sp"], w["expand"], w["g_out"],
                             w["w_out"], w["g_ffn"], w["w_router"], seq)

    cap = EC_CAPACITY_FACTOR * t // N_EXPERTS
    n_tiles = t // MOE_TILE
    thr_col, quota_col, thr_row, quota_row = _threshold(afft, cap)
    lp, lpt, gate, start, off, rows, total = _positions(afft, aff, thr_col, quota_col, thr_row, quota_row)
    scalars = lambda a: a[:, :, 0].astype(jnp.int32).reshape(-1)
    start_i, off_i, rows_i = scalars(start), scalars(off), scalars(rows)
    used = total[:, 0].astype(jnp.int32)
    worst_pad = (BF16_ROWS - 1) * n_tiles
    cap_rows = -(-(cap + worst_pad) // FFN_BLOCK) * FFN_BLOCK
    xe = _dispatch(start_i, off_i, rows_i, xn, lp, cap_rows)
    ye = _ffn(used, xe, w["w_gate"], w["w_up"], w["w_down"])
    y = _combine(start_i, off_i, rows_i, x1, lpt, gate, off, rows, w["g_final"], ye)
    return y.reshape(b, seq, D_MODEL)


def kernel(x_prompt, x_sample, mem_prompt, mem_sample, g_mix, w_in, ln_b_g, ln_b_b, w_spatial,
           b_spatial, g_mem, w_mem_kv, g_out, w_out, g_ffn, w_router, w_gate, w_up, w_down, g_final):
    w = _prepare_weights(g_mix, w_in, ln_b_g, ln_b_b, w_spatial, b_spatial, g_mem, w_mem_kv,
                         g_out, w_out, g_ffn, w_router, w_gate, w_up, w_down, g_final)
    return (_encoder(x_prompt, mem_prompt, w), _encoder(x_sample, mem_sample, w))
```

```python
import functools

import jax
import jax.numpy as jnp
from jax import lax
from jax.experimental import pallas as pl
from jax.experimental.pallas import tpu as pltpu

F32 = jnp.float32
BF16 = jnp.bfloat16

D_MODEL = 1024
N_HEADS_A = 8
HEAD_DIM = 64
D_A = 512
D_B = 256
N_GROUPS_B = 4
CHUNK_B = 128
D_C = 256
N_HEADS_C = 4
N_MEM = 256
N_EXPERTS = 16
EC_CAPACITY_FACTOR = 2
DILATIONS = (1, 4, 16)
RADIUS = 64
Q_BLOCK = 128
ROPE_THETA = 10000.0
EPS = 1e-6
NEG_INF = -1e30
LOG2_E = 1.4426950408889634

LANES = 128
BF16_ROWS = 16
TOKEN_TILE = 512
ATTN_TILE = 512
MOE_TILE = 256
POSITION_TILES = 4
STACK_ROWS = 768
FILL_ROWS = 64
RUN_ROWS = 128
RUN_PIECES = (8, 4, 2, 1)
FFN_BLOCK = 512
FFN_VMEM_BYTES = 52 * 1024 * 1024
UNSELECTED = -4096.0
BINADE_STEPS = (64, 32, 16, 8, 4, 2, 1)
MANTISSA_STEPS = 52


def _rms(x, g):
    return x * lax.rsqrt(jnp.mean(x * x, axis=-1, keepdims=True) + EPS) * g


def _mem_kv_kernel(mem_ref, g_ref, w_ref, k_ref, v_ref):
    h = _rms(mem_ref[0], g_ref[...]).astype(BF16)
    kv = jnp.dot(h, w_ref[...], preferred_element_type=F32)
    k_ref[0] = kv[:, :D_C].astype(BF16)
    v_ref[0] = kv[:, D_C:].astype(BF16)


def _mem_kv(mem, g_mem, w_kv):
    b = mem.shape[0]
    return pl.pallas_call(
        _mem_kv_kernel,
        grid=(b,),
        in_specs=[
            pl.BlockSpec((1, N_MEM, D_MODEL), lambda i: (i, 0, 0)),
            pl.BlockSpec((1, D_MODEL), lambda i: (0, 0)),
            pl.BlockSpec((D_MODEL, 2 * D_C), lambda i: (0, 0)),
        ],
        out_specs=[
            pl.BlockSpec((1, N_MEM, D_C), lambda i: (i, 0, 0)),
            pl.BlockSpec((1, N_MEM, D_C), lambda i: (i, 0, 0)),
        ],
        out_shape=[jax.ShapeDtypeStruct((b, N_MEM, D_C), BF16)] * 2,
        name="mem_kv",
    )(mem, g_mem, w_kv)


def _in_proj_kernel(x_ref, g_ref, w_ref, cos_ref, sin_ref, lng_ref, lnb_ref,
                    q4_ref, k4_ref, v4_ref, q16_ref, k16_ref, v16_ref, u_ref, vb_ref, qc_ref,
                    chunks, chunks4):
    tm = x_ref.shape[0]
    d4 = DILATIONS[1]
    n_chunks = D_A // LANES

    def emit_by_residue(z, out4_ref, out16_ref):
        for c in range(n_chunks):
            chunks[c] = z[:, c * LANES:(c + 1) * LANES]
        quarter = tm // d4
        for r in range(d4):
            rows = [chunks[c, pl.ds(r, quarter, stride=d4), :] for c in range(n_chunks)]
            out4_ref[0, r] = jnp.concatenate(rows, axis=1).astype(BF16)
            for c in range(n_chunks):
                chunks4[c, r * quarter:(r + 1) * quarter, :] = rows[c]
        for r in range(d4):
            for a in range(d4):
                rows = [chunks4[c, pl.ds(r * quarter + a, quarter // d4, stride=d4), :]
                        for c in range(n_chunks)]
                out16_ref[0, r + d4 * a] = jnp.concatenate(rows, axis=1).astype(BF16)

    h = _rms(x_ref[...], g_ref[...]).astype(BF16)
    cos = jnp.concatenate([cos_ref[...]] * (D_A // LANES), axis=1)
    sin = jnp.concatenate([sin_ref[...]] * (D_A // LANES), axis=1)
    lane = lax.broadcasted_iota(jnp.int32, (tm, D_A), 1)
    first_half = (lane & (HEAD_DIM - 1)) < (HEAD_DIM // 2)

    def rope(z):
        rot = jnp.where(first_half, pltpu.roll(z, D_A - HEAD_DIM // 2, 1),
                        pltpu.roll(z, HEAD_DIM // 2, 1))
        return z * cos + rot * sin

    def proj(lo, hi):
        return jnp.dot(h, w_ref[:, lo:hi], preferred_element_type=F32)

    emit_by_residue(rope(proj(0, D_A)), q4_ref, q16_ref)
    emit_by_residue(rope(proj(D_A, 2 * D_A)), k4_ref, k16_ref)
    emit_by_residue(proj(2 * D_A, 3 * D_A), v4_ref, v16_ref)
    u_ref[...] = jax.nn.gelu(proj(3 * D_A, 3 * D_A + D_B)).astype(BF16)
    vb = jax.nn.gelu(proj(3 * D_A + D_B, 3 * D_A + 2 * D_B))
    mu = jnp.mean(vb, axis=-1, keepdims=True)
    var = jnp.mean(jnp.square(vb - mu), axis=-1, keepdims=True)
    vb_ref[...] = ((vb - mu) * lax.rsqrt(var + EPS) * lng_ref[...] + lnb_ref[...]).astype(BF16)
    qc_ref[...] = proj(3 * D_A + 2 * D_B, 3 * D_A + 2 * D_B + D_C).astype(BF16)


def _residue_spec(dil, tm, tiles_per_seq, width):
    return pl.BlockSpec((1, dil, tm // dil, width),
                        lambda i: (i // tiles_per_seq, 0, i % tiles_per_seq, 0))


def _in_proj(x2, g_mix, w_in, cos_t, sin_t, ln_g, ln_b, b, seq):
    t = x2.shape[0]
    tm = TOKEN_TILE
    d_in = w_in.shape[1]
    tiles_per_seq = seq // tm
    row = lambda i: (i, 0)
    const = lambda i: (0, 0)
    pos = lambda i: (i % tiles_per_seq, 0)
    d4, d16 = DILATIONS[1], DILATIONS[2]
    res4 = _residue_spec(d4, tm, tiles_per_seq, D_A)
    res16 = _residue_spec(d16, tm, tiles_per_seq, D_A)
    shape4 = jax.ShapeDtypeStruct((b, d4, seq // d4, D_A), BF16)
    shape16 = jax.ShapeDtypeStruct((b, d16, seq // d16, D_A), BF16)
    return pl.pallas_call(
        _in_proj_kernel,
        grid=(t // tm,),
        in_specs=[
            pl.BlockSpec((tm, D_MODEL), row),
            pl.BlockSpec((1, D_MODEL), const),
            pl.BlockSpec((D_MODEL, d_in), const),
            pl.BlockSpec((tm, LANES), pos),
            pl.BlockSpec((tm, LANES), pos),
            pl.BlockSpec((1, D_B), const),
            pl.BlockSpec((1, D_B), const),
        ],
        out_specs=[res4] * 3 + [res16] * 3 + [pl.BlockSpec((tm, w), row) for w in (D_B, D_B, D_C)],
        out_shape=[shape4] * 3 + [shape16] * 3
                  + [jax.ShapeDtypeStruct((t, w), BF16) for w in (D_B, D_B, D_C)],
        scratch_shapes=[pltpu.VMEM((D_A // LANES, tm, LANES), F32)] * 2,
        name="in_proj",
    )(x2, g_mix, w_in, cos_t, sin_t, ln_g, ln_b)


def _lse_lane(head):
    return head + HEAD_DIM * (1 - head % 2)


def _attn_kernel(q_ref, kp_ref, kc_ref, kn_ref, vp_ref, vc_ref, vn_ref,
                 o_ref, lse_ref, kwin, vwin, bias, scores, probs, dens, *, nres, tr, sub):
    j = pl.program_id(2)
    halo = RADIUS // nres
    qb = Q_BLOCK // nres
    for win, (p_ref, c_ref, n_ref) in ((kwin, (kp_ref, kc_ref, kn_ref)), (vwin, (vp_ref, vc_ref, vn_ref))):
        win[:, 0:halo] = p_ref[0]
        win[:, halo:halo + tr] = c_ref[0]
        win[:, halo + tr:] = n_ref[0]

    t_idx = lax.broadcasted_iota(jnp.int32, (Q_BLOCK, 2 * Q_BLOCK), 0)
    s_idx = lax.broadcasted_iota(jnp.int32, (Q_BLOCK, 2 * Q_BLOCK), 1)
    q_res, q_row = t_idx >> (qb.bit_length() - 1), t_idx & (qb - 1)
    k_res, k_row = s_idx >> qb.bit_length(), s_idx & (2 * qb - 1)
    band = jnp.abs(nres * (q_row - k_row + halo) + q_res - k_res) <= RADIUS
    k_elem = nres * (k_row - halo) + k_res
    lane = lax.broadcasted_iota(jnp.int32, (Q_BLOCK, LANES), 1)
    low_half = lane < HEAD_DIM

    def block(i, carry):
        r0 = pl.multiple_of(i * qb, qb)
        kpos = k_elem + nres * (j * tr + r0)
        bias[...] = jnp.where(band & (kpos >= 0) & (kpos < sub), 0.0, NEG_INF)
        stack = lambda pieces: pieces[0] if nres == 1 else jnp.concatenate(pieces, axis=0)
        pair_cols = lambda hp: slice(hp * LANES, (hp + 1) * LANES)

        for hp in range(N_HEADS_A // 2):
            q_pair = stack([q_ref[0, r, pl.ds(r0, qb), pair_cols(hp)] for r in range(nres)])
            k_pair = stack([kwin[r, pl.ds(r0, 2 * qb), pair_cols(hp)] for r in range(nres)])
            for half in range(2):
                keep = low_half if half == 0 else jnp.logical_not(low_half)
                qm = jnp.where(keep, q_pair, jnp.zeros_like(q_pair))
                scores[2 * hp + half] = lax.dot_general(qm, k_pair, (((1,), (1,)), ((), ())),
                                                        preferred_element_type=F32)

        lse_blk = jnp.zeros((Q_BLOCK, LANES), F32)
        for head in range(N_HEADS_A):
            s = scores[head] + bias[...]
            m = jnp.max(s, axis=-1, keepdims=True)
            p = jnp.exp2(s - m)
            den = jnp.sum(p, axis=-1, keepdims=True)
            probs[head] = p.astype(BF16)
            dens[head] = jnp.broadcast_to(den, (Q_BLOCK, LANES))
            hit = (lane == _lse_lane(head)) | (lane == _lse_lane(head) + N_HEADS_A)
            lse_blk = jnp.where(hit, m + jnp.log(den) * LOG2_E, lse_blk)
        for r in range(nres):
            lse_ref[0, r, pl.ds(r0, qb), :] = lse_blk[r * qb:(r + 1) * qb]

        for hp in range(N_HEADS_A // 2):
            v_pair = stack([vwin[r, pl.ds(r0, 2 * qb), pair_cols(hp)] for r in range(nres)])
            halves = [jnp.dot(probs[2 * hp + half], v_pair, preferred_element_type=F32)
                      / dens[2 * hp + half] for half in range(2)]
            out = jnp.where(low_half, halves[0], halves[1]).astype(BF16)
            for r in range(nres):
                o_ref[0, r, pl.ds(r0, qb), pair_cols(hp)] = out[r * qb:(r + 1) * qb]
        return carry

    lax.fori_loop(0, tr // qb, block, 0)


def _banded_attention(q, k, v, nres, name):
    b, n_res, rows, _ = q.shape
    tr = min(ATTN_TILE // nres, rows)
    halo = RADIUS // nres
    halo_per_tile = tr // halo
    last_halo = rows // halo - 1
    if nres == 1:
        grid = (b, n_res, rows // tr)
        at = lambda row_block: (lambda bi, r, j: (bi, r, row_block(j), 0))
    else:
        grid = (b, 1, rows // tr)
        at = lambda row_block: (lambda bi, r, j: (bi, 0, row_block(j), 0))
    cur = pl.BlockSpec((1, nres, tr, D_A), at(lambda j: j))
    prev = pl.BlockSpec((1, nres, halo, D_A), at(lambda j: jnp.maximum(j * halo_per_tile - 1, 0)))
    nxt = pl.BlockSpec((1, nres, halo, D_A),
                       at(lambda j: jnp.minimum((j + 1) * halo_per_tile, last_halo)))
    return pl.pallas_call(
        functools.partial(_attn_kernel, nres=nres, tr=tr, sub=nres * rows),
        grid=grid,
        in_specs=[cur, prev, cur, nxt, prev, cur, nxt],
        out_specs=[cur, pl.BlockSpec((1, nres, tr, LANES), at(lambda j: j))],
        out_shape=[
            jax.ShapeDtypeStruct((b, n_res, rows, D_A), BF16),
            jax.ShapeDtypeStruct((b, n_res, rows, LANES), F32),
        ],
        scratch_shapes=[pltpu.VMEM((nres, tr + 2 * halo, D_A), BF16)] * 2
                       + [pltpu.VMEM((Q_BLOCK, 2 * Q_BLOCK), F32),
                          pltpu.VMEM((N_HEADS_A, Q_BLOCK, 2 * Q_BLOCK), F32),
                          pltpu.VMEM((N_HEADS_A, Q_BLOCK, 2 * Q_BLOCK), BF16),
                          pltpu.VMEM((N_HEADS_A, Q_BLOCK, LANES), F32)],
        name=name,
    )(q, k, k, k, v, v, v)


def _split_bf16(x):
    hi = x.astype(BF16)
    lo = (x - hi.astype(F32)).astype(BF16)
    return hi, lo


def _mix_kernel(x_ref, o1_ref, o2_ref, o3_ref, l1_ref, l2_ref, l3_ref, u_ref, vb_ref, qc_ref,
                mk_ref, mv_ref, ws_ref, bs_ref, expand_ref, go_ref, wo_ref, gf_ref, wr_ref,
                x1_ref, xn_ref, aff_ref, afft_ref, obuf1, obuf2, obuf3, lbuf1, lbuf2, lbuf3):
    tm = x_ref.shape[0]

    def token_order(ref, buf):
        dil, width = ref.shape[1], ref.shape[3]
        for r in range(dil):
            piece = ref[0, r].astype(F32)
            for c in range(width // LANES):
                buf[c, pl.ds(r, tm // dil, stride=dil), :] = piece[:, c * LANES:(c + 1) * LANES]
        return jnp.concatenate([buf[c] for c in range(width // LANES)], axis=1)

    l1, l2, l3 = token_order(l1_ref, lbuf1), token_order(l2_ref, lbuf2), token_order(l3_ref, lbuf3)
    mx = jnp.maximum(jnp.maximum(l1, l2), l3)
    e1, e2, e3 = jnp.exp2(l1 - mx), jnp.exp2(l2 - mx), jnp.exp2(l3 - mx)
    tot = e1 + e2 + e3
    is_hi_lane = (lax.broadcasted_iota(jnp.int32, (tm, LANES), 1) & N_HEADS_A) == 0
    o_a = jnp.zeros((tm, D_A), F32)
    for e, o_ref, buf in ((e1, o1_ref, obuf1), (e2, o2_ref, obuf2), (e3, o3_ref, obuf3)):
        hi, lo = _split_bf16(e / tot)
        w = jnp.dot(jnp.where(is_hi_lane, hi, lo), expand_ref[...], preferred_element_type=F32)
        o_a = o_a + w * token_order(o_ref, buf)

    lane_b = lax.broadcasted_iota(jnp.int32, (CHUNK_B, D_B), 1)
    group_w = D_B // N_GROUPS_B
    gated = []
    for c in range(tm // CHUNK_B):
        vchunk = vb_ref[c * CHUNK_B:(c + 1) * CHUNK_B, :]
        acc = bs_ref[...]
        for g in range(N_GROUPS_B):
            y = jnp.dot(ws_ref[g], vchunk, preferred_element_type=F32)
            acc = acc + jnp.where(lane_b // group_w == g, y, 0.0)
        gated.append(acc)
    o_b = u_ref[...].astype(F32) * jnp.concatenate(gated, axis=0)

    lane_c = lax.broadcasted_iota(jnp.int32, (tm, LANES), 1)
    low_half = lane_c < HEAD_DIM
    oc_parts = []
    for hp in range(N_HEADS_C // 2):
        cols = slice(hp * LANES, (hp + 1) * LANES)
        q_pair = qc_ref[:, cols]
        k_pair = mk_ref[0, :, cols]
        v_pair = mv_ref[0, :, cols]
        halves = []
        for half in range(2):
            keep = low_half if half == 0 else jnp.logical_not(low_half)
            qm = jnp.where(keep, q_pair, jnp.zeros_like(q_pair))
            s = lax.dot_general(qm, k_pair, (((1,), (1,)), ((), ())), preferred_element_type=F32)
            m = jnp.max(s, axis=-1, keepdims=True)
            p = jnp.exp(s - m)
            p = p / jnp.sum(p, axis=-1, keepdims=True)
            halves.append(jnp.dot(p.astype(BF16), v_pair, preferred_element_type=F32))
        oc_parts.append(jnp.where(low_half, halves[0], halves[1]))
    o_c = jnp.concatenate(oc_parts, axis=1)

    go = go_ref[...]
    o = jnp.concatenate([_rms(o_a, go[:, :D_A]), _rms(o_b, go[:, D_A:D_A + D_B]),
                         _rms(o_c, go[:, D_A + D_B:])], axis=1).astype(BF16)
    x1 = x_ref[...] + jnp.dot(o, wo_ref[...], preferred_element_type=F32)
    x1_ref[...] = x1

    xn = _rms(x1, gf_ref[...]).astype(BF16)
    xn_ref[...] = xn
    logits = jnp.dot(xn, wr_ref[...], preferred_element_type=F32)
    is_expert = lane_c < N_EXPERTS
    logits = jnp.where(is_expert, logits, NEG_INF)
    m = jnp.max(logits, axis=-1, keepdims=True)
    ex = jnp.where(is_expert, jnp.exp(logits - m), 0.0)
    aff = ex / jnp.sum(ex, axis=-1, keepdims=True)
    aff_ref[...] = aff
    afft_ref[...] = jnp.transpose(aff)[:N_EXPERTS, :]


def _mix(x2, o_pats, lse_pats, u, vb, qc, mem_k, mem_v, w_sp, b_sp, expand, g_out, w_out,
         g_ffn, w_router, seq):
    t = x2.shape[0]
    tm = TOKEN_TILE
    tiles_per_seq = seq // tm
    row = lambda i: (i, 0)
    const2 = lambda i: (0, 0)
    const3 = lambda i: (0, 0, 0)
    batch = lambda i: (i // tiles_per_seq, 0, 0)
    dils = [o.shape[1] for o in o_pats]
    return pl.pallas_call(
        _mix_kernel,
        grid=(t // tm,),
        in_specs=[
            pl.BlockSpec((tm, D_MODEL), row),
            *[_residue_spec(d, tm, tiles_per_seq, D_A) for d in dils],
            *[_residue_spec(d, tm, tiles_per_seq, LANES) for d in dils],
            pl.BlockSpec((tm, D_B), row), pl.BlockSpec((tm, D_B), row), pl.BlockSpec((tm, D_C), row),
            pl.BlockSpec((1, N_MEM, D_C), batch), pl.BlockSpec((1, N_MEM, D_C), batch),
            pl.BlockSpec((N_GROUPS_B, CHUNK_B, CHUNK_B), const3),
            pl.BlockSpec((CHUNK_B, D_B), const2),
            pl.BlockSpec((LANES, D_A), const2),
            pl.BlockSpec((1, D_MODEL), const2),
            pl.BlockSpec((D_MODEL, D_MODEL), const2),
            pl.BlockSpec((1, D_MODEL), const2),
            pl.BlockSpec((D_MODEL, LANES), const2),
        ],
        out_specs=[
            pl.BlockSpec((tm, D_MODEL), row),
            pl.BlockSpec((tm, D_MODEL), row),
            pl.BlockSpec((tm, LANES), row),
            pl.BlockSpec((N_EXPERTS, tm), lambda i: (0, i)),
        ],
        out_shape=[
            jax.ShapeDtypeStruct((t, D_MODEL), F32),
            jax.ShapeDtypeStruct((t, D_MODEL), BF16),
            jax.ShapeDtypeStruct((t, LANES), F32),
            jax.ShapeDtypeStruct((N_EXPERTS, t), F32),
        ],
        scratch_shapes=[pltpu.VMEM((D_A // LANES, tm, LANES), F32)] * 3
                       + [pltpu.VMEM((1, tm, LANES), F32)] * 3,
        name="mix",
    )(x2, *o_pats, *lse_pats, u, vb, qc, mem_k, mem_v, w_sp, b_sp, expand, g_out, w_out,
      g_ffn, w_router)


def _threshold_kernel(afft_ref, thr_col_ref, quota_col_ref, thr_row_ref, quota_row_ref, *, cap):
    aff = afft_ref[...]

    def enough(v):
        return jnp.sum(jnp.where(aff >= v, 1.0, 0.0), axis=-1, keepdims=True) >= cap

    hi = jnp.full((N_EXPERTS, 1), 2.0, F32)
    for shift in BINADE_STEPS:
        cand = hi * (2.0 ** -shift)
        hi = jnp.where(enough(cand), hi, cand)
    lo = jnp.where(enough(hi * 0.5), hi * 0.5, 0.0)

    def bisect(_, bracket):
        lo, hi = bracket
        mid = (lo + hi) * 0.5
        ok = enough(mid)
        return jnp.where(ok, mid, lo), jnp.where(ok, hi, mid)

    lo, hi = lax.fori_loop(0, MANTISSA_STEPS, bisect, (lo, hi))
    thr_f = jnp.min(jnp.where(aff >= lo, aff, jnp.inf), axis=-1, keepdims=True)
    n_gt = jnp.sum(jnp.where(aff > thr_f, 1.0, 0.0), axis=-1, keepdims=True)
    quota = cap - n_gt
    thr_col = jnp.broadcast_to(thr_f, (N_EXPERTS, LANES))
    quota_col = jnp.broadcast_to(quota, (N_EXPERTS, LANES))
    thr_col_ref[...] = thr_col
    quota_col_ref[...] = quota_col
    diag = (lax.broadcasted_iota(jnp.int32, (N_EXPERTS, LANES), 0)
            == lax.broadcasted_iota(jnp.int32, (N_EXPERTS, LANES), 1))
    thr_row = jnp.sum(jnp.where(diag, thr_col, 0.0), axis=0, keepdims=True)
    quota_row = jnp.sum(jnp.where(diag, quota_col, 0.0), axis=0, keepdims=True)
    thr_row_ref[...] = jnp.broadcast_to(thr_row, (8, LANES))
    quota_row_ref[...] = jnp.broadcast_to(quota_row, (8, LANES))


def _threshold(afft, cap):
    t = afft.shape[1]
    full = lambda shape: pl.BlockSpec(shape, lambda: (0,) * len(shape))
    return pl.pallas_call(
        functools.partial(_threshold_kernel, cap=float(cap)),
        in_specs=[full((N_EXPERTS, t))],
        out_specs=[full((N_EXPERTS, LANES)), full((N_EXPERTS, LANES)),
                   full((8, LANES)), full((8, LANES))],
        out_shape=[jax.ShapeDtypeStruct((N_EXPERTS, LANES), F32)] * 2
                  + [jax.ShapeDtypeStruct((8, LANES), F32)] * 2,
        name="threshold",
    )(afft)


def _positions_kernel(afft_ref, aff_ref, thr_col_ref, quota_col_ref, thr_row_ref, quota_row_ref,
                      lp_ref, lpt_ref, gate_ref, start_ref, off_ref, padded_ref, total_ref,
                      eq_col, eq_row, slot_col):
    i = pl.program_id(0)
    tm = MOE_TILE

    @pl.when(i == 0)
    def _():
        eq_col[...] = jnp.zeros_like(eq_col)
        eq_row[...] = jnp.zeros_like(eq_row)
        slot_col[...] = jnp.zeros_like(slot_col)

    r = lax.broadcasted_iota(jnp.int32, (tm, tm), 0)
    c = lax.broadcasted_iota(jnp.int32, (tm, tm), 1)
    before = jnp.where(r < c, 1.0, 0.0).astype(BF16)
    after = jnp.where(c < r, 1.0, 0.0).astype(BF16)
    thr, quota = thr_col_ref[:, 0:1], quota_col_ref[:, 0:1]
    thr_r, quota_r = thr_row_ref[0:1, :], quota_row_ref[0:1, :]
    is_expert = lax.broadcasted_iota(jnp.int32, (tm, LANES), 1) < N_EXPERTS
    earlier_expert = (lax.broadcasted_iota(jnp.int32, (N_EXPERTS, LANES), 1)
                      < lax.broadcasted_iota(jnp.int32, (N_EXPERTS, LANES), 0))
    pad = lambda n: jnp.ceil(n / BF16_ROWS) * BF16_ROWS

    eq_seen_col, eq_seen_row, first_slot = eq_col[:, 0:1], eq_row[0:1, :], slot_col[...]
    for s in range(aff_ref.shape[0] // tm):
        a = afft_ref[:, s * tm:(s + 1) * tm]
        eq = a == thr
        eq_f = jnp.where(eq, 1.0, 0.0)
        eq_rank = jnp.dot(eq_f.astype(BF16), before, preferred_element_type=F32) + eq_seen_col
        sel = (a > thr) | (eq & (eq_rank < quota))
        sel_f = jnp.where(sel, 1.0, 0.0)
        lp = jnp.dot(sel_f.astype(BF16), before, preferred_element_type=F32)
        lp_ref[:, s * tm:(s + 1) * tm] = jnp.where(sel, lp, UNSELECTED)
        padded = pad(jnp.sum(sel_f, axis=-1, keepdims=True))
        start_ref[s] = first_slot
        padded_ref[s] = jnp.broadcast_to(padded, (N_EXPERTS, LANES))
        first_slot = first_slot + padded
        eq_seen_col = eq_seen_col + jnp.sum(eq_f, axis=-1, keepdims=True)

        at = aff_ref[s * tm:(s + 1) * tm, :]
        eq_t = (at == thr_r) & is_expert
        eq_tf = jnp.where(eq_t, 1.0, 0.0)
        eq_rank_t = jnp.dot(after, eq_tf.astype(BF16), preferred_element_type=F32) + eq_seen_row
        sel_t = ((at > thr_r) & is_expert) | (eq_t & (eq_rank_t < quota_r))
        sel_tf = jnp.where(sel_t, 1.0, 0.0)
        lp_t = jnp.dot(after, sel_tf.astype(BF16), preferred_element_type=F32)
        lpt_ref[s * tm:(s + 1) * tm, :] = jnp.where(sel_t, lp_t, UNSELECTED)
        gate_ref[s * tm:(s + 1) * tm, :] = jnp.where(sel_t, at, 0.0)
        eq_seen_row = eq_seen_row + jnp.sum(eq_tf, axis=0, keepdims=True)
        padded_row = pad(jnp.sum(sel_tf, axis=0, keepdims=True))
        off = jnp.sum(jnp.where(earlier_expert, padded_row, 0.0), axis=-1, keepdims=True)
        off_ref[s] = jnp.broadcast_to(off, (N_EXPERTS, LANES))

    eq_col[...] = jnp.broadcast_to(eq_seen_col, eq_col.shape)
    eq_row[...] = jnp.broadcast_to(eq_seen_row, eq_row.shape)
    slot_col[...] = first_slot
    total_ref[...] = first_slot


def _positions(afft, aff, thr_col, quota_col, thr_row, quota_row):
    t = aff.shape[0]
    tm = MOE_TILE * POSITION_TILES
    n_tiles = t // MOE_TILE
    c2 = lambda i: (0, 0)
    per_tile = pl.BlockSpec((POSITION_TILES, N_EXPERTS, LANES), lambda i: (i, 0, 0))
    per_tile_shape = jax.ShapeDtypeStruct((n_tiles, N_EXPERTS, LANES), F32)
    return pl.pallas_call(
        _positions_kernel,
        grid=(t // tm,),
        in_specs=[
            pl.BlockSpec((N_EXPERTS, tm), lambda i: (0, i)),
            pl.BlockSpec((tm, LANES), lambda i: (i, 0)),
            pl.BlockSpec((N_EXPERTS, LANES), c2), pl.BlockSpec((N_EXPERTS, LANES), c2),
            pl.BlockSpec((8, LANES), c2), pl.BlockSpec((8, LANES), c2),
        ],
        out_specs=[
            pl.BlockSpec((N_EXPERTS, tm), lambda i: (0, i)),
            pl.BlockSpec((tm, LANES), lambda i: (i, 0)),
            pl.BlockSpec((tm, LANES), lambda i: (i, 0)),
            per_tile, per_tile, per_tile,
            pl.BlockSpec((N_EXPERTS, LANES), c2),
        ],
        out_shape=[
            jax.ShapeDtypeStruct((N_EXPERTS, t), F32),
            jax.ShapeDtypeStruct((t, LANES), F32),
            jax.ShapeDtypeStruct((t, LANES), F32),
            per_tile_shape,
            per_tile_shape,
            per_tile_shape,
            jax.ShapeDtypeStruct((N_EXPERTS, LANES), F32),
        ],
        scratch_shapes=[pltpu.VMEM((N_EXPERTS, LANES), F32), pltpu.VMEM((8, LANES), F32),
                        pltpu.VMEM((N_EXPERTS, LANES), F32)],
        compiler_params=pltpu.CompilerParams(dimension_semantics=("arbitrary",)),
        name="positions",
    )(afft, aff, thr_col, quota_col, thr_row, quota_row)


class _TileTables:
    def __init__(self, start_ref, off_ref, rows_ref):
        self.start_ref, self.off_ref, self.rows_ref = start_ref, off_ref, rows_ref

    def slot(self, tile, e, k):
        return pl.multiple_of(self.start_ref[tile * N_EXPERTS + e] + k * BF16_ROWS, BF16_ROWS)

    def stack_rows(self, tile):
        last = tile * N_EXPERTS + N_EXPERTS - 1
        return self.off_ref[last] + self.rows_ref[last]

    def fits(self, tile):
        return self.stack_rows(tile) <= STACK_ROWS

    def regular(self, tile):
        longest = self.rows_ref[tile * N_EXPERTS]
        for e in range(1, N_EXPERTS):
            longest = jnp.maximum(longest, self.rows_ref[tile * N_EXPERTS + e])
        return self.fits(tile) & (longest <= RUN_ROWS)

    def for_each_piece(self, tile, fn):
        for e in range(N_EXPERTS):
            n_groups = self.rows_ref[tile * N_EXPERTS + e] // BF16_ROWS
            row = self.off_ref[tile * N_EXPERTS + e]
            slot = self.start_ref[tile * N_EXPERTS + e]
            for groups in RUN_PIECES:
                take = (n_groups & groups) != 0
                n_rows = groups * BF16_ROWS

                @pl.when(take)
                def _(e=e, row=row, slot=slot, n_rows=n_rows):
                    fn(e, pl.multiple_of(row, BF16_ROWS), pl.multiple_of(slot, BF16_ROWS), n_rows)

                step = jnp.where(take, n_rows, 0)
                row, slot = row + step, slot + step

    def for_each_group(self, tile, base, fn):
        for e in range(N_EXPERTS):
            off = self.off_ref[tile * N_EXPERTS + e]
            n_groups = self.rows_ref[tile * N_EXPERTS + e] // BF16_ROWS

            def body(k, carry, e=e, off=off):
                s = off + k * BF16_ROWS - base

                @pl.when((s >= 0) & (s < STACK_ROWS))
                def _():
                    fn(e, k, pl.multiple_of(s, BF16_ROWS))
                return carry

            lax.fori_loop(0, n_groups, body, 0)


def _dispatch_kernel(start_ref, off_ref, rows_ref, xn_ref, lp_ref, xe_ref,
                     onehot, stack, zeros, sem, sem_fill):
    i = pl.program_id(0)
    n = pl.num_programs(0)
    slot = i % 2
    tm = xn_ref.shape[0]
    tables = _TileTables(start_ref, off_ref, rows_ref)
    def send(buf, e, row, dst, n_rows):
        return pltpu.make_async_copy(stack.at[buf, pl.ds(row, n_rows), :],
                                     xe_ref.at[e, pl.ds(dst, n_rows), :], sem.at[buf])

    def compact():
        stack[slot] = jnp.dot(onehot[0:STACK_ROWS, :], xn_ref[...],
                              preferred_element_type=F32).astype(BF16)

    @pl.when(i == 0)
    def _():
        onehot[...] = jnp.zeros_like(onehot)

    @pl.when((i >= 2) & tables.regular(jnp.maximum(i - 2, 0)))
    def _():
        tables.for_each_piece(i - 2, lambda e, row, dst, n: send(slot, e, row, dst, n).wait())

    @pl.when(tables.regular(i))
    def _():
        run_iota = lax.broadcasted_iota(jnp.int32, (RUN_ROWS, tm), 0).astype(F32)
        for e in range(N_EXPERTS):
            row = pl.multiple_of(off_ref[i * N_EXPERTS + e], BF16_ROWS)
            onehot[pl.ds(row, RUN_ROWS), :] = jnp.where(lp_ref[e:e + 1, :] == run_iota,
                                                        1.0, 0.0).astype(BF16)
        compact()
        tables.for_each_piece(i, lambda e, row, dst, n: send(slot, e, row, dst, n).start())

    @pl.when(jnp.logical_not(tables.regular(i)))
    def _():
        group_iota = lax.broadcasted_iota(jnp.int32, (BF16_ROWS, tm), 0).astype(F32)
        n_rounds = (tables.stack_rows(i) + STACK_ROWS - 1) // STACK_ROWS

        def one_round(r, carry):
            base = r * STACK_ROWS
            onehot[...] = jnp.zeros_like(onehot)

            def mark(e, k, s):
                hit = lp_ref[e:e + 1, :] == group_iota + (k * BF16_ROWS).astype(F32)
                onehot[pl.ds(s, BF16_ROWS), :] = jnp.where(hit, 1.0, 0.0).astype(BF16)

            def group_copy(e, k, s):
                return send(slot, e, s, tables.slot(i, e, k), BF16_ROWS)

            tables.for_each_group(i, base, mark)
            compact()
            tables.for_each_group(i, base, lambda e, k, s: group_copy(e, k, s).start())
            tables.for_each_group(i, base, lambda e, k, s: group_copy(e, k, s).wait())
            return carry

        lax.fori_loop(0, n_rounds, one_round, 0)

    @pl.when(i == n - 1)
    def _():
        @pl.when((i >= 1) & tables.regular(jnp.maximum(i - 1, 0)))
        def _():
            tables.for_each_piece(i - 1, lambda e, row, dst, n: send(1 - slot, e, row, dst, n).wait())

        @pl.when(tables.regular(i))
        def _():
            tables.for_each_piece(i, lambda e, row, dst, n: send(slot, e, row, dst, n).wait())

        zeros[...] = jnp.zeros_like(zeros)
        cap_rows = xe_ref.shape[1]
        big = zeros.shape[0]

        def fill(e, row0, rows):
            return pltpu.make_async_copy(zeros.at[pl.ds(0, rows), :],
                                         xe_ref.at[e, pl.ds(row0, rows), :], sem_fill.at[0])

        for e in range(N_EXPERTS):
            used = start_ref[i * N_EXPERTS + e] + rows_ref[i * N_EXPERTS + e]
            n_small = ((-used) & (big - 1)) // BF16_ROWS
            base = used + n_small * BF16_ROWS
            n_big = (cap_rows - base) // big

            def small_row(k):
                return pl.multiple_of(used + k * BF16_ROWS, BF16_ROWS)

            def big_row(k):
                return pl.multiple_of(base + k * big, big)

            lax.fori_loop(0, n_small, lambda k, c: (fill(e, small_row(k), BF16_ROWS).start(), c)[1], 0)
            lax.fori_loop(0, n_big, lambda k, c: (fill(e, big_row(k), big).start(), c)[1], 0)
            lax.fori_loop(0, n_small, lambda k, c: (fill(e, small_row(k), BF16_ROWS).wait(), c)[1], 0)
            lax.fori_loop(0, n_big, lambda k, c: (fill(e, big_row(k), big).wait(), c)[1], 0)


def _dispatch(start, off, rows, xn, lp, cap_rows):
    t = xn.shape[0]
    tm = MOE_TILE
    return pl.pallas_call(
        _dispatch_kernel,
        grid_spec=pltpu.PrefetchScalarGridSpec(
            num_scalar_prefetch=3,
            grid=(t // tm,),
            in_specs=[
                pl.BlockSpec((tm, D_MODEL), lambda i, *_: (i, 0)),
                pl.BlockSpec((N_EXPERTS, tm), lambda i, *_: (0, i)),
            ],
            out_specs=pl.BlockSpec(memory_space=pl.ANY),
            scratch_shapes=[pltpu.VMEM((STACK_ROWS + RUN_ROWS, tm), BF16),
                            pltpu.VMEM((2, STACK_ROWS, D_MODEL), BF16),
                            pltpu.VMEM((FILL_ROWS, D_MODEL), BF16),
                            pltpu.SemaphoreType.DMA((2,)),
                            pltpu.SemaphoreType.DMA((1,))],
        ),
        out_shape=jax.ShapeDtypeStruct((N_EXPERTS, cap_rows, D_MODEL), BF16),
        compiler_params=pltpu.CompilerParams(dimension_semantics=("arbitrary",)),
        name="dispatch",
    )(start, off, rows, xn, lp)


def _ffn_kernel(used_ref, xe_ref, wg_ref, wu_ref, wd_ref, ye_ref, wg, wu, wd):
    e, j = pl.program_id(0), pl.program_id(1)
    bs = xe_ref.shape[1]
    n_valid = used_ref[e] - j * bs

    @pl.when(j == 0)
    def _():
        wg[...] = wg_ref[0].astype(BF16)
        wu[...] = wu_ref[0].astype(BF16)
        wd[...] = wd_ref[0].astype(BF16)

    @pl.when(n_valid > 0)
    def _():
        row = lax.broadcasted_iota(jnp.int32, (bs, D_MODEL), 0)
        x = jnp.where(row < n_valid, xe_ref[0], jnp.zeros((bs, D_MODEL), BF16))
        gate = jnp.dot(x, wg[...], preferred_element_type=F32)
        up = jnp.dot(x, wu[...], preferred_element_type=F32)
        hdn = (jax.nn.silu(gate) * up).astype(BF16)
        ye_ref[0] = jnp.dot(hdn, wd[...], preferred_element_type=F32).astype(BF16)

    @pl.when(n_valid <= 0)
    def _():
        ye_ref[0] = jnp.zeros((bs, D_MODEL), BF16)


def _ffn(used, xe, w_gate, w_up, w_down):
    cap_rows = xe.shape[1]
    bs = FFN_BLOCK

    def x_map(e, j, used_ref):
        last = jnp.maximum((used_ref[e] + bs - 1) // bs - 1, 0)
        return (e, jnp.minimum(j, last), 0)

    w_map = lambda e, j, used_ref: (e, 0, 0)
    return pl.pallas_call(
        _ffn_kernel,
        grid_spec=pltpu.PrefetchScalarGridSpec(
            num_scalar_prefetch=1,
            grid=(N_EXPERTS, cap_rows // bs),
            in_specs=[
                pl.BlockSpec((1, bs, D_MODEL), x_map),
                pl.BlockSpec((1, D_MODEL, D_MODEL), w_map),
                pl.BlockSpec((1, D_MODEL, D_MODEL), w_map),
                pl.BlockSpec((1, D_MODEL, D_MODEL), w_map),
            ],
            out_specs=pl.BlockSpec((1, bs, D_MODEL), lambda e, j, used_ref: (e, j, 0)),
            scratch_shapes=[pltpu.VMEM((D_MODEL, D_MODEL), BF16)] * 3,
        ),
        out_shape=jax.ShapeDtypeStruct((N_EXPERTS, cap_rows, D_MODEL), BF16),
        compiler_params=pltpu.CompilerParams(dimension_semantics=("arbitrary", "arbitrary"),
                                             vmem_limit_bytes=FFN_VMEM_BYTES),
        name="expert_ffn",
    )(used, xe, w_gate, w_up, w_down)


def _gate_matrix(lpt_ref, gate_ref, offv_ref, rowsv_ref, base):
    off = offv_ref[0][:, 0:1]
    end = off + rowsv_ref[0][:, 0:1]
    row = lax.broadcasted_iota(jnp.int32, (N_EXPERTS, STACK_ROWS), 1).astype(F32) + base
    owner = (row >= off) & (row < end)
    owner_pad = jnp.concatenate(
        [jnp.where(owner, 1.0, 0.0), jnp.zeros((LANES - N_EXPERTS, STACK_ROWS), F32)], axis=0).astype(BF16)
    slot_of_row = row[0:1, :] - jnp.sum(jnp.where(owner, off, 0.0), axis=0, keepdims=True)
    slot_of_token = jnp.dot(lpt_ref[...].astype(BF16), owner_pad, preferred_element_type=F32)
    gate_of_token = jnp.dot(gate_ref[...].astype(BF16), owner_pad, preferred_element_type=F32)
    return jnp.where(slot_of_token == slot_of_row, gate_of_token, 0.0).astype(BF16)


def _combine_kernel(start_ref, off_ref, rows_ref, x1_ref, lpt_ref, gate_ref, offv_ref, rowsv_ref,
                    gfin_ref, ye_ref, y_ref, stack, extra, sem, sem_extra):
    i = pl.program_id(0)
    n = pl.num_programs(0)
    slot = i % 2
    tables = _TileTables(start_ref, off_ref, rows_ref)

    def fetch(buf, e, row, src, n_rows):
        return pltpu.make_async_copy(ye_ref.at[e, pl.ds(src, n_rows), :],
                                     stack.at[buf, pl.ds(row, n_rows), :], sem.at[buf])

    @pl.when(i == 0)
    def _():
        stack[...] = jnp.zeros_like(stack)
        extra[...] = jnp.zeros_like(extra)

        @pl.when(tables.regular(0))
        def _():
            tables.for_each_piece(0, lambda e, row, src, m: fetch(0, e, row, src, m).start())

    nxt = jnp.minimum(i + 1, n - 1)

    @pl.when((i + 1 < n) & tables.regular(nxt))
    def _():
        tables.for_each_piece(nxt, lambda e, row, src, m: fetch(1 - slot, e, row, src, m).start())

    regular = tables.regular(i)

    @pl.when(regular)
    def _():
        tables.for_each_piece(i, lambda e, row, src, m: fetch(slot, e, row, src, m).wait())

    moe = jnp.dot(_gate_matrix(lpt_ref, gate_ref, offv_ref, rowsv_ref, 0.0), stack[slot],
                  preferred_element_type=F32)
    moe = jnp.where(regular, moe, 0.0)

    n_rounds = jnp.where(regular, 0, (tables.stack_rows(i) + STACK_ROWS - 1) // STACK_ROWS)

    def one_round(r, acc):
        base = r * STACK_ROWS

        def fetch_group(e, k, s):
            return pltpu.make_async_copy(ye_ref.at[e, pl.ds(tables.slot(i, e, k), BF16_ROWS), :],
                                         extra.at[pl.ds(s, BF16_ROWS), :], sem_extra.at[0])

        tables.for_each_group(i, base, lambda e, k, s: fetch_group(e, k, s).start())
        tables.for_each_group(i, base, lambda e, k, s: fetch_group(e, k, s).wait())
        g = _gate_matrix(lpt_ref, gate_ref, offv_ref, rowsv_ref, base.astype(F32))
        return acc + jnp.dot(g, extra[...], preferred_element_type=F32)

    moe = lax.fori_loop(0, n_rounds, one_round, moe)
    y_ref[...] = _rms(x1_ref[...] + moe, gfin_ref[...])


def _combine(start, off, rows, x1, lpt, gate, off_vec, rows_vec, g_final, ye):
    t = x1.shape[0]
    tm = MOE_TILE
    row = lambda i, *_: (i, 0)
    per_tile = pl.BlockSpec((1, N_EXPERTS, LANES), lambda i, *_: (i, 0, 0))
    return pl.pallas_call(
        _combine_kernel,
        grid_spec=pltpu.PrefetchScalarGridSpec(
            num_scalar_prefetch=3,
            grid=(t // tm,),
            in_specs=[
                pl.BlockSpec((tm, D_MODEL), row),
                pl.BlockSpec((tm, LANES), row),
                pl.BlockSpec((tm, LANES), row),
                per_tile, per_tile,
                pl.BlockSpec((1, D_MODEL), lambda i, *_: (0, 0)),
                pl.BlockSpec(memory_space=pl.ANY),
            ],
            out_specs=pl.BlockSpec((tm, D_MODEL), row),
            scratch_shapes=[pltpu.VMEM((2, STACK_ROWS, D_MODEL), BF16),
                            pltpu.VMEM((STACK_ROWS, D_MODEL), BF16),
                            pltpu.SemaphoreType.DMA((2,)),
                            pltpu.SemaphoreType.DMA((1,))],
        ),
        out_shape=jax.ShapeDtypeStruct((t, D_MODEL), F32),
        compiler_params=pltpu.CompilerParams(dimension_semantics=("arbitrary",)),
        name="combine",
    )(start, off, rows, x1, lpt, gate, off_vec, rows_vec, g_final, ye)


def _rope_tables(seq):
    half = HEAD_DIM // 2
    inv_freq = jnp.power(ROPE_THETA, -jnp.arange(half, dtype=F32) * 2.0 / HEAD_DIM)
    ang = jnp.arange(seq, dtype=F32)[:, None] * inv_freq[None, :]
    cos, sin = jnp.cos(ang), jnp.sin(ang)
    reps = LANES // HEAD_DIM
    return (jnp.tile(jnp.concatenate([cos, cos], axis=1), (1, reps)),
            jnp.tile(jnp.concatenate([-sin, sin], axis=1), (1, reps)))


def _prepare_weights(g_mix, w_in, ln_b_g, ln_b_b, w_spatial, b_spatial, g_mem, w_mem_kv,
                     g_out, w_out, g_ffn, w_router, w_gate, w_up, w_down, g_final):
    scale = jnp.ones((w_in.shape[-1],), F32)
    scale = scale.at[:D_A].set(HEAD_DIM ** -0.5 * LOG2_E)
    scale = scale.at[3 * D_A + 2 * D_B:].set((D_C // N_HEADS_C) ** -0.5)
    head_of_lane = jnp.arange(D_A) // HEAD_DIM
    return dict(
        g_mix=g_mix[0][None], w_in=(w_in[0] * scale).astype(BF16),
        ln_g=ln_b_g[0][None], ln_b=ln_b_b[0][None],
        w_sp=w_spatial[0].astype(BF16),
        b_sp=jnp.repeat(b_spatial[0].T, D_B // N_GROUPS_B, axis=1),
        g_mem=g_mem[0][None], w_kv=w_mem_kv[0].astype(BF16),
        expand=sum((jnp.arange(LANES)[:, None] == (_lse_lane(h) + dup))
                   & (head_of_lane[None, :] == h)
                   for h in range(N_HEADS_A) for dup in (0, N_HEADS_A)).astype(BF16),
        g_out=g_out[0][None], w_out=w_out[0].astype(BF16),
        g_ffn=g_ffn[0][None],
        w_router=jnp.pad(w_router[0], ((0, 0), (0, LANES - N_EXPERTS))).astype(BF16),
        w_gate=w_gate[0], w_up=w_up[0], w_down=w_down[0],
        g_final=g_final[None],
    )


def _encoder(x, mem, w):
    b, seq, _ = x.shape
    t = b * seq
    x2 = x.reshape(t, D_MODEL)
    cos_t, sin_t = _rope_tables(seq)
    mem_k, mem_v = _mem_kv(mem, w["g_mem"], w["w_kv"])
    q4, k4, v4, q16, k16, v16, u, vb, qc = _in_proj(x2, w["g_mix"], w["w_in"], cos_t, sin_t,
                                                    w["ln_g"], w["ln_b"], b, seq)
    pats = [_banded_attention(q4, k4, v4, DILATIONS[1], "attn_d1"),
            _banded_attention(q4, k4, v4, 1, "attn_d4"),
            _banded_attention(q16, k16, v16, 1, "attn_d16")]
    x1, xn, aff, afft = _mix(x2, [p[0] for p in pats], [p[1] for p in pats], u, vb, qc,
                             mem_k, mem_v, w["w_sp"], w["b_sp"], w["expand"], w["g_out"],
                             w["w_out"], w["g_ffn"], w["w_router"], seq)

    cap = EC_CAPACITY_FACTOR * t // N_EXPERTS
    n_tiles = t // MOE_TILE
    thr_col, quota_col, thr_row, quota_row = _threshold(afft, cap)
    lp, lpt, gate, start, off, rows, total = _positions(afft, aff, thr_col, quota_col, thr_row, quota_row)
    scalars = lambda a: a[:, :, 0].astype(jnp.int32).reshape(-1)
    start_i, off_i, rows_i = scalars(start), scalars(off), scalars(rows)
    used = total[:, 0].astype(jnp.int32)
    worst_pad = (BF16_ROWS - 1) * n_tiles
    cap_rows = -(-(cap + worst_pad) // FFN_BLOCK) * FFN_BLOCK
    xe = _dispatch(start_i, off_i, rows_i, xn, lp, cap_rows)
    ye = _ffn(used, xe, w["w_gate"], w["w_up"], w["w_down"])
    y = _combine(start_i, off_i, rows_i, x1, lpt, gate, off, rows, w["g_final"], ye)
    return y.reshape(b, seq, D_MODEL)


def kernel(x_prompt, x_sample, mem_prompt, mem_sample, g_mix, w_in, ln_b_g, ln_b_b, w_spatial,
           b_spatial, g_mem, w_mem_kv, g_out, w_out, g_ffn, w_router, w_gate, w_up, w_down, g_final):
    w = _prepare_weights(g_mix, w_in, ln_b_g, ln_b_b, w_spatial, b_spatial, g_mem, w_mem_kv,
                         g_out, w_out, g_ffn, w_router, w_gate, w_up, w_down, g_final)
    return (_encoder(x_prompt, mem_prompt, w), _encoder(x_sample, mem_sample, w))
```

```python
import functools

import jax
import jax.numpy as jnp
from jax import lax
from jax.experimental import pallas as pl
from jax.experimental.pallas import tpu as pltpu

F32 = jnp.float32
BF16 = jnp.bfloat16

D_MODEL = 1024
N_HEADS_A = 8
HEAD_DIM = 64
D_A = 512
D_B = 256
N_GROUPS_B = 4
CHUNK_B = 128
D_C = 256
N_HEADS_C = 4
N_MEM = 256
N_EXPERTS = 16
EC_CAPACITY_FACTOR = 2
DILATIONS = (1, 4, 16)
RADIUS = 64
Q_BLOCK = 128
ROPE_THETA = 10000.0
EPS = 1e-6
NEG_INF = -1e30
LOG2_E = 1.4426950408889634

LANES = 128
BF16_ROWS = 16
TOKEN_TILE = 512
ATTN_TILE = 512
MOE_TILE = 256
POSITION_TILES = 4
STACK_ROWS = 768
FILL_ROWS = 64
RUN_ROWS = 128
RUN_PIECES = (8, 4, 2, 1)
FFN_BLOCK = 512
FFN_VMEM_BYTES = 52 * 1024 * 1024
UNSELECTED = -4096.0
BINADE_STEPS = (64, 32, 16, 8, 4, 2, 1)
MANTISSA_STEPS = 52


def _rms(x, g):
    return x * lax.rsqrt(jnp.mean(x * x, axis=-1, keepdims=True) + EPS) * g


def _mem_kv_kernel(mem_ref, g_ref, w_ref, k_ref, v_ref):
    h = _rms(mem_ref[0], g_ref[...]).astype(BF16)
    kv = jnp.dot(h, w_ref[...], preferred_element_type=F32)
    k_ref[0] = kv[:, :D_C].astype(BF16)
    v_ref[0] = kv[:, D_C:].astype(BF16)


def _mem_kv(mem, g_mem, w_kv):
    b = mem.shape[0]
    return pl.pallas_call(
        _mem_kv_kernel,
        grid=(b,),
        in_specs=[
            pl.BlockSpec((1, N_MEM, D_MODEL), lambda i: (i, 0, 0)),
            pl.BlockSpec((1, D_MODEL), lambda i: (0, 0)),
            pl.BlockSpec((D_MODEL, 2 * D_C), lambda i: (0, 0)),
        ],
        out_specs=[
            pl.BlockSpec((1, N_MEM, D_C), lambda i: (i, 0, 0)),
            pl.BlockSpec((1, N_MEM, D_C), lambda i: (i, 0, 0)),
        ],
        out_shape=[jax.ShapeDtypeStruct((b, N_MEM, D_C), BF16)] * 2,
        name="mem_kv",
    )(mem, g_mem, w_kv)


def _in_proj_kernel(x_ref, g_ref, w_ref, cos_ref, sin_ref, lng_ref, lnb_ref,
                    q4_ref, k4_ref, v4_ref, q16_ref, k16_ref, v16_ref, u_ref, vb_ref, qc_ref,
                    chunks, chunks4):
    tm = x_ref.shape[0]
    d4 = DILATIONS[1]
    n_chunks = D_A // LANES

    def emit_by_residue(z, out4_ref, out16_ref):
        for c in range(n_chunks):
            chunks[c] = z[:, c * LANES:(c + 1) * LANES]
        quarter = tm // d4
        for r in range(d4):
            rows = [chunks[c, pl.ds(r, quarter, stride=d4), :] for c in range(n_chunks)]
            out4_ref[0, r] = jnp.concatenate(rows, axis=1).astype(BF16)
            for c in range(n_chunks):
                chunks4[c, r * quarter:(r + 1) * quarter, :] = rows[c]
        for r in range(d4):
            for a in range(d4):
                rows = [chunks4[c, pl.ds(r * quarter + a, quarter // d4, stride=d4), :]
                        for c in range(n_chunks)]
                out16_ref[0, r + d4 * a] = jnp.concatenate(rows, axis=1).astype(BF16)

    h = _rms(x_ref[...], g_ref[...]).astype(BF16)
    cos = jnp.concatenate([cos_ref[...]] * (D_A // LANES), axis=1)
    sin = jnp.concatenate([sin_ref[...]] * (D_A // LANES), axis=1)
    lane = lax.broadcasted_iota(jnp.int32, (tm, D_A), 1)
    first_half = (lane & (HEAD_DIM - 1)) < (HEAD_DIM // 2)

    def rope(z):
        rot = jnp.where(first_half, pltpu.roll(z, D_A - HEAD_DIM // 2, 1),
                        pltpu.roll(z, HEAD_DIM // 2, 1))
        return z * cos + rot * sin

    def proj(lo, hi):
        return jnp.dot(h, w_ref[:, lo:hi], preferred_element_type=F32)

    emit_by_residue(rope(proj(0, D_A)), q4_ref, q16_ref)
    emit_by_residue(rope(proj(D_A, 2 * D_A)), k4_ref, k16_ref)
    emit_by_residue(proj(2 * D_A, 3 * D_A), v4_ref, v16_ref)
    u_ref[...] = jax.nn.gelu(proj(3 * D_A, 3 * D_A + D_B)).astype(BF16)
    vb = jax.nn.gelu(proj(3 * D_A + D_B, 3 * D_A + 2 * D_B))
    mu = jnp.mean(vb, axis=-1, keepdims=True)
    var = jnp.mean(jnp.square(vb - mu), axis=-1, keepdims=True)
    vb_ref[...] = ((vb - mu) * lax.rsqrt(var + EPS) * lng_ref[...] + lnb_ref[...]).astype(BF16)
    qc_ref[...] = proj(3 * D_A + 2 * D_B, 3 * D_A + 2 * D_B + D_C).astype(BF16)


def _residue_spec(dil, tm, tiles_per_seq, width):
    return pl.BlockSpec((1, dil, tm // dil, width),
                        lambda i: (i // tiles_per_seq, 0, i % tiles_per_seq, 0))


def _in_proj(x2, g_mix, w_in, cos_t, sin_t, ln_g, ln_b, b, seq):
    t = x2.shape[0]
    tm = TOKEN_TILE
    d_in = w_in.shape[1]
    tiles_per_seq = seq // tm
    row = lambda i: (i, 0)
    const = lambda i: (0, 0)
    pos = lambda i: (i % tiles_per_seq, 0)
    d4, d16 = DILATIONS[1], DILATIONS[2]
    res4 = _residue_spec(d4, tm, tiles_per_seq, D_A)
    res16 = _residue_spec(d16, tm, tiles_per_seq, D_A)
    shape4 = jax.ShapeDtypeStruct((b, d4, seq // d4, D_A), BF16)
    shape16 = jax.ShapeDtypeStruct((b, d16, seq // d16, D_A), BF16)
    return pl.pallas_call(
        _in_proj_kernel,
        grid=(t // tm,),
        in_specs=[
            pl.BlockSpec((tm, D_MODEL), row),
            pl.BlockSpec((1, D_MODEL), const),
            pl.BlockSpec((D_MODEL, d_in), const),
            pl.BlockSpec((tm, LANES), pos),
            pl.BlockSpec((tm, LANES), pos),
            pl.BlockSpec((1, D_B), const),
            pl.BlockSpec((1, D_B), const),
        ],
        out_specs=[res4] * 3 + [res16] * 3 + [pl.BlockSpec((tm, w), row) for w in (D_B, D_B, D_C)],
        out_shape=[shape4] * 3 + [shape16] * 3
                  + [jax.ShapeDtypeStruct((t, w), BF16) for w in (D_B, D_B, D_C)],
        scratch_shapes=[pltpu.VMEM((D_A // LANES, tm, LANES), F32)] * 2,
        name="in_proj",
    )(x2, g_mix, w_in, cos_t, sin_t, ln_g, ln_b)


def _lse_lane(head):
    return head + HEAD_DIM * (1 - head % 2)


def _attn_kernel(q_ref, kp_ref, kc_ref, kn_ref, vp_ref, vc_ref, vn_ref,
                 o_ref, lse_ref, kwin, vwin, bias, scores, probs, dens, *, nres, tr, sub):
    j = pl.program_id(2)
    halo = RADIUS // nres
    qb = Q_BLOCK // nres
    for win, (p_ref, c_ref, n_ref) in ((kwin, (kp_ref, kc_ref, kn_ref)), (vwin, (vp_ref, vc_ref, vn_ref))):
        win[:, 0:halo] = p_ref[0]
        win[:, halo:halo + tr] = c_ref[0]
        win[:, halo + tr:] = n_ref[0]

    t_idx = lax.broadcasted_iota(jnp.int32, (Q_BLOCK, 2 * Q_BLOCK), 0)
    s_idx = lax.broadcasted_iota(jnp.int32, (Q_BLOCK, 2 * Q_BLOCK), 1)
    q_res, q_row = t_idx >> (qb.bit_length() - 1), t_idx & (qb - 1)
    k_res, k_row = s_idx >> qb.bit_length(), s_idx & (2 * qb - 1)
    band = jnp.abs(nres * (q_row - k_row + halo) + q_res - k_res) <= RADIUS
    k_elem = nres * (k_row - halo) + k_res
    lane = lax.broadcasted_iota(jnp.int32, (Q_BLOCK, LANES), 1)
    low_half = lane < HEAD_DIM

    def block(i, carry):
        r0 = pl.multiple_of(i * qb, qb)
        kpos = k_elem + nres * (j * tr + r0)
        bias[...] = jnp.where(band & (kpos >= 0) & (kpos < sub), 0.0, NEG_INF)
        stack = lambda pieces: pieces[0] if nres == 1 else jnp.concatenate(pieces, axis=0)
        pair_cols = lambda hp: slice(hp * LANES, (hp + 1) * LANES)

        for hp in range(N_HEADS_A // 2):
            q_pair = stack([q_ref[0, r, pl.ds(r0, qb), pair_cols(hp)] for r in range(nres)])
            k_pair = stack([kwin[r, pl.ds(r0, 2 * qb), pair_cols(hp)] for r in range(nres)])
            for half in range(2):
                keep = low_half if half == 0 else jnp.logical_not(low_half)
                qm = jnp.where(keep, q_pair, jnp.zeros_like(q_pair))
                scores[2 * hp + half] = lax.dot_general(qm, k_pair, (((1,), (1,)), ((), ())),
                                                        preferred_element_type=F32)

        lse_blk = jnp.zeros((Q_BLOCK, LANES), F32)
        for head in range(N_HEADS_A):
            s = scores[head] + bias[...]
            m = jnp.max(s, axis=-1, keepdims=True)
            p = jnp.exp2(s - m)
            den = jnp.sum(p, axis=-1, keepdims=True)
            probs[head] = p.astype(BF16)
            dens[head] = jnp.broadcast_to(den, (Q_BLOCK, LANES))
            hit = (lane == _lse_lane(head)) | (lane == _lse_lane(head) + N_HEADS_A)
            lse_blk = jnp.where(hit, m + jnp.log(den) * LOG2_E, lse_blk)
        for r in range(nres):
            lse_ref[0, r, pl.ds(r0, qb), :] = lse_blk[r * qb:(r + 1) * qb]

        for hp in range(N_HEADS_A // 2):
            v_pair = stack([vwin[r, pl.ds(r0, 2 * qb), pair_cols(hp)] for r in range(nres)])
            halves = [jnp.dot(probs[2 * hp + half], v_pair, preferred_element_type=F32)
                      / dens[2 * hp + half] for half in range(2)]
            out = jnp.where(low_half, halves[0], halves[1]).astype(BF16)
            for r in range(nres):
                o_ref[0, r, pl.ds(r0, qb), pair_cols(hp)] = out[r * qb:(r + 1) * qb]
        return carry

    lax.fori_loop(0, tr // qb, block, 0)


def _banded_attention(q, k, v, nres, name):
    b, n_res, rows, _ = q.shape
    tr = min(ATTN_TILE // nres, rows)
    halo = RADIUS // nres
    halo_per_tile = tr // halo
    last_halo = rows // halo - 1
    if nres == 1:
        grid = (b, n_res, rows // tr)
        at = lambda row_block: (lambda bi, r, j: (bi, r, row_block(j), 0))
    else:
        grid = (b, 1, rows // tr)
        at = lambda row_block: (lambda bi, r, j: (bi, 0, row_block(j), 0))
    cur = pl.BlockSpec((1, nres, tr, D_A), at(lambda j: j))
    prev = pl.BlockSpec((1, nres, halo, D_A), at(lambda j: jnp.maximum(j * halo_per_tile - 1, 0)))
    nxt = pl.BlockSpec((1, nres, halo, D_A),
                       at(lambda j: jnp.minimum((j + 1) * halo_per_tile, last_halo)))
    return pl.pallas_call(
        functools.partial(_attn_kernel, nres=nres, tr=tr, sub=nres * rows),
        grid=grid,
        in_specs=[cur, prev, cur, nxt, prev, cur, nxt],
        out_specs=[cur, pl.BlockSpec((1, nres, tr, LANES), at(lambda j: j))],
        out_shape=[
            jax.ShapeDtypeStruct((b, n_res, rows, D_A), BF16),
            jax.ShapeDtypeStruct((b, n_res, rows, LANES), F32),
        ],
        scratch_shapes=[pltpu.VMEM((nres, tr + 2 * halo, D_A), BF16)] * 2
                       + [pltpu.VMEM((Q_BLOCK, 2 * Q_BLOCK), F32),
                          pltpu.VMEM((N_HEADS_A, Q_BLOCK, 2 * Q_BLOCK), F32),
                          pltpu.VMEM((N_HEADS_A, Q_BLOCK, 2 * Q_BLOCK), BF16),
                          pltpu.VMEM((N_HEADS_A, Q_BLOCK, LANES), F32)],
        name=name,
    )(q, k, k, k, v, v, v)


def _split_bf16(x):
    hi = x.astype(BF16)
    lo = (x - hi.astype(F32)).astype(BF16)
    return hi, lo


def _mix_kernel(x_ref, o1_ref, o2_ref, o3_ref, l1_ref, l2_ref, l3_ref, u_ref, vb_ref, qc_ref,
                mk_ref, mv_ref, ws_ref, bs_ref, expand_ref, go_ref, wo_ref, gf_ref, wr_ref,
                x1_ref, xn_ref, aff_ref, afft_ref, obuf1, obuf2, obuf3, lbuf1, lbuf2, lbuf3):
    tm = x_ref.shape[0]

    def token_order(ref, buf):
        dil, width = ref.shape[1], ref.shape[3]
        for r in range(dil):
            piece = ref[0, r].astype(F32)
            for c in range(width // LANES):
                buf[c, pl.ds(r, tm // dil, stride=dil), :] = piece[:, c * LANES:(c + 1) * LANES]
        return jnp.concatenate([buf[c] for c in range(width // LANES)], axis=1)

    l1, l2, l3 = token_order(l1_ref, lbuf1), token_order(l2_ref, lbuf2), token_order(l3_ref, lbuf3)
    mx = jnp.maximum(jnp.maximum(l1, l2), l3)
    e1, e2, e3 = jnp.exp2(l1 - mx), jnp.exp2(l2 - mx), jnp.exp2(l3 - mx)
    tot = e1 + e2 + e3
    is_hi_lane = (lax.broadcasted_iota(jnp.int32, (tm, LANES), 1) & N_HEADS_A) == 0
    o_a = jnp.zeros((tm, D_A), F32)
    for e, o_ref, buf in ((e1, o1_ref, obuf1), (e2, o2_ref, obuf2), (e3, o3_ref, obuf3)):
        hi, lo = _split_bf16(e / tot)
        w = jnp.dot(jnp.where(is_hi_lane, hi, lo), expand_ref[...], preferred_element_type=F32)
        o_a = o_a + w * token_order(o_ref, buf)

    lane_b = lax.broadcasted_iota(jnp.int32, (CHUNK_B, D_B), 1)
    group_w = D_B // N_GROUPS_B
    gated = []
    for c in range(tm // CHUNK_B):
        vchunk = vb_ref[c * CHUNK_B:(c + 1) * CHUNK_B, :]
        acc = bs_ref[...]
        for g in range(N_GROUPS_B):
            y = jnp.dot(ws_ref[g], vchunk, preferred_element_type=F32)
            acc = acc + jnp.where(lane_b // group_w == g, y, 0.0)
        gated.append(acc)
    o_b = u_ref[...].astype(F32) * jnp.concatenate(gated, axis=0)

    lane_c = lax.broadcasted_iota(jnp.int32, (tm, LANES), 1)
    low_half = lane_c < HEAD_DIM
    oc_parts = []
    for hp in range(N_HEADS_C // 2):
        cols = slice(hp * LANES, (hp + 1) * LANES)
        q_pair = qc_ref[:, cols]
        k_pair = mk_ref[0, :, cols]
        v_pair = mv_ref[0, :, cols]
        halves = []
        for half in range(2):
            keep = low_half if half == 0 else jnp.logical_not(low_half)
            qm = jnp.where(keep, q_pair, jnp.zeros_like(q_pair))
            s = lax.dot_general(qm, k_pair, (((1,), (1,)), ((), ())), preferred_element_type=F32)
            m = jnp.max(s, axis=-1, keepdims=True)
            p = jnp.exp(s - m)
            p = p / jnp.sum(p, axis=-1, keepdims=True)
            halves.append(jnp.dot(p.astype(BF16), v_pair, preferred_element_type=F32))
        oc_parts.append(jnp.where(low_half, halves[0], halves[1]))
    o_c = jnp.concatenate(oc_parts, axis=1)

    go = go_ref[...]
    o = jnp.concatenate([_rms(o_a, go[:, :D_A]), _rms(o_b, go[:, D_A:D_A + D_B]),
                         _rms(o_c, go[:, D_A + D_B:])], axis=1).astype(BF16)
    x1 = x_ref[...] + jnp.dot(o, wo_ref[...], preferred_element_type=F32)
    x1_ref[...] = x1

    xn = _rms(x1, gf_ref[...]).astype(BF16)
    xn_ref[...] = xn
    logits = jnp.dot(xn, wr_ref[...], preferred_element_type=F32)
    is_expert = lane_c < N_EXPERTS
    logits = jnp.where(is_expert, logits, NEG_INF)
    m = jnp.max(logits, axis=-1, keepdims=True)
    ex = jnp.where(is_expert, jnp.exp(logits - m), 0.0)
    aff = ex / jnp.sum(ex, axis=-1, keepdims=True)
    aff_ref[...] = aff
    afft_ref[...] = jnp.transpose(aff)[:N_EXPERTS, :]


def _mix(x2, o_pats, lse_pats, u, vb, qc, mem_k, mem_v, w_sp, b_sp, expand, g_out, w_out,
         g_ffn, w_router, seq):
    t = x2.shape[0]
    tm = TOKEN_TILE
    tiles_per_seq = seq // tm
    row = lambda i: (i, 0)
    const2 = lambda i: (0, 0)
    const3 = lambda i: (0, 0, 0)
    batch = lambda i: (i // tiles_per_seq, 0, 0)
    dils = [o.shape[1] for o in o_pats]
    return pl.pallas_call(
        _mix_kernel,
        grid=(t // tm,),
        in_specs=[
            pl.BlockSpec((tm, D_MODEL), row),
            *[_residue_spec(d, tm, tiles_per_seq, D_A) for d in dils],
            *[_residue_spec(d, tm, tiles_per_seq, LANES) for d in dils],
            pl.BlockSpec((tm, D_B), row), pl.BlockSpec((tm, D_B), row), pl.BlockSpec((tm, D_C), row),
            pl.BlockSpec((1, N_MEM, D_C), batch), pl.BlockSpec((1, N_MEM, D_C), batch),
            pl.BlockSpec((N_GROUPS_B, CHUNK_B, CHUNK_B), const3),
            pl.BlockSpec((CHUNK_B, D_B), const2),
            pl.BlockSpec((LANES, D_A), const2),
            pl.BlockSpec((1, D_MODEL), const2),
            pl.BlockSpec((D_MODEL, D_MODEL), const2),
            pl.BlockSpec((1, D_MODEL), const2),
            pl.BlockSpec((D_MODEL, LANES), const2),
        ],
        out_specs=[
            pl.BlockSpec((tm, D_MODEL), row),
            pl.BlockSpec((tm, D_MODEL), row),
            pl.BlockSpec((tm, LANES), row),
            pl.BlockSpec((N_EXPERTS, tm), lambda i: (0, i)),
        ],
        out_shape=[
            jax.ShapeDtypeStruct((t, D_MODEL), F32),
            jax.ShapeDtypeStruct((t, D_MODEL), BF16),
            jax.ShapeDtypeStruct((t, LANES), F32),
            jax.ShapeDtypeStruct((N_EXPERTS, t), F32),
        ],
        scratch_shapes=[pltpu.VMEM((D_A // LANES, tm, LANES), F32)] * 3
                       + [pltpu.VMEM((1, tm, LANES), F32)] * 3,
        name="mix",
    )(x2, *o_pats, *lse_pats, u, vb, qc, mem_k, mem_v, w_sp, b_sp, expand, g_out, w_out,
      g_ffn, w_router)


def _threshold_kernel(afft_ref, thr_col_ref, quota_col_ref, thr_row_ref, quota_row_ref, *, cap):
    aff = afft_ref[...]

    def enough(v):
        return jnp.sum(jnp.where(aff >= v, 1.0, 0.0), axis=-1, keepdims=True) >= cap

    hi = jnp.full((N_EXPERTS, 1), 2.0, F32)
    for shift in BINADE_STEPS:
        cand = hi * (2.0 ** -shift)
        hi = jnp.where(enough(cand), hi, cand)
    lo = jnp.where(enough(hi * 0.5), hi * 0.5, 0.0)

    def bisect(_, bracket):
        lo, hi = bracket
        mid = (lo + hi) * 0.5
        ok = enough(mid)
        return jnp.where(ok, mid, lo), jnp.where(ok, hi, mid)

    lo, hi = lax.fori_loop(0, MANTISSA_STEPS, bisect, (lo, hi))
    thr_f = jnp.min(jnp.where(aff >= lo, aff, jnp.inf), axis=-1, keepdims=True)
    n_gt = jnp.sum(jnp.where(aff > thr_f, 1.0, 0.0), axis=-1, keepdims=True)
    quota = cap - n_gt
    thr_col = jnp.broadcast_to(thr_f, (N_EXPERTS, LANES))
    quota_col = jnp.broadcast_to(quota, (N_EXPERTS, LANES))
    thr_col_ref[...] = thr_col
    quota_col_ref[...] = quota_col
    diag = (lax.broadcasted_iota(jnp.int32, (N_EXPERTS, LANES), 0)
            == lax.broadcasted_iota(jnp.int32, (N_EXPERTS, LANES), 1))
    thr_row = jnp.sum(jnp.where(diag, thr_col, 0.0), axis=0, keepdims=True)
    quota_row = jnp.sum(jnp.where(diag, quota_col, 0.0), axis=0, keepdims=True)
    thr_row_ref[...] = jnp.broadcast_to(thr_row, (8, LANES))
    quota_row_ref[...] = jnp.broadcast_to(quota_row, (8, LANES))


def _threshold(afft, cap):
    t = afft.shape[1]
    full = lambda shape: pl.BlockSpec(shape, lambda: (0,) * len(shape))
    return pl.pallas_call(
        functools.partial(_threshold_kernel, cap=float(cap)),
        in_specs=[full((N_EXPERTS, t))],
        out_specs=[full((N_EXPERTS, LANES)), full((N_EXPERTS, LANES)),
                   full((8, LANES)), full((8, LANES))],
        out_shape=[jax.ShapeDtypeStruct((N_EXPERTS, LANES), F32)] * 2
                  + [jax.ShapeDtypeStruct((8, LANES), F32)] * 2,
        name="threshold",
    )(afft)


def _positions_kernel(afft_ref, aff_ref, thr_col_ref, quota_col_ref, thr_row_ref, quota_row_ref,
                      lp_ref, lpt_ref, gate_ref, start_ref, off_ref, span_ref, full_ref, lead_ref,
                      total_ref, eq_col, eq_row, slot_col, slot_row):
    i = pl.program_id(0)
    tm = MOE_TILE

    @pl.when(i == 0)
    def _():
        eq_col[...] = jnp.zeros_like(eq_col)
        eq_row[...] = jnp.zeros_like(eq_row)
        slot_col[...] = jnp.zeros_like(slot_col)
        slot_row[...] = jnp.zeros_like(slot_row)

    r = lax.broadcasted_iota(jnp.int32, (tm, tm), 0)
    c = lax.broadcasted_iota(jnp.int32, (tm, tm), 1)
    before = jnp.where(r < c, 1.0, 0.0).astype(BF16)
    after = jnp.where(c < r, 1.0, 0.0).astype(BF16)
    thr, quota = thr_col_ref[:, 0:1], quota_col_ref[:, 0:1]
    thr_r, quota_r = thr_row_ref[0:1, :], quota_row_ref[0:1, :]
    is_expert = lax.broadcasted_iota(jnp.int32, (tm, LANES), 1) < N_EXPERTS
    earlier_expert = (lax.broadcasted_iota(jnp.int32, (N_EXPERTS, LANES), 1)
                      < lax.broadcasted_iota(jnp.int32, (N_EXPERTS, LANES), 0))
    tile_floor = lambda n: jnp.floor(n / BF16_ROWS) * BF16_ROWS
    tile_ceil = lambda n: jnp.ceil(n / BF16_ROWS) * BF16_ROWS

    eq_seen_col, eq_seen_row = eq_col[:, 0:1], eq_row[0:1, :]
    first_slot, first_slot_row = slot_col[...], slot_row[0:1, :]
    for s in range(aff_ref.shape[0] // tm):
        a = afft_ref[:, s * tm:(s + 1) * tm]
        eq = a == thr
        eq_f = jnp.where(eq, 1.0, 0.0)
        eq_rank = jnp.dot(eq_f.astype(BF16), before, preferred_element_type=F32) + eq_seen_col
        sel = (a > thr) | (eq & (eq_rank < quota))
        sel_f = jnp.where(sel, 1.0, 0.0)
        lp = jnp.dot(sel_f.astype(BF16), before, preferred_element_type=F32)
        lp_ref[:, s * tm:(s + 1) * tm] = jnp.where(sel, lp, UNSELECTED)
        count = jnp.sum(sel_f, axis=-1, keepdims=True)
        start = tile_floor(first_slot)
        lead = first_slot - start
        start_ref[s] = start
        lead_ref[s] = lead
        span_ref[s] = tile_ceil(lead + count)
        full_ref[s] = tile_floor(lead + count)
        first_slot = first_slot + count
        eq_seen_col = eq_seen_col + jnp.sum(eq_f, axis=-1, keepdims=True)

        at = aff_ref[s * tm:(s + 1) * tm, :]
        eq_t = (at == thr_r) & is_expert
        eq_tf = jnp.where(eq_t, 1.0, 0.0)
        eq_rank_t = jnp.dot(after, eq_tf.astype(BF16), preferred_element_type=F32) + eq_seen_row
        sel_t = ((at > thr_r) & is_expert) | (eq_t & (eq_rank_t < quota_r))
        sel_tf = jnp.where(sel_t, 1.0, 0.0)
        lp_t = jnp.dot(after, sel_tf.astype(BF16), preferred_element_type=F32)
        lpt_ref[s * tm:(s + 1) * tm, :] = jnp.where(sel_t, lp_t, UNSELECTED)
        gate_ref[s * tm:(s + 1) * tm, :] = jnp.where(sel_t, at, 0.0)
        eq_seen_row = eq_seen_row + jnp.sum(eq_tf, axis=0, keepdims=True)
        count_row = jnp.sum(sel_tf, axis=0, keepdims=True)
        span_row = tile_ceil(first_slot_row - tile_floor(first_slot_row) + count_row)
        off = jnp.sum(jnp.where(earlier_expert, span_row, 0.0), axis=-1, keepdims=True)
        off_ref[s] = jnp.broadcast_to(off, (N_EXPERTS, LANES))
        first_slot_row = first_slot_row + count_row

    eq_col[...] = jnp.broadcast_to(eq_seen_col, eq_col.shape)
    eq_row[...] = jnp.broadcast_to(eq_seen_row, eq_row.shape)
    slot_col[...] = first_slot
    slot_row[...] = jnp.broadcast_to(first_slot_row, slot_row.shape)
    total_ref[...] = first_slot


def _positions(afft, aff, thr_col, quota_col, thr_row, quota_row):
    t = aff.shape[0]
    tm = MOE_TILE * POSITION_TILES
    n_tiles = t // MOE_TILE
    c2 = lambda i: (0, 0)
    per_tile = pl.BlockSpec((POSITION_TILES, N_EXPERTS, LANES), lambda i: (i, 0, 0))
    per_tile_shape = jax.ShapeDtypeStruct((n_tiles, N_EXPERTS, LANES), F32)
    return pl.pallas_call(
        _positions_kernel,
        grid=(t // tm,),
        in_specs=[
            pl.BlockSpec((N_EXPERTS, tm), lambda i: (0, i)),
            pl.BlockSpec((tm, LANES), lambda i: (i, 0)),
            pl.BlockSpec((N_EXPERTS, LANES), c2), pl.BlockSpec((N_EXPERTS, LANES), c2),
            pl.BlockSpec((8, LANES), c2), pl.BlockSpec((8, LANES), c2),
        ],
        out_specs=[
            pl.BlockSpec((N_EXPERTS, tm), lambda i: (0, i)),
            pl.BlockSpec((tm, LANES), lambda i: (i, 0)),
            pl.BlockSpec((tm, LANES), lambda i: (i, 0)),
            per_tile, per_tile, per_tile, per_tile, per_tile,
            pl.BlockSpec((N_EXPERTS, LANES), c2),
        ],
        out_shape=[
            jax.ShapeDtypeStruct((N_EXPERTS, t), F32),
            jax.ShapeDtypeStruct((t, LANES), F32),
            jax.ShapeDtypeStruct((t, LANES), F32),
            per_tile_shape,
            per_tile_shape,
            per_tile_shape,
            per_tile_shape,
            per_tile_shape,
            jax.ShapeDtypeStruct((N_EXPERTS, LANES), F32),
        ],
        scratch_shapes=[pltpu.VMEM((N_EXPERTS, LANES), F32), pltpu.VMEM((8, LANES), F32),
                        pltpu.VMEM((N_EXPERTS, LANES), F32), pltpu.VMEM((8, LANES), F32)],
        compiler_params=pltpu.CompilerParams(dimension_semantics=("arbitrary",)),
        name="positions",
    )(afft, aff, thr_col, quota_col, thr_row, quota_row)


class _TileTables:
    def __init__(self, start_ref, off_ref, span_ref, full_ref, lead_ref):
        self.start_ref, self.off_ref, self.span_ref = start_ref, off_ref, span_ref
        self.full_ref, self.lead_ref = full_ref, lead_ref

    def slot(self, tile, e, k):
        return pl.multiple_of(self.start_ref[tile * N_EXPERTS + e] + k * BF16_ROWS, BF16_ROWS)

    def stack_rows(self, tile):
        last = tile * N_EXPERTS + N_EXPERTS - 1
        return self.off_ref[last] + self.span_ref[last]

    def fits(self, tile):
        return self.stack_rows(tile) <= STACK_ROWS

    def regular(self, tile):
        longest = self.span_ref[tile * N_EXPERTS]
        for e in range(1, N_EXPERTS):
            longest = jnp.maximum(longest, self.span_ref[tile * N_EXPERTS + e])
        return self.fits(tile) & (longest <= RUN_ROWS)

    def for_each_piece(self, tile, rows_ref, fn):
        for e in range(N_EXPERTS):
            n_groups = rows_ref[tile * N_EXPERTS + e] // BF16_ROWS
            row = self.off_ref[tile * N_EXPERTS + e]
            slot = self.start_ref[tile * N_EXPERTS + e]
            for groups in RUN_PIECES:
                take = (n_groups & groups) != 0
                n_rows = groups * BF16_ROWS

                @pl.when(take)
                def _(e=e, row=row, slot=slot, n_rows=n_rows):
                    fn(e, pl.multiple_of(row, BF16_ROWS), pl.multiple_of(slot, BF16_ROWS), n_rows)

                step = jnp.where(take, n_rows, 0)
                row, slot = row + step, slot + step

    def for_each_group(self, tile, base, fn):
        for e in range(N_EXPERTS):
            off = self.off_ref[tile * N_EXPERTS + e]
            n_groups = self.span_ref[tile * N_EXPERTS + e] // BF16_ROWS

            def body(k, carry, e=e, off=off):
                s = off + k * BF16_ROWS - base

                @pl.when((s >= 0) & (s < STACK_ROWS))
                def _():
                    fn(e, k, pl.multiple_of(s, BF16_ROWS))
                return carry

            lax.fori_loop(0, n_groups, body, 0)


def _dispatch_kernel(start_ref, off_ref, span_ref, full_ref, lead_ref, xn_ref, lp_ref, xe_ref,
                     onehot, stack, tails, zeros, sem, sem_fill):
    i = pl.program_id(0)
    n = pl.num_programs(0)
    slot = i % 2
    tm = xn_ref.shape[0]
    tables = _TileTables(start_ref, off_ref, span_ref, full_ref, lead_ref)

    def send(buf, e, row, dst, n_rows):
        return pltpu.make_async_copy(stack.at[buf, pl.ds(row, n_rows), :],
                                     xe_ref.at[e, pl.ds(dst, n_rows), :], sem.at[buf])

    def compact():
        stack[slot, 0:STACK_ROWS] = jnp.dot(onehot[0:STACK_ROWS, :], xn_ref[...],
                                            preferred_element_type=F32).astype(BF16)

    def lead_of(e):
        return lead_ref[i * N_EXPERTS + e].astype(F32)

    def has_tail(e):
        return span_ref[i * N_EXPERTS + e] > full_ref[i * N_EXPERTS + e]

    def join_head(e, row):
        stack[slot, pl.ds(row, BF16_ROWS), :] = stack[slot, pl.ds(row, BF16_ROWS), :] + tails[e]

    def keep_tail(e, row):
        tails[e] = stack[slot, pl.ds(row, BF16_ROWS), :]

    @pl.when(i == 0)
    def _():
        onehot[...] = jnp.zeros_like(onehot)
        stack[...] = jnp.zeros_like(stack)
        tails[...] = jnp.zeros_like(tails)

    @pl.when((i >= 2) & tables.regular(jnp.maximum(i - 2, 0)))
    def _():
        tables.for_each_piece(i - 2, full_ref, lambda e, row, dst, m: send(slot, e, row, dst, m).wait())

    @pl.when(tables.regular(i))
    def _():
        run_iota = lax.broadcasted_iota(jnp.int32, (RUN_ROWS, tm), 0).astype(F32)
        for e in range(N_EXPERTS):
            row = pl.multiple_of(off_ref[i * N_EXPERTS + e], BF16_ROWS)
            hit = lp_ref[e:e + 1, :] == run_iota - lead_of(e)
            onehot[pl.ds(row, RUN_ROWS), :] = jnp.where(hit, 1.0, 0.0).astype(BF16)
        compact()
        for e in range(N_EXPERTS):
            row = pl.multiple_of(off_ref[i * N_EXPERTS + e], BF16_ROWS)
            join_head(e, row)
            tail_row = pl.multiple_of(row + full_ref[i * N_EXPERTS + e], BF16_ROWS)
            tails[e] = jnp.where(has_tail(e), stack[slot, pl.ds(tail_row, BF16_ROWS), :],
                                 jnp.zeros((BF16_ROWS, D_MODEL), BF16))
        tables.for_each_piece(i, full_ref, lambda e, row, dst, m: send(slot, e, row, dst, m).start())

    @pl.when(jnp.logical_not(tables.regular(i)))
    def _():
        group_iota = lax.broadcasted_iota(jnp.int32, (BF16_ROWS, tm), 0).astype(F32)
        n_rounds = (tables.stack_rows(i) + STACK_ROWS - 1) // STACK_ROWS

        def one_round(r, carry):
            base = r * STACK_ROWS
            onehot[...] = jnp.zeros_like(onehot)

            def mark(e, k, s):
                hit = lp_ref[e:e + 1, :] == group_iota + (k * BF16_ROWS).astype(F32) - lead_of(e)
                onehot[pl.ds(s, BF16_ROWS), :] = jnp.where(hit, 1.0, 0.0).astype(BF16)

            def settle(e, k, s):
                @pl.when(k == 0)
                def _():
                    join_head(e, s)

                @pl.when(k * BF16_ROWS == full_ref[i * N_EXPERTS + e])
                def _():
                    keep_tail(e, s)

            def completed(e, k):
                return k * BF16_ROWS < full_ref[i * N_EXPERTS + e]

            def group_copy(e, k, s):
                return send(slot, e, s, tables.slot(i, e, k), BF16_ROWS)

            def start_completed(e, k, s):
                @pl.when(completed(e, k))
                def _():
                    group_copy(e, k, s).start()

            def wait_completed(e, k, s):
                @pl.when(completed(e, k))
                def _():
                    group_copy(e, k, s).wait()

            tables.for_each_group(i, base, mark)
            compact()
            tables.for_each_group(i, base, settle)
            tables.for_each_group(i, base, start_completed)
            tables.for_each_group(i, base, wait_completed)
            return carry

        lax.fori_loop(0, n_rounds, one_round, 0)
        for e in range(N_EXPERTS):
            @pl.when(jnp.logical_not(has_tail(e)))
            def _():
                tails[e] = jnp.zeros((BF16_ROWS, D_MODEL), BF16)

    @pl.when(i == n - 1)
    def _():
        @pl.when((i >= 1) & tables.regular(jnp.maximum(i - 1, 0)))
        def _():
            tables.for_each_piece(i - 1, full_ref,
                                  lambda e, row, dst, m: send(1 - slot, e, row, dst, m).wait())

        @pl.when(tables.regular(i))
        def _():
            tables.for_each_piece(i, full_ref, lambda e, row, dst, m: send(slot, e, row, dst, m).wait())

        zeros[...] = jnp.zeros_like(zeros)
        cap_rows = xe_ref.shape[1]
        big = zeros.shape[0]

        def fill(e, row0, rows):
            return pltpu.make_async_copy(zeros.at[pl.ds(0, rows), :],
                                         xe_ref.at[e, pl.ds(row0, rows), :], sem_fill.at[0])

        for e in range(N_EXPERTS):
            used = start_ref[i * N_EXPERTS + e] + span_ref[i * N_EXPERTS + e]
            n_small = ((-used) & (big - 1)) // BF16_ROWS
            base = used + n_small * BF16_ROWS
            n_big = (cap_rows - base) // big

            def small_row(k):
                return pl.multiple_of(used + k * BF16_ROWS, BF16_ROWS)

            def big_row(k):
                return pl.multiple_of(base + k * big, big)

            lax.fori_loop(0, n_small, lambda k, c: (fill(e, small_row(k), BF16_ROWS).start(), c)[1], 0)
            lax.fori_loop(0, n_big, lambda k, c: (fill(e, big_row(k), big).start(), c)[1], 0)
            lax.fori_loop(0, n_small, lambda k, c: (fill(e, small_row(k), BF16_ROWS).wait(), c)[1], 0)
            lax.fori_loop(0, n_big, lambda k, c: (fill(e, big_row(k), big).wait(), c)[1], 0)


def _dispatch(tables, xn, lp, cap_rows):
    t = xn.shape[0]
    tm = MOE_TILE
    return pl.pallas_call(
        _dispatch_kernel,
        grid_spec=pltpu.PrefetchScalarGridSpec(
            num_scalar_prefetch=len(tables),
            grid=(t // tm,),
            in_specs=[
                pl.BlockSpec((tm, D_MODEL), lambda i, *_: (i, 0)),
                pl.BlockSpec((N_EXPERTS, tm), lambda i, *_: (0, i)),
            ],
            out_specs=pl.BlockSpec(memory_space=pl.ANY),
            scratch_shapes=[pltpu.VMEM((STACK_ROWS + RUN_ROWS, tm), BF16),
                            pltpu.VMEM((2, STACK_ROWS + BF16_ROWS, D_MODEL), BF16),
                            pltpu.VMEM((N_EXPERTS, BF16_ROWS, D_MODEL), BF16),
                            pltpu.VMEM((FILL_ROWS, D_MODEL), BF16),
                            pltpu.SemaphoreType.DMA((2,)),
                            pltpu.SemaphoreType.DMA((1,))],
        ),
        out_shape=jax.ShapeDtypeStruct((N_EXPERTS, cap_rows, D_MODEL), BF16),
        compiler_params=pltpu.CompilerParams(dimension_semantics=("arbitrary",)),
        name="dispatch",
    )(*tables, xn, lp)


def _ffn_kernel(used_ref, xe_ref, wg_ref, wu_ref, wd_ref, ye_ref, wg, wu, wd):
    e, j = pl.program_id(0), pl.program_id(1)
    bs = xe_ref.shape[1]
    n_valid = used_ref[e] - j * bs

    @pl.when(j == 0)
    def _():
        wg[...] = wg_ref[0].astype(BF16)
        wu[...] = wu_ref[0].astype(BF16)
        wd[...] = wd_ref[0].astype(BF16)

    @pl.when(n_valid > 0)
    def _():
        row = lax.broadcasted_iota(jnp.int32, (bs, D_MODEL), 0)
        x = jnp.where(row < n_valid, xe_ref[0], jnp.zeros((bs, D_MODEL), BF16))
        gate = jnp.dot(x, wg[...], preferred_element_type=F32)
        up = jnp.dot(x, wu[...], preferred_element_type=F32)
        hdn = (jax.nn.silu(gate) * up).astype(BF16)
        ye_ref[0] = jnp.dot(hdn, wd[...], preferred_element_type=F32).astype(BF16)

    @pl.when(n_valid <= 0)
    def _():
        ye_ref[0] = jnp.zeros((bs, D_MODEL), BF16)


def _ffn(used, xe, w_gate, w_up, w_down):
    cap_rows = xe.shape[1]
    bs = FFN_BLOCK

    def x_map(e, j, used_ref):
        last = jnp.maximum((used_ref[e] + bs - 1) // bs - 1, 0)
        return (e, jnp.minimum(j, last), 0)

    w_map = lambda e, j, used_ref: (e, 0, 0)
    return pl.pallas_call(
        _ffn_kernel,
        grid_spec=pltpu.PrefetchScalarGridSpec(
            num_scalar_prefetch=1,
            grid=(N_EXPERTS, cap_rows // bs),
            in_specs=[
                pl.BlockSpec((1, bs, D_MODEL), x_map),
                pl.BlockSpec((1, D_MODEL, D_MODEL), w_map),
                pl.BlockSpec((1, D_MODEL, D_MODEL), w_map),
                pl.BlockSpec((1, D_MODEL, D_MODEL), w_map),
            ],
            out_specs=pl.BlockSpec((1, bs, D_MODEL), lambda e, j, used_ref: (e, j, 0)),
            scratch_shapes=[pltpu.VMEM((D_MODEL, D_MODEL), BF16)] * 3,
        ),
        out_shape=jax.ShapeDtypeStruct((N_EXPERTS, cap_rows, D_MODEL), BF16),
        compiler_params=pltpu.CompilerParams(dimension_semantics=("arbitrary", "arbitrary"),
                                             vmem_limit_bytes=FFN_VMEM_BYTES),
        name="expert_ffn",
    )(used, xe, w_gate, w_up, w_down)


def _gate_matrix(lpt_ref, gate_ref, offv_ref, spanv_ref, leadv_ref, base):
    off = offv_ref[0][:, 0:1]
    end = off + spanv_ref[0][:, 0:1]
    first = off + leadv_ref[0][:, 0:1]
    row = lax.broadcasted_iota(jnp.int32, (N_EXPERTS, STACK_ROWS), 1).astype(F32) + base
    owner = (row >= off) & (row < end)
    owner_pad = jnp.concatenate(
        [jnp.where(owner, 1.0, 0.0), jnp.zeros((LANES - N_EXPERTS, STACK_ROWS), F32)], axis=0).astype(BF16)
    slot_of_row = row[0:1, :] - jnp.sum(jnp.where(owner, first, 0.0), axis=0, keepdims=True)
    slot_of_token = jnp.dot(lpt_ref[...].astype(BF16), owner_pad, preferred_element_type=F32)
    gate_of_token = jnp.dot(gate_ref[...].astype(BF16), owner_pad, preferred_element_type=F32)
    return jnp.where(slot_of_token == slot_of_row, gate_of_token, 0.0).astype(BF16)


def _combine_kernel(start_ref, off_ref, span_ref, full_ref, lead_ref, x1_ref, lpt_ref, gate_ref,
                    offv_ref, spanv_ref, leadv_ref, gfin_ref, ye_ref, y_ref,
                    stack, extra, sem, sem_extra):
    i = pl.program_id(0)
    n = pl.num_programs(0)
    slot = i % 2
    tables = _TileTables(start_ref, off_ref, span_ref, full_ref, lead_ref)

    def fetch(buf, e, row, src, n_rows):
        return pltpu.make_async_copy(ye_ref.at[e, pl.ds(src, n_rows), :],
                                     stack.at[buf, pl.ds(row, n_rows), :], sem.at[buf])

    @pl.when(i == 0)
    def _():
        stack[...] = jnp.zeros_like(stack)
        extra[...] = jnp.zeros_like(extra)

        @pl.when(tables.regular(0))
        def _():
            tables.for_each_piece(0, span_ref, lambda e, row, src, m: fetch(0, e, row, src, m).start())

    nxt = jnp.minimum(i + 1, n - 1)

    @pl.when((i + 1 < n) & tables.regular(nxt))
    def _():
        tables.for_each_piece(nxt, span_ref,
                              lambda e, row, src, m: fetch(1 - slot, e, row, src, m).start())

    regular = tables.regular(i)

    @pl.when(regular)
    def _():
        tables.for_each_piece(i, span_ref, lambda e, row, src, m: fetch(slot, e, row, src, m).wait())

    moe = jnp.dot(_gate_matrix(lpt_ref, gate_ref, offv_ref, spanv_ref, leadv_ref, 0.0), stack[slot],
                  preferred_element_type=F32)
    moe = jnp.where(regular, moe, 0.0)

    n_rounds = jnp.where(regular, 0, (tables.stack_rows(i) + STACK_ROWS - 1) // STACK_ROWS)

    def one_round(r, acc):
        base = r * STACK_ROWS

        def fetch_group(e, k, s):
            return pltpu.make_async_copy(ye_ref.at[e, pl.ds(tables.slot(i, e, k), BF16_ROWS), :],
                                         extra.at[pl.ds(s, BF16_ROWS), :], sem_extra.at[0])

        tables.for_each_group(i, base, lambda e, k, s: fetch_group(e, k, s).start())
        tables.for_each_group(i, base, lambda e, k, s: fetch_group(e, k, s).wait())
        g = _gate_matrix(lpt_ref, gate_ref, offv_ref, spanv_ref, leadv_ref, base.astype(F32))
        return acc + jnp.dot(g, extra[...], preferred_element_type=F32)

    moe = lax.fori_loop(0, n_rounds, one_round, moe)
    y_ref[...] = _rms(x1_ref[...] + moe, gfin_ref[...])


def _combine(tables, x1, lpt, gate, off_vec, span_vec, lead_vec, g_final, ye):
    t = x1.shape[0]
    tm = MOE_TILE
    row = lambda i, *_: (i, 0)
    per_tile = pl.BlockSpec((1, N_EXPERTS, LANES), lambda i, *_: (i, 0, 0))
    return pl.pallas_call(
        _combine_kernel,
        grid_spec=pltpu.PrefetchScalarGridSpec(
            num_scalar_prefetch=len(tables),
            grid=(t // tm,),
            in_specs=[
                pl.BlockSpec((tm, D_MODEL), row),
                pl.BlockSpec((tm, LANES), row),
                pl.BlockSpec((tm, LANES), row),
                per_tile, per_tile, per_tile,
                pl.BlockSpec((1, D_MODEL), lambda i, *_: (0, 0)),
                pl.BlockSpec(memory_space=pl.ANY),
            ],
            out_specs=pl.BlockSpec((tm, D_MODEL), row),
            scratch_shapes=[pltpu.VMEM((2, STACK_ROWS, D_MODEL), BF16),
                            pltpu.VMEM((STACK_ROWS, D_MODEL), BF16),
                            pltpu.SemaphoreType.DMA((2,)),
                            pltpu.SemaphoreType.DMA((1,))],
        ),
        out_shape=jax.ShapeDtypeStruct((t, D_MODEL), F32),
        compiler_params=pltpu.CompilerParams(dimension_semantics=("arbitrary",)),
        name="combine",
    )(*tables, x1, lpt, gate, off_vec, span_vec, lead_vec, g_final, ye)


def _rope_tables(seq):
    half = HEAD_DIM // 2
    inv_freq = jnp.power(ROPE_THETA, -jnp.arange(half, dtype=F32) * 2.0 / HEAD_DIM)
    ang = jnp.arange(seq, dtype=F32)[:, None] * inv_freq[None, :]
    cos, sin = jnp.cos(ang), jnp.sin(ang)
    reps = LANES // HEAD_DIM
    return (jnp.tile(jnp.concatenate([cos, cos], axis=1), (1, reps)),
            jnp.tile(jnp.concatenate([-sin, sin], axis=1), (1, reps)))


def _prepare_weights(g_mix, w_in, ln_b_g, ln_b_b, w_spatial, b_spatial, g_mem, w_mem_kv,
                     g_out, w_out, g_ffn, w_router, w_gate, w_up, w_down, g_final):
    scale = jnp.ones((w_in.shape[-1],), F32)
    scale = scale.at[:D_A].set(HEAD_DIM ** -0.5 * LOG2_E)
    scale = scale.at[3 * D_A + 2 * D_B:].set((D_C // N_HEADS_C) ** -0.5)
    head_of_lane = jnp.arange(D_A) // HEAD_DIM
    return dict(
        g_mix=g_mix[0][None], w_in=(w_in[0] * scale).astype(BF16),
        ln_g=ln_b_g[0][None], ln_b=ln_b_b[0][None],
        w_sp=w_spatial[0].astype(BF16),
        b_sp=jnp.repeat(b_spatial[0].T, D_B // N_GROUPS_B, axis=1),
        g_mem=g_mem[0][None], w_kv=w_mem_kv[0].astype(BF16),
        expand=sum((jnp.arange(LANES)[:, None] == (_lse_lane(h) + dup))
                   & (head_of_lane[None, :] == h)
                   for h in range(N_HEADS_A) for dup in (0, N_HEADS_A)).astype(BF16),
        g_out=g_out[0][None], w_out=w_out[0].astype(BF16),
        g_ffn=g_ffn[0][None],
        w_router=jnp.pad(w_router[0], ((0, 0), (0, LANES - N_EXPERTS))).astype(BF16),
        w_gate=w_gate[0], w_up=w_up[0], w_down=w_down[0],
        g_final=g_final[None],
    )


def _encoder(x, mem, w):
    b, seq, _ = x.shape
    t = b * seq
    x2 = x.reshape(t, D_MODEL)
    cos_t, sin_t = _rope_tables(seq)
    mem_k, mem_v = _mem_kv(mem, w["g_mem"], w["w_kv"])
    q4, k4, v4, q16, k16, v16, u, vb, qc = _in_proj(x2, w["g_mix"], w["w_in"], cos_t, sin_t,
                                                    w["ln_g"], w["ln_b"], b, seq)
    pats = [_banded_attention(q4, k4, v4, DILATIONS[1], "attn_d1"),
            _banded_attention(q4, k4, v4, 1, "attn_d4"),
            _banded_attention(q16, k16, v16, 1, "attn_d16")]
    x1, xn, aff, afft = _mix(x2, [p[0] for p in pats], [p[1] for p in pats], u, vb, qc,
                             mem_k, mem_v, w["w_sp"], w["b_sp"], w["expand"], w["g_out"],
                             w["w_out"], w["g_ffn"], w["w_router"], seq)

    cap = EC_CAPACITY_FACTOR * t // N_EXPERTS
    thr_col, quota_col, thr_row, quota_row = _threshold(afft, cap)
    lp, lpt, gate, start, off, span, full, lead, total = _positions(
        afft, aff, thr_col, quota_col, thr_row, quota_row)
    tables = [a[:, :, 0].astype(jnp.int32).reshape(-1) for a in (start, off, span, full, lead)]
    used = total[:, 0].astype(jnp.int32)
    assert cap % BF16_ROWS == 0
    cap_rows = -(-cap // FFN_BLOCK) * FFN_BLOCK
    xe = _dispatch(tables, xn, lp, cap_rows)
    ye = _ffn(used, xe, w["w_gate"], w["w_up"], w["w_down"])
    y = _combine(tables, x1, lpt, gate, off, span, lead, w["g_final"], ye)
    return y.reshape(b, seq, D_MODEL)


def kernel(x_prompt, x_sample, mem_prompt, mem_sample, g_mix, w_in, ln_b_g, ln_b_b, w_spatial,
           b_spatial, g_mem, w_mem_kv, g_out, w_out, g_ffn, w_router, w_gate, w_up, w_down, g_final):
    w = _prepare_weights(g_mix, w_in, ln_b_g, ln_b_b, w_spatial, b_spatial, g_mem, w_mem_kv,
                         g_out, w_out, g_ffn, w_router, w_gate, w_up, w_down, g_final)
    return (_encoder(x_prompt, mem_prompt, w), _encoder(x_sample, mem_sample, w))
```

```python
import functools

import jax
import jax.numpy as jnp
from jax import lax
from jax.experimental import pallas as pl
from jax.experimental.pallas import tpu as pltpu

F32 = jnp.float32
BF16 = jnp.bfloat16

D_MODEL = 1024
N_HEADS_A = 8
HEAD_DIM = 64
D_A = 512
D_B = 256
N_GROUPS_B = 4
CHUNK_B = 128
D_C = 256
N_HEADS_C = 4
N_MEM = 256
N_EXPERTS = 16
EC_CAPACITY_FACTOR = 2
DILATIONS = (1, 4, 16)
RADIUS = 64
Q_BLOCK = 128
ROPE_THETA = 10000.0
EPS = 1e-6
NEG_INF = -1e30
LOG2_E = 1.4426950408889634

LANES = 128
BF16_ROWS = 16
TOKEN_TILE = 512
ATTN_TILE = 512
MOE_TILE = 256
POSITION_TILES = 4
STACK_ROWS = 896
FILL_ROWS = 64
RUN_ROWS = 128
RUN_PIECES = (8, 4, 2, 1)
FFN_BLOCK = 512
FFN_VMEM_BYTES = 52 * 1024 * 1024
UNSELECTED = -4096.0
BINADE_STEPS = (64, 32, 16, 8, 4, 2, 1)
MANTISSA_STEPS = 52


def _rms(x, g):
    return x * lax.rsqrt(jnp.mean(x * x, axis=-1, keepdims=True) + EPS) * g


def _mem_kv_kernel(mem_ref, g_ref, w_ref, k_ref, v_ref):
    h = _rms(mem_ref[0], g_ref[...]).astype(BF16)
    kv = jnp.dot(h, w_ref[...], preferred_element_type=F32)
    k_ref[0] = kv[:, :D_C].astype(BF16)
    v_ref[0] = kv[:, D_C:].astype(BF16)


def _mem_kv(mem, g_mem, w_kv):
    b = mem.shape[0]
    return pl.pallas_call(
        _mem_kv_kernel,
        grid=(b,),
        in_specs=[
            pl.BlockSpec((1, N_MEM, D_MODEL), lambda i: (i, 0, 0)),
            pl.BlockSpec((1, D_MODEL), lambda i: (0, 0)),
            pl.BlockSpec((D_MODEL, 2 * D_C), lambda i: (0, 0)),
        ],
        out_specs=[
            pl.BlockSpec((1, N_MEM, D_C), lambda i: (i, 0, 0)),
            pl.BlockSpec((1, N_MEM, D_C), lambda i: (i, 0, 0)),
        ],
        out_shape=[jax.ShapeDtypeStruct((b, N_MEM, D_C), BF16)] * 2,
        name="mem_kv",
    )(mem, g_mem, w_kv)


def _in_proj_kernel(x_ref, g_ref, w_ref, cos_ref, sin_ref, lng_ref, lnb_ref,
                    q4_ref, k4_ref, v4_ref, q16_ref, k16_ref, v16_ref, u_ref, vb_ref, qc_ref,
                    chunks, chunks4):
    tm = x_ref.shape[0]
    d4 = DILATIONS[1]
    n_chunks = D_A // LANES

    def emit_by_residue(z, out4_ref, out16_ref):
        for c in range(n_chunks):
            chunks[c] = z[:, c * LANES:(c + 1) * LANES]
        quarter = tm // d4
        for r in range(d4):
            rows = [chunks[c, pl.ds(r, quarter, stride=d4), :] for c in range(n_chunks)]
            out4_ref[0, r] = jnp.concatenate(rows, axis=1).astype(BF16)
            for c in range(n_chunks):
                chunks4[c, r * quarter:(r + 1) * quarter, :] = rows[c]
        for r in range(d4):
            for a in range(d4):
                rows = [chunks4[c, pl.ds(r * quarter + a, quarter // d4, stride=d4), :]
                        for c in range(n_chunks)]
                out16_ref[0, r + d4 * a] = jnp.concatenate(rows, axis=1).astype(BF16)

    h = _rms(x_ref[...], g_ref[...]).astype(BF16)
    cos = jnp.concatenate([cos_ref[...]] * (D_A // LANES), axis=1)
    sin = jnp.concatenate([sin_ref[...]] * (D_A // LANES), axis=1)
    lane = lax.broadcasted_iota(jnp.int32, (tm, D_A), 1)
    first_half = (lane & (HEAD_DIM - 1)) < (HEAD_DIM // 2)

    def rope(z):
        rot = jnp.where(first_half, pltpu.roll(z, D_A - HEAD_DIM // 2, 1),
                        pltpu.roll(z, HEAD_DIM // 2, 1))
        return z * cos + rot * sin

    def proj(lo, hi):
        return jnp.dot(h, w_ref[:, lo:hi], preferred_element_type=F32)

    emit_by_residue(rope(proj(0, D_A)), q4_ref, q16_ref)
    emit_by_residue(rope(proj(D_A, 2 * D_A)), k4_ref, k16_ref)
    emit_by_residue(proj(2 * D_A, 3 * D_A), v4_ref, v16_ref)
    u_ref[...] = jax.nn.gelu(proj(3 * D_A, 3 * D_A + D_B)).astype(BF16)
    vb = jax.nn.gelu(proj(3 * D_A + D_B, 3 * D_A + 2 * D_B))
    mu = jnp.mean(vb, axis=-1, keepdims=True)
    var = jnp.mean(jnp.square(vb - mu), axis=-1, keepdims=True)
    vb_ref[...] = ((vb - mu) * lax.rsqrt(var + EPS) * lng_ref[...] + lnb_ref[...]).astype(BF16)
    qc_ref[...] = proj(3 * D_A + 2 * D_B, 3 * D_A + 2 * D_B + D_C).astype(BF16)


def _residue_spec(dil, tm, tiles_per_seq, width):
    return pl.BlockSpec((1, dil, tm // dil, width),
                        lambda i: (i // tiles_per_seq, 0, i % tiles_per_seq, 0))


def _in_proj(x2, g_mix, w_in, cos_t, sin_t, ln_g, ln_b, b, seq):
    t = x2.shape[0]
    tm = TOKEN_TILE
    d_in = w_in.shape[1]
    tiles_per_seq = seq // tm
    row = lambda i: (i, 0)
    const = lambda i: (0, 0)
    pos = lambda i: (i % tiles_per_seq, 0)
    d4, d16 = DILATIONS[1], DILATIONS[2]
    res4 = _residue_spec(d4, tm, tiles_per_seq, D_A)
    res16 = _residue_spec(d16, tm, tiles_per_seq, D_A)
    shape4 = jax.ShapeDtypeStruct((b, d4, seq // d4, D_A), BF16)
    shape16 = jax.ShapeDtypeStruct((b, d16, seq // d16, D_A), BF16)
    return pl.pallas_call(
        _in_proj_kernel,
        grid=(t // tm,),
        in_specs=[
            pl.BlockSpec((tm, D_MODEL), row),
            pl.BlockSpec((1, D_MODEL), const),
            pl.BlockSpec((D_MODEL, d_in), const),
            pl.BlockSpec((tm, LANES), pos),
            pl.BlockSpec((tm, LANES), pos),
            pl.BlockSpec((1, D_B), const),
            pl.BlockSpec((1, D_B), const),
        ],
        out_specs=[res4] * 3 + [res16] * 3 + [pl.BlockSpec((tm, w), row) for w in (D_B, D_B, D_C)],
        out_shape=[shape4] * 3 + [shape16] * 3
                  + [jax.ShapeDtypeStruct((t, w), BF16) for w in (D_B, D_B, D_C)],
        scratch_shapes=[pltpu.VMEM((D_A // LANES, tm, LANES), F32)] * 2,
        name="in_proj",
    )(x2, g_mix, w_in, cos_t, sin_t, ln_g, ln_b)


def _lse_lane(head):
    return head + HEAD_DIM * (1 - head % 2)


def _attn_kernel(q_ref, kp_ref, kc_ref, kn_ref, vp_ref, vc_ref, vn_ref,
                 o_ref, lse_ref, kwin, vwin, bias, scores, probs, dens, *, nres, tr, sub):
    j = pl.program_id(2)
    halo = RADIUS // nres
    qb = Q_BLOCK // nres
    for win, (p_ref, c_ref, n_ref) in ((kwin, (kp_ref, kc_ref, kn_ref)), (vwin, (vp_ref, vc_ref, vn_ref))):
        win[:, 0:halo] = p_ref[0]
        win[:, halo:halo + tr] = c_ref[0]
        win[:, halo + tr:] = n_ref[0]

    t_idx = lax.broadcasted_iota(jnp.int32, (Q_BLOCK, 2 * Q_BLOCK), 0)
    s_idx = lax.broadcasted_iota(jnp.int32, (Q_BLOCK, 2 * Q_BLOCK), 1)
    q_res, q_row = t_idx >> (qb.bit_length() - 1), t_idx & (qb - 1)
    k_res, k_row = s_idx >> qb.bit_length(), s_idx & (2 * qb - 1)
    band = jnp.abs(nres * (q_row - k_row + halo) + q_res - k_res) <= RADIUS
    k_elem = nres * (k_row - halo) + k_res
    lane = lax.broadcasted_iota(jnp.int32, (Q_BLOCK, LANES), 1)
    low_half = lane < HEAD_DIM

    def block(i, carry):
        r0 = pl.multiple_of(i * qb, qb)
        kpos = k_elem + nres * (j * tr + r0)
        bias[...] = jnp.where(band & (kpos >= 0) & (kpos < sub), 0.0, NEG_INF)
        stack = lambda pieces: pieces[0] if nres == 1 else jnp.concatenate(pieces, axis=0)
        pair_cols = lambda hp: slice(hp * LANES, (hp + 1) * LANES)

        for hp in range(N_HEADS_A // 2):
            q_pair = stack([q_ref[0, r, pl.ds(r0, qb), pair_cols(hp)] for r in range(nres)])
            k_pair = stack([kwin[r, pl.ds(r0, 2 * qb), pair_cols(hp)] for r in range(nres)])
            for half in range(2):
                keep = low_half if half == 0 else jnp.logical_not(low_half)
                qm = jnp.where(keep, q_pair, jnp.zeros_like(q_pair))
                scores[2 * hp + half] = lax.dot_general(qm, k_pair, (((1,), (1,)), ((), ())),
                                                        preferred_element_type=F32)

        lse_blk = jnp.zeros((Q_BLOCK, LANES), F32)
        for head in range(N_HEADS_A):
            s = scores[head] + bias[...]
            m = jnp.max(s, axis=-1, keepdims=True)
            p = jnp.exp2(s - m)
            den = jnp.sum(p, axis=-1, keepdims=True)
            probs[head] = p.astype(BF16)
            dens[head] = jnp.broadcast_to(den, (Q_BLOCK, LANES))
            hit = (lane == _lse_lane(head)) | (lane == _lse_lane(head) + N_HEADS_A)
            lse_blk = jnp.where(hit, m + jnp.log(den) * LOG2_E, lse_blk)
        for r in range(nres):
            lse_ref[0, r, pl.ds(r0, qb), :] = lse_blk[r * qb:(r + 1) * qb]

        for hp in range(N_HEADS_A // 2):
            v_pair = stack([vwin[r, pl.ds(r0, 2 * qb), pair_cols(hp)] for r in range(nres)])
            halves = [jnp.dot(probs[2 * hp + half], v_pair, preferred_element_type=F32)
                      / dens[2 * hp + half] for half in range(2)]
            out = jnp.where(low_half, halves[0], halves[1]).astype(BF16)
            for r in range(nres):
                o_ref[0, r, pl.ds(r0, qb), pair_cols(hp)] = out[r * qb:(r + 1) * qb]
        return carry

    lax.fori_loop(0, tr // qb, block, 0)


def _banded_attention(q, k, v, nres, name):
    b, n_res, rows, _ = q.shape
    tr = min(ATTN_TILE // nres, rows)
    halo = RADIUS // nres
    halo_per_tile = tr // halo
    last_halo = rows // halo - 1
    if nres == 1:
        grid = (b, n_res, rows // tr)
        at = lambda row_block: (lambda bi, r, j: (bi, r, row_block(j), 0))
    else:
        grid = (b, 1, rows // tr)
        at = lambda row_block: (lambda bi, r, j: (bi, 0, row_block(j), 0))
    cur = pl.BlockSpec((1, nres, tr, D_A), at(lambda j: j))
    prev = pl.BlockSpec((1, nres, halo, D_A), at(lambda j: jnp.maximum(j * halo_per_tile - 1, 0)))
    nxt = pl.BlockSpec((1, nres, halo, D_A),
                       at(lambda j: jnp.minimum((j + 1) * halo_per_tile, last_halo)))
    return pl.pallas_call(
        functools.partial(_attn_kernel, nres=nres, tr=tr, sub=nres * rows),
        grid=grid,
        in_specs=[cur, prev, cur, nxt, prev, cur, nxt],
        out_specs=[cur, pl.BlockSpec((1, nres, tr, LANES), at(lambda j: j))],
        out_shape=[
            jax.ShapeDtypeStruct((b, n_res, rows, D_A), BF16),
            jax.ShapeDtypeStruct((b, n_res, rows, LANES), F32),
        ],
        scratch_shapes=[pltpu.VMEM((nres, tr + 2 * halo, D_A), BF16)] * 2
                       + [pltpu.VMEM((Q_BLOCK, 2 * Q_BLOCK), F32),
                          pltpu.VMEM((N_HEADS_A, Q_BLOCK, 2 * Q_BLOCK), F32),
                          pltpu.VMEM((N_HEADS_A, Q_BLOCK, 2 * Q_BLOCK), BF16),
                          pltpu.VMEM((N_HEADS_A, Q_BLOCK, LANES), F32)],
        name=name,
    )(q, k, k, k, v, v, v)


def _split_bf16(x):
    hi = x.astype(BF16)
    lo = (x - hi.astype(F32)).astype(BF16)
    return hi, lo


def _mix_kernel(x_ref, o1_ref, o2_ref, o3_ref, l1_ref, l2_ref, l3_ref, u_ref, vb_ref, qc_ref,
                mk_ref, mv_ref, ws_ref, bs_ref, expand_ref, go_ref, wo_ref, gf_ref, wr_ref,
                x1_ref, xn_ref, aff_ref, afft_ref, obuf1, obuf2, obuf3, lbuf1, lbuf2, lbuf3):
    tm = x_ref.shape[0]

    def token_order(ref, buf):
        dil, width = ref.shape[1], ref.shape[3]
        for r in range(dil):
            piece = ref[0, r].astype(F32)
            for c in range(width // LANES):
                buf[c, pl.ds(r, tm // dil, stride=dil), :] = piece[:, c * LANES:(c + 1) * LANES]
        return jnp.concatenate([buf[c] for c in range(width // LANES)], axis=1)

    l1, l2, l3 = token_order(l1_ref, lbuf1), token_order(l2_ref, lbuf2), token_order(l3_ref, lbuf3)
    mx = jnp.maximum(jnp.maximum(l1, l2), l3)
    e1, e2, e3 = jnp.exp2(l1 - mx), jnp.exp2(l2 - mx), jnp.exp2(l3 - mx)
    tot = e1 + e2 + e3
    is_hi_lane = (lax.broadcasted_iota(jnp.int32, (tm, LANES), 1) & N_HEADS_A) == 0
    o_a = jnp.zeros((tm, D_A), F32)
    for e, o_ref, buf in ((e1, o1_ref, obuf1), (e2, o2_ref, obuf2), (e3, o3_ref, obuf3)):
        hi, lo = _split_bf16(e / tot)
        w = jnp.dot(jnp.where(is_hi_lane, hi, lo), expand_ref[...], preferred_element_type=F32)
        o_a = o_a + w * token_order(o_ref, buf)

    lane_b = lax.broadcasted_iota(jnp.int32, (CHUNK_B, D_B), 1)
    group_w = D_B // N_GROUPS_B
    gated = []
    for c in range(tm // CHUNK_B):
        vchunk = vb_ref[c * CHUNK_B:(c + 1) * CHUNK_B, :]
        acc = bs_ref[...]
        for g in range(N_GROUPS_B):
            y = jnp.dot(ws_ref[g], vchunk, preferred_element_type=F32)
            acc = acc + jnp.where(lane_b // group_w == g, y, 0.0)
        gated.append(acc)
    o_b = u_ref[...].astype(F32) * jnp.concatenate(gated, axis=0)

    lane_c = lax.broadcasted_iota(jnp.int32, (tm, LANES), 1)
    low_half = lane_c < HEAD_DIM
    oc_parts = []
    for hp in range(N_HEADS_C // 2):
        cols = slice(hp * LANES, (hp + 1) * LANES)
        q_pair = qc_ref[:, cols]
        k_pair = mk_ref[0, :, cols]
        v_pair = mv_ref[0, :, cols]
        halves = []
        for half in range(2):
            keep = low_half if half == 0 else jnp.logical_not(low_half)
            qm = jnp.where(keep, q_pair, jnp.zeros_like(q_pair))
            s = lax.dot_general(qm, k_pair, (((1,), (1,)), ((), ())), preferred_element_type=F32)
            m = jnp.max(s, axis=-1, keepdims=True)
            p = jnp.exp(s - m)
            p = p / jnp.sum(p, axis=-1, keepdims=True)
            halves.append(jnp.dot(p.astype(BF16), v_pair, preferred_element_type=F32))
        oc_parts.append(jnp.where(low_half, halves[0], halves[1]))
    o_c = jnp.concatenate(oc_parts, axis=1)

    go = go_ref[...]
    o = jnp.concatenate([_rms(o_a, go[:, :D_A]), _rms(o_b, go[:, D_A:D_A + D_B]),
                         _rms(o_c, go[:, D_A + D_B:])], axis=1).astype(BF16)
    x1 = x_ref[...] + jnp.dot(o, wo_ref[...], preferred_element_type=F32)
    x1_ref[...] = x1

    xn = _rms(x1, gf_ref[...]).astype(BF16)
    xn_ref[...] = xn
    logits = jnp.dot(xn, wr_ref[...], preferred_element_type=F32)
    is_expert = lane_c < N_EXPERTS
    logits = jnp.where(is_expert, logits, NEG_INF)
    m = jnp.max(logits, axis=-1, keepdims=True)
    ex = jnp.where(is_expert, jnp.exp(logits - m), 0.0)
    aff = ex / jnp.sum(ex, axis=-1, keepdims=True)
    aff_ref[...] = aff
    afft_ref[...] = jnp.transpose(aff)[:N_EXPERTS, :]


def _mix(x2, o_pats, lse_pats, u, vb, qc, mem_k, mem_v, w_sp, b_sp, expand, g_out, w_out,
         g_ffn, w_router, seq):
    t = x2.shape[0]
    tm = TOKEN_TILE
    tiles_per_seq = seq // tm
    row = lambda i: (i, 0)
    const2 = lambda i: (0, 0)
    const3 = lambda i: (0, 0, 0)
    batch = lambda i: (i // tiles_per_seq, 0, 0)
    dils = [o.shape[1] for o in o_pats]
    return pl.pallas_call(
        _mix_kernel,
        grid=(t // tm,),
        in_specs=[
            pl.BlockSpec((tm, D_MODEL), row),
            *[_residue_spec(d, tm, tiles_per_seq, D_A) for d in dils],
            *[_residue_spec(d, tm, tiles_per_seq, LANES) for d in dils],
            pl.BlockSpec((tm, D_B), row), pl.BlockSpec((tm, D_B), row), pl.BlockSpec((tm, D_C), row),
            pl.BlockSpec((1, N_MEM, D_C), batch), pl.BlockSpec((1, N_MEM, D_C), batch),
            pl.BlockSpec((N_GROUPS_B, CHUNK_B, CHUNK_B), const3),
            pl.BlockSpec((CHUNK_B, D_B), const2),
            pl.BlockSpec((LANES, D_A), const2),
            pl.BlockSpec((1, D_MODEL), const2),
            pl.BlockSpec((D_MODEL, D_MODEL), const2),
            pl.BlockSpec((1, D_MODEL), const2),
            pl.BlockSpec((D_MODEL, LANES), const2),
        ],
        out_specs=[
            pl.BlockSpec((tm, D_MODEL), row),
            pl.BlockSpec((tm, D_MODEL), row),
            pl.BlockSpec((tm, LANES), row),
            pl.BlockSpec((N_EXPERTS, tm), lambda i: (0, i)),
        ],
        out_shape=[
            jax.ShapeDtypeStruct((t, D_MODEL), F32),
            jax.ShapeDtypeStruct((t, D_MODEL), BF16),
            jax.ShapeDtypeStruct((t, LANES), F32),
            jax.ShapeDtypeStruct((N_EXPERTS, t), F32),
        ],
        scratch_shapes=[pltpu.VMEM((D_A // LANES, tm, LANES), F32)] * 3
                       + [pltpu.VMEM((1, tm, LANES), F32)] * 3,
        name="mix",
    )(x2, *o_pats, *lse_pats, u, vb, qc, mem_k, mem_v, w_sp, b_sp, expand, g_out, w_out,
      g_ffn, w_router)


def _threshold_kernel(afft_ref, thr_col_ref, quota_col_ref, thr_row_ref, quota_row_ref, *, cap):
    aff = afft_ref[...]

    def enough(v):
        return jnp.sum(jnp.where(aff >= v, 1.0, 0.0), axis=-1, keepdims=True) >= cap

    hi = jnp.full((N_EXPERTS, 1), 2.0, F32)
    for shift in BINADE_STEPS:
        cand = hi * (2.0 ** -shift)
        hi = jnp.where(enough(cand), hi, cand)
    lo = jnp.where(enough(hi * 0.5), hi * 0.5, 0.0)

    def bisect(_, bracket):
        lo, hi = bracket
        mid = (lo + hi) * 0.5
        ok = enough(mid)
        return jnp.where(ok, mid, lo), jnp.where(ok, hi, mid)

    lo, hi = lax.fori_loop(0, MANTISSA_STEPS, bisect, (lo, hi))
    thr_f = jnp.min(jnp.where(aff >= lo, aff, jnp.inf), axis=-1, keepdims=True)
    n_gt = jnp.sum(jnp.where(aff > thr_f, 1.0, 0.0), axis=-1, keepdims=True)
    quota = cap - n_gt
    thr_col = jnp.broadcast_to(thr_f, (N_EXPERTS, LANES))
    quota_col = jnp.broadcast_to(quota, (N_EXPERTS, LANES))
    thr_col_ref[...] = thr_col
    quota_col_ref[...] = quota_col
    diag = (lax.broadcasted_iota(jnp.int32, (N_EXPERTS, LANES), 0)
            == lax.broadcasted_iota(jnp.int32, (N_EXPERTS, LANES), 1))
    thr_row = jnp.sum(jnp.where(diag, thr_col, 0.0), axis=0, keepdims=True)
    quota_row = jnp.sum(jnp.where(diag, quota_col, 0.0), axis=0, keepdims=True)
    thr_row_ref[...] = jnp.broadcast_to(thr_row, (8, LANES))
    quota_row_ref[...] = jnp.broadcast_to(quota_row, (8, LANES))


def _threshold(afft, cap):
    t = afft.shape[1]
    full = lambda shape: pl.BlockSpec(shape, lambda: (0,) * len(shape))
    return pl.pallas_call(
        functools.partial(_threshold_kernel, cap=float(cap)),
        in_specs=[full((N_EXPERTS, t))],
        out_specs=[full((N_EXPERTS, LANES)), full((N_EXPERTS, LANES)),
                   full((8, LANES)), full((8, LANES))],
        out_shape=[jax.ShapeDtypeStruct((N_EXPERTS, LANES), F32)] * 2
                  + [jax.ShapeDtypeStruct((8, LANES), F32)] * 2,
        name="threshold",
    )(afft)


def _positions_kernel(afft_ref, aff_ref, thr_col_ref, quota_col_ref, thr_row_ref, quota_row_ref,
                      lp_ref, lpt_ref, gate_ref, start_ref, off_ref, span_ref, full_ref, lead_ref,
                      total_ref, eq_col, eq_row, slot_col, slot_row):
    i = pl.program_id(0)
    tm = MOE_TILE

    @pl.when(i == 0)
    def _():
        eq_col[...] = jnp.zeros_like(eq_col)
        eq_row[...] = jnp.zeros_like(eq_row)
        slot_col[...] = jnp.zeros_like(slot_col)
        slot_row[...] = jnp.zeros_like(slot_row)

    r = lax.broadcasted_iota(jnp.int32, (tm, tm), 0)
    c = lax.broadcasted_iota(jnp.int32, (tm, tm), 1)
    before = jnp.where(r < c, 1.0, 0.0).astype(BF16)
    after = jnp.where(c < r, 1.0, 0.0).astype(BF16)
    thr, quota = thr_col_ref[:, 0:1], quota_col_ref[:, 0:1]
    thr_r, quota_r = thr_row_ref[0:1, :], quota_row_ref[0:1, :]
    is_expert = lax.broadcasted_iota(jnp.int32, (tm, LANES), 1) < N_EXPERTS
    earlier_expert = (lax.broadcasted_iota(jnp.int32, (N_EXPERTS, LANES), 1)
                      < lax.broadcasted_iota(jnp.int32, (N_EXPERTS, LANES), 0))
    tile_floor = lambda n: jnp.floor(n / BF16_ROWS) * BF16_ROWS
    tile_ceil = lambda n: jnp.ceil(n / BF16_ROWS) * BF16_ROWS

    eq_seen_col, eq_seen_row = eq_col[:, 0:1], eq_row[0:1, :]
    first_slot, first_slot_row = slot_col[...], slot_row[0:1, :]
    for s in range(aff_ref.shape[0] // tm):
        a = afft_ref[:, s * tm:(s + 1) * tm]
        eq = a == thr
        eq_f = jnp.where(eq, 1.0, 0.0)
        eq_rank = jnp.dot(eq_f.astype(BF16), before, preferred_element_type=F32) + eq_seen_col
        sel = (a > thr) | (eq & (eq_rank < quota))
        sel_f = jnp.where(sel, 1.0, 0.0)
        lp = jnp.dot(sel_f.astype(BF16), before, preferred_element_type=F32)
        lp_ref[:, s * tm:(s + 1) * tm] = jnp.where(sel, lp, UNSELECTED)
        count = jnp.sum(sel_f, axis=-1, keepdims=True)
        start = tile_floor(first_slot)
        lead = first_slot - start
        start_ref[s] = start
        lead_ref[s] = lead
        span_ref[s] = tile_ceil(lead + count)
        full_ref[s] = tile_floor(lead + count)
        first_slot = first_slot + count
        eq_seen_col = eq_seen_col + jnp.sum(eq_f, axis=-1, keepdims=True)

        at = aff_ref[s * tm:(s + 1) * tm, :]
        eq_t = (at == thr_r) & is_expert
        eq_tf = jnp.where(eq_t, 1.0, 0.0)
        eq_rank_t = jnp.dot(after, eq_tf.astype(BF16), preferred_element_type=F32) + eq_seen_row
        sel_t = ((at > thr_r) & is_expert) | (eq_t & (eq_rank_t < quota_r))
        sel_tf = jnp.where(sel_t, 1.0, 0.0)
        lp_t = jnp.dot(after, sel_tf.astype(BF16), preferred_element_type=F32)
        lpt_ref[s * tm:(s + 1) * tm, :] = jnp.where(sel_t, lp_t, UNSELECTED)
        gate_ref[s * tm:(s + 1) * tm, :] = jnp.where(sel_t, at, 0.0)
        eq_seen_row = eq_seen_row + jnp.sum(eq_tf, axis=0, keepdims=True)
        count_row = jnp.sum(sel_tf, axis=0, keepdims=True)
        span_row = tile_ceil(first_slot_row - tile_floor(first_slot_row) + count_row)
        off = jnp.sum(jnp.where(earlier_expert, span_row, 0.0), axis=-1, keepdims=True)
        off_ref[s] = jnp.broadcast_to(off, (N_EXPERTS, LANES))
        first_slot_row = first_slot_row + count_row

    eq_col[...] = jnp.broadcast_to(eq_seen_col, eq_col.shape)
    eq_row[...] = jnp.broadcast_to(eq_seen_row, eq_row.shape)
    slot_col[...] = first_slot
    slot_row[...] = jnp.broadcast_to(first_slot_row, slot_row.shape)
    total_ref[...] = first_slot


def _positions(afft, aff, thr_col, quota_col, thr_row, quota_row):
    t = aff.shape[0]
    tm = MOE_TILE * POSITION_TILES
    n_tiles = t // MOE_TILE
    c2 = lambda i: (0, 0)
    per_tile = pl.BlockSpec((POSITION_TILES, N_EXPERTS, LANES), lambda i: (i, 0, 0))
    per_tile_shape = jax.ShapeDtypeStruct((n_tiles, N_EXPERTS, LANES), F32)
    return pl.pallas_call(
        _positions_kernel,
        grid=(t // tm,),
        in_specs=[
            pl.BlockSpec((N_EXPERTS, tm), lambda i: (0, i)),
            pl.BlockSpec((tm, LANES), lambda i: (i, 0)),
            pl.BlockSpec((N_EXPERTS, LANES), c2), pl.BlockSpec((N_EXPERTS, LANES), c2),
            pl.BlockSpec((8, LANES), c2), pl.BlockSpec((8, LANES), c2),
        ],
        out_specs=[
            pl.BlockSpec((N_EXPERTS, tm), lambda i: (0, i)),
            pl.BlockSpec((tm, LANES), lambda i: (i, 0)),
            pl.BlockSpec((tm, LANES), lambda i: (i, 0)),
            per_tile, per_tile, per_tile, per_tile, per_tile,
            pl.BlockSpec((N_EXPERTS, LANES), c2),
        ],
        out_shape=[
            jax.ShapeDtypeStruct((N_EXPERTS, t), F32),
            jax.ShapeDtypeStruct((t, LANES), F32),
            jax.ShapeDtypeStruct((t, LANES), F32),
            per_tile_shape,
            per_tile_shape,
            per_tile_shape,
            per_tile_shape,
            per_tile_shape,
            jax.ShapeDtypeStruct((N_EXPERTS, LANES), F32),
        ],
        scratch_shapes=[pltpu.VMEM((N_EXPERTS, LANES), F32), pltpu.VMEM((8, LANES), F32),
                        pltpu.VMEM((N_EXPERTS, LANES), F32), pltpu.VMEM((8, LANES), F32)],
        compiler_params=pltpu.CompilerParams(dimension_semantics=("arbitrary",)),
        name="positions",
    )(afft, aff, thr_col, quota_col, thr_row, quota_row)


class _TileTables:
    def __init__(self, start_ref, off_ref, span_ref, full_ref, lead_ref):
        self.start_ref, self.off_ref, self.span_ref = start_ref, off_ref, span_ref
        self.full_ref, self.lead_ref = full_ref, lead_ref

    def slot(self, tile, e, k):
        return pl.multiple_of(self.start_ref[tile * N_EXPERTS + e] + k * BF16_ROWS, BF16_ROWS)

    def stack_rows(self, tile):
        last = tile * N_EXPERTS + N_EXPERTS - 1
        return self.off_ref[last] + self.span_ref[last]

    def fits(self, tile):
        return self.stack_rows(tile) <= STACK_ROWS

    def regular(self, tile):
        longest = self.span_ref[tile * N_EXPERTS]
        for e in range(1, N_EXPERTS):
            longest = jnp.maximum(longest, self.span_ref[tile * N_EXPERTS + e])
        return self.fits(tile) & (longest <= RUN_ROWS)

    def for_each_piece(self, tile, rows_ref, fn):
        for e in range(N_EXPERTS):
            n_groups = rows_ref[tile * N_EXPERTS + e] // BF16_ROWS
            row = self.off_ref[tile * N_EXPERTS + e]
            slot = self.start_ref[tile * N_EXPERTS + e]
            for groups in RUN_PIECES:
                take = (n_groups & groups) != 0
                n_rows = groups * BF16_ROWS

                @pl.when(take)
                def _(e=e, row=row, slot=slot, n_rows=n_rows):
                    fn(e, pl.multiple_of(row, BF16_ROWS), pl.multiple_of(slot, BF16_ROWS), n_rows)

                step = jnp.where(take, n_rows, 0)
                row, slot = row + step, slot + step

    def for_each_group(self, tile, base, fn):
        for e in range(N_EXPERTS):
            off = self.off_ref[tile * N_EXPERTS + e]
            n_groups = self.span_ref[tile * N_EXPERTS + e] // BF16_ROWS

            def body(k, carry, e=e, off=off):
                s = off + k * BF16_ROWS - base

                @pl.when((s >= 0) & (s < STACK_ROWS))
                def _():
                    fn(e, k, pl.multiple_of(s, BF16_ROWS))
                return carry

            lax.fori_loop(0, n_groups, body, 0)


def _dispatch_kernel(start_ref, off_ref, span_ref, full_ref, lead_ref, xn_ref, lp_ref, xe_ref,
                     onehot, stack, tails, zeros, sem, sem_fill):
    i = pl.program_id(0)
    n = pl.num_programs(0)
    slot = i % 2
    tm = xn_ref.shape[0]
    tables = _TileTables(start_ref, off_ref, span_ref, full_ref, lead_ref)

    def send(buf, e, row, dst, n_rows):
        return pltpu.make_async_copy(stack.at[buf, pl.ds(row, n_rows), :],
                                     xe_ref.at[e, pl.ds(dst, n_rows), :], sem.at[buf])

    def compact():
        stack[slot, 0:STACK_ROWS] = jnp.dot(onehot[0:STACK_ROWS, :], xn_ref[...],
                                            preferred_element_type=F32).astype(BF16)

    def lead_of(e):
        return lead_ref[i * N_EXPERTS + e].astype(F32)

    def has_tail(e):
        return span_ref[i * N_EXPERTS + e] > full_ref[i * N_EXPERTS + e]

    def join_head(e, row):
        stack[slot, pl.ds(row, BF16_ROWS), :] = stack[slot, pl.ds(row, BF16_ROWS), :] + tails[e]

    def keep_tail(e, row):
        tails[e] = stack[slot, pl.ds(row, BF16_ROWS), :]

    @pl.when(i == 0)
    def _():
        onehot[...] = jnp.zeros_like(onehot)
        stack[...] = jnp.zeros_like(stack)
        tails[...] = jnp.zeros_like(tails)

    @pl.when((i >= 2) & tables.regular(jnp.maximum(i - 2, 0)))
    def _():
        tables.for_each_piece(i - 2, full_ref, lambda e, row, dst, m: send(slot, e, row, dst, m).wait())

    @pl.when(tables.regular(i))
    def _():
        run_iota = lax.broadcasted_iota(jnp.int32, (RUN_ROWS, tm), 0).astype(F32)
        for e in range(N_EXPERTS):
            row = pl.multiple_of(off_ref[i * N_EXPERTS + e], BF16_ROWS)
            hit = lp_ref[e:e + 1, :] == run_iota - lead_of(e)
            onehot[pl.ds(row, RUN_ROWS), :] = jnp.where(hit, 1.0, 0.0).astype(BF16)
        compact()
        for e in range(N_EXPERTS):
            row = pl.multiple_of(off_ref[i * N_EXPERTS + e], BF16_ROWS)
            join_head(e, row)
            tail_row = pl.multiple_of(row + full_ref[i * N_EXPERTS + e], BF16_ROWS)
            tails[e] = jnp.where(has_tail(e), stack[slot, pl.ds(tail_row, BF16_ROWS), :],
                                 jnp.zeros((BF16_ROWS, D_MODEL), BF16))
        tables.for_each_piece(i, full_ref, lambda e, row, dst, m: send(slot, e, row, dst, m).start())

    @pl.when(jnp.logical_not(tables.regular(i)))
    def _():
        group_iota = lax.broadcasted_iota(jnp.int32, (BF16_ROWS, tm), 0).astype(F32)
        n_rounds = (tables.stack_rows(i) + STACK_ROWS - 1) // STACK_ROWS

        def one_round(r, carry):
            base = r * STACK_ROWS
            onehot[...] = jnp.zeros_like(onehot)

            def mark(e, k, s):
                hit = lp_ref[e:e + 1, :] == group_iota + (k * BF16_ROWS).astype(F32) - lead_of(e)
                onehot[pl.ds(s, BF16_ROWS), :] = jnp.where(hit, 1.0, 0.0).astype(BF16)

            def settle(e, k, s):
                @pl.when(k == 0)
                def _():
                    join_head(e, s)

                @pl.when(k * BF16_ROWS == full_ref[i * N_EXPERTS + e])
                def _():
                    keep_tail(e, s)

            def completed(e, k):
                return k * BF16_ROWS < full_ref[i * N_EXPERTS + e]

            def group_copy(e, k, s):
                return send(slot, e, s, tables.slot(i, e, k), BF16_ROWS)

            def start_completed(e, k, s):
                @pl.when(completed(e, k))
                def _():
                    group_copy(e, k, s).start()

            def wait_completed(e, k, s):
                @pl.when(completed(e, k))
                def _():
                    group_copy(e, k, s).wait()

            tables.for_each_group(i, base, mark)
            compact()
            tables.for_each_group(i, base, settle)
            tables.for_each_group(i, base, start_completed)
            tables.for_each_group(i, base, wait_completed)
            return carry

        lax.fori_loop(0, n_rounds, one_round, 0)
        for e in range(N_EXPERTS):
            @pl.when(jnp.logical_not(has_tail(e)))
            def _():
                tails[e] = jnp.zeros((BF16_ROWS, D_MODEL), BF16)

    @pl.when(i == n - 1)
    def _():
        @pl.when((i >= 1) & tables.regular(jnp.maximum(i - 1, 0)))
        def _():
            tables.for_each_piece(i - 1, full_ref,
                                  lambda e, row, dst, m: send(1 - slot, e, row, dst, m).wait())

        @pl.when(tables.regular(i))
        def _():
            tables.for_each_piece(i, full_ref, lambda e, row, dst, m: send(slot, e, row, dst, m).wait())

        zeros[...] = jnp.zeros_like(zeros)
        cap_rows = xe_ref.shape[1]
        big = zeros.shape[0]

        def fill(e, row0, rows):
            return pltpu.make_async_copy(zeros.at[pl.ds(0, rows), :],
                                         xe_ref.at[e, pl.ds(row0, rows), :], sem_fill.at[0])

        for e in range(N_EXPERTS):
            used = start_ref[i * N_EXPERTS + e] + span_ref[i * N_EXPERTS + e]
            n_small = ((-used) & (big - 1)) // BF16_ROWS
            base = used + n_small * BF16_ROWS
            n_big = (cap_rows - base) // big

            def small_row(k):
                return pl.multiple_of(used + k * BF16_ROWS, BF16_ROWS)

            def big_row(k):
                return pl.multiple_of(base + k * big, big)

            lax.fori_loop(0, n_small, lambda k, c: (fill(e, small_row(k), BF16_ROWS).start(), c)[1], 0)
            lax.fori_loop(0, n_big, lambda k, c: (fill(e, big_row(k), big).start(), c)[1], 0)
            lax.fori_loop(0, n_small, lambda k, c: (fill(e, small_row(k), BF16_ROWS).wait(), c)[1], 0)
            lax.fori_loop(0, n_big, lambda k, c: (fill(e, big_row(k), big).wait(), c)[1], 0)


def _dispatch(tables, xn, lp, cap_rows):
    t = xn.shape[0]
    tm = MOE_TILE
    return pl.pallas_call(
        _dispatch_kernel,
        grid_spec=pltpu.PrefetchScalarGridSpec(
            num_scalar_prefetch=len(tables),
            grid=(t // tm,),
            in_specs=[
                pl.BlockSpec((tm, D_MODEL), lambda i, *_: (i, 0)),
                pl.BlockSpec((N_EXPERTS, tm), lambda i, *_: (0, i)),
            ],
            out_specs=pl.BlockSpec(memory_space=pl.ANY),
            scratch_shapes=[pltpu.VMEM((STACK_ROWS + RUN_ROWS, tm), BF16),
                            pltpu.VMEM((2, STACK_ROWS + BF16_ROWS, D_MODEL), BF16),
                            pltpu.VMEM((N_EXPERTS, BF16_ROWS, D_MODEL), BF16),
                            pltpu.VMEM((FILL_ROWS, D_MODEL), BF16),
                            pltpu.SemaphoreType.DMA((2,)),
                            pltpu.SemaphoreType.DMA((1,))],
        ),
        out_shape=jax.ShapeDtypeStruct((N_EXPERTS, cap_rows, D_MODEL), BF16),
        compiler_params=pltpu.CompilerParams(dimension_semantics=("arbitrary",)),
        name="dispatch",
    )(*tables, xn, lp)


def _ffn_kernel(used_ref, xe_ref, wg_ref, wu_ref, wd_ref, ye_ref, wg, wu, wd):
    e, j = pl.program_id(0), pl.program_id(1)
    bs = xe_ref.shape[1]
    n_valid = used_ref[e] - j * bs

    @pl.when(j == 0)
    def _():
        wg[...] = wg_ref[0].astype(BF16)
        wu[...] = wu_ref[0].astype(BF16)
        wd[...] = wd_ref[0].astype(BF16)

    @pl.when(n_valid > 0)
    def _():
        row = lax.broadcasted_iota(jnp.int32, (bs, D_MODEL), 0)
        x = jnp.where(row < n_valid, xe_ref[0], jnp.zeros((bs, D_MODEL), BF16))
        gate = jnp.dot(x, wg[...], preferred_element_type=F32)
        up = jnp.dot(x, wu[...], preferred_element_type=F32)
        hdn = (jax.nn.silu(gate) * up).astype(BF16)
        ye_ref[0] = jnp.dot(hdn, wd[...], preferred_element_type=F32).astype(BF16)

    @pl.when(n_valid <= 0)
    def _():
        ye_ref[0] = jnp.zeros((bs, D_MODEL), BF16)


def _ffn(used, xe, w_gate, w_up, w_down):
    cap_rows = xe.shape[1]
    bs = FFN_BLOCK

    def x_map(e, j, used_ref):
        last = jnp.maximum((used_ref[e] + bs - 1) // bs - 1, 0)
        return (e, jnp.minimum(j, last), 0)

    w_map = lambda e, j, used_ref: (e, 0, 0)
    return pl.pallas_call(
        _ffn_kernel,
        grid_spec=pltpu.PrefetchScalarGridSpec(
            num_scalar_prefetch=1,
            grid=(N_EXPERTS, cap_rows // bs),
            in_specs=[
                pl.BlockSpec((1, bs, D_MODEL), x_map),
                pl.BlockSpec((1, D_MODEL, D_MODEL), w_map),
                pl.BlockSpec((1, D_MODEL, D_MODEL), w_map),
                pl.BlockSpec((1, D_MODEL, D_MODEL), w_map),
            ],
            out_specs=pl.BlockSpec((1, bs, D_MODEL), lambda e, j, used_ref: (e, j, 0)),
            scratch_shapes=[pltpu.VMEM((D_MODEL, D_MODEL), BF16)] * 3,
        ),
        out_shape=jax.ShapeDtypeStruct((N_EXPERTS, cap_rows, D_MODEL), BF16),
        compiler_params=pltpu.CompilerParams(dimension_semantics=("arbitrary", "arbitrary"),
                                             vmem_limit_bytes=FFN_VMEM_BYTES),
        name="expert_ffn",
    )(used, xe, w_gate, w_up, w_down)


def _gate_matrix(lpt_ref, gate_ref, offv_ref, spanv_ref, leadv_ref, base):
    off = offv_ref[0][:, 0:1]
    end = off + spanv_ref[0][:, 0:1]
    first = off + leadv_ref[0][:, 0:1]
    row = lax.broadcasted_iota(jnp.int32, (N_EXPERTS, STACK_ROWS), 1).astype(F32) + base
    owner = (row >= off) & (row < end)
    owner_pad = jnp.concatenate(
        [jnp.where(owner, 1.0, 0.0), jnp.zeros((LANES - N_EXPERTS, STACK_ROWS), F32)], axis=0).astype(BF16)
    slot_of_row = row[0:1, :] - jnp.sum(jnp.where(owner, first, 0.0), axis=0, keepdims=True)
    slot_of_token = jnp.dot(lpt_ref[...].astype(BF16), owner_pad, preferred_element_type=F32)
    gate_of_token = jnp.dot(gate_ref[...].astype(BF16), owner_pad, preferred_element_type=F32)
    return jnp.where(slot_of_token == slot_of_row, gate_of_token, 0.0).astype(BF16)


def _combine_kernel(start_ref, off_ref, span_ref, full_ref, lead_ref, x1_ref, lpt_ref, gate_ref,
                    offv_ref, spanv_ref, leadv_ref, gfin_ref, ye_ref, y_ref,
                    stack, extra, sem, sem_extra):
    i = pl.program_id(0)
    n = pl.num_programs(0)
    slot = i % 2
    tables = _TileTables(start_ref, off_ref, span_ref, full_ref, lead_ref)

    def fetch(buf, e, row, src, n_rows):
        return pltpu.make_async_copy(ye_ref.at[e, pl.ds(src, n_rows), :],
                                     stack.at[buf, pl.ds(row, n_rows), :], sem.at[buf])

    @pl.when(i == 0)
    def _():
        stack[...] = jnp.zeros_like(stack)
        extra[...] = jnp.zeros_like(extra)

        @pl.when(tables.regular(0))
        def _():
            tables.for_each_piece(0, span_ref, lambda e, row, src, m: fetch(0, e, row, src, m).start())

    nxt = jnp.minimum(i + 1, n - 1)

    @pl.when((i + 1 < n) & tables.regular(nxt))
    def _():
        tables.for_each_piece(nxt, span_ref,
                              lambda e, row, src, m: fetch(1 - slot, e, row, src, m).start())

    regular = tables.regular(i)

    @pl.when(regular)
    def _():
        tables.for_each_piece(i, span_ref, lambda e, row, src, m: fetch(slot, e, row, src, m).wait())

    moe = jnp.dot(_gate_matrix(lpt_ref, gate_ref, offv_ref, spanv_ref, leadv_ref, 0.0), stack[slot],
                  preferred_element_type=F32)
    moe = jnp.where(regular, moe, 0.0)

    n_rounds = jnp.where(regular, 0, (tables.stack_rows(i) + STACK_ROWS - 1) // STACK_ROWS)

    def one_round(r, acc):
        base = r * STACK_ROWS

        def fetch_group(e, k, s):
            return pltpu.make_async_copy(ye_ref.at[e, pl.ds(tables.slot(i, e, k), BF16_ROWS), :],
                                         extra.at[pl.ds(s, BF16_ROWS), :], sem_extra.at[0])

        tables.for_each_group(i, base, lambda e, k, s: fetch_group(e, k, s).start())
        tables.for_each_group(i, base, lambda e, k, s: fetch_group(e, k, s).wait())
        g = _gate_matrix(lpt_ref, gate_ref, offv_ref, spanv_ref, leadv_ref, base.astype(F32))
        return acc + jnp.dot(g, extra[...], preferred_element_type=F32)

    moe = lax.fori_loop(0, n_rounds, one_round, moe)
    y_ref[...] = _rms(x1_ref[...] + moe, gfin_ref[...])


def _combine(tables, x1, lpt, gate, off_vec, span_vec, lead_vec, g_final, ye):
    t = x1.shape[0]
    tm = MOE_TILE
    row = lambda i, *_: (i, 0)
    per_tile = pl.BlockSpec((1, N_EXPERTS, LANES), lambda i, *_: (i, 0, 0))
    return pl.pallas_call(
        _combine_kernel,
        grid_spec=pltpu.PrefetchScalarGridSpec(
            num_scalar_prefetch=len(tables),
            grid=(t // tm,),
            in_specs=[
                pl.BlockSpec((tm, D_MODEL), row),
                pl.BlockSpec((tm, LANES), row),
                pl.BlockSpec((tm, LANES), row),
                per_tile, per_tile, per_tile,
                pl.BlockSpec((1, D_MODEL), lambda i, *_: (0, 0)),
                pl.BlockSpec(memory_space=pl.ANY),
            ],
            out_specs=pl.BlockSpec((tm, D_MODEL), row),
            scratch_shapes=[pltpu.VMEM((2, STACK_ROWS, D_MODEL), BF16),
                            pltpu.VMEM((STACK_ROWS, D_MODEL), BF16),
                            pltpu.SemaphoreType.DMA((2,)),
                            pltpu.SemaphoreType.DMA((1,))],
        ),
        out_shape=jax.ShapeDtypeStruct((t, D_MODEL), F32),
        compiler_params=pltpu.CompilerParams(dimension_semantics=("arbitrary",)),
        name="combine",
    )(*tables, x1, lpt, gate, off_vec, span_vec, lead_vec, g_final, ye)


def _rope_tables(seq):
    half = HEAD_DIM // 2
    inv_freq = jnp.power(ROPE_THETA, -jnp.arange(half, dtype=F32) * 2.0 / HEAD_DIM)
    ang = jnp.arange(seq, dtype=F32)[:, None] * inv_freq[None, :]
    cos, sin = jnp.cos(ang), jnp.sin(ang)
    reps = LANES // HEAD_DIM
    return (jnp.tile(jnp.concatenate([cos, cos], axis=1), (1, reps)),
            jnp.tile(jnp.concatenate([-sin, sin], axis=1), (1, reps)))


def _prepare_weights(g_mix, w_in, ln_b_g, ln_b_b, w_spatial, b_spatial, g_mem, w_mem_kv,
                     g_out, w_out, g_ffn, w_router, w_gate, w_up, w_down, g_final):
    scale = jnp.ones((w_in.shape[-1],), F32)
    scale = scale.at[:D_A].set(HEAD_DIM ** -0.5 * LOG2_E)
    scale = scale.at[3 * D_A + 2 * D_B:].set((D_C // N_HEADS_C) ** -0.5)
    head_of_lane = jnp.arange(D_A) // HEAD_DIM
    return dict(
        g_mix=g_mix[0][None], w_in=(w_in[0] * scale).astype(BF16),
        ln_g=ln_b_g[0][None], ln_b=ln_b_b[0][None],
        w_sp=w_spatial[0].astype(BF16),
        b_sp=jnp.repeat(b_spatial[0].T, D_B // N_GROUPS_B, axis=1),
        g_mem=g_mem[0][None], w_kv=w_mem_kv[0].astype(BF16),
        expand=sum((jnp.arange(LANES)[:, None] == (_lse_lane(h) + dup))
                   & (head_of_lane[None, :] == h)
                   for h in range(N_HEADS_A) for dup in (0, N_HEADS_A)).astype(BF16),
        g_out=g_out[0][None], w_out=w_out[0].astype(BF16),
        g_ffn=g_ffn[0][None],
        w_router=jnp.pad(w_router[0], ((0, 0), (0, LANES - N_EXPERTS))).astype(BF16),
        w_gate=w_gate[0], w_up=w_up[0], w_down=w_down[0],
        g_final=g_final[None],
    )


def _encoder(x, mem, w):
    b, seq, _ = x.shape
    t = b * seq
    x2 = x.reshape(t, D_MODEL)
    cos_t, sin_t = _rope_tables(seq)
    mem_k, mem_v = _mem_kv(mem, w["g_mem"], w["w_kv"])
    q4, k4, v4, q16, k16, v16, u, vb, qc = _in_proj(x2, w["g_mix"], w["w_in"], cos_t, sin_t,
                                                    w["ln_g"], w["ln_b"], b, seq)
    pats = [_banded_attention(q4, k4, v4, DILATIONS[1], "attn_d1"),
            _banded_attention(q4, k4, v4, 1, "attn_d4"),
            _banded_attention(q16, k16, v16, 1, "attn_d16")]
    x1, xn, aff, afft = _mix(x2, [p[0] for p in pats], [p[1] for p in pats], u, vb, qc,
                             mem_k, mem_v, w["w_sp"], w["b_sp"], w["expand"], w["g_out"],
                             w["w_out"], w["g_ffn"], w["w_router"], seq)

    cap = EC_CAPACITY_FACTOR * t // N_EXPERTS
    thr_col, quota_col, thr_row, quota_row = _threshold(afft, cap)
    lp, lpt, gate, start, off, span, full, lead, total = _positions(
        afft, aff, thr_col, quota_col, thr_row, quota_row)
    tables = [a[:, :, 0].astype(jnp.int32).reshape(-1) for a in (start, off, span, full, lead)]
    used = total[:, 0].astype(jnp.int32)
    assert cap % BF16_ROWS == 0
    cap_rows = -(-cap // FFN_BLOCK) * FFN_BLOCK
    xe = _dispatch(tables, xn, lp, cap_rows)
    ye = _ffn(used, xe, w["w_gate"], w["w_up"], w["w_down"])
    y = _combine(tables, x1, lpt, gate, off, span, lead, w["g_final"], ye)
    return y.reshape(b, seq, D_MODEL)


def kernel(x_prompt, x_sample, mem_prompt, mem_sample, g_mix, w_in, ln_b_g, ln_b_b, w_spatial,
           b_spatial, g_mem, w_mem_kv, g_out, w_out, g_ffn, w_router, w_gate, w_up, w_down, g_final):
    w = _prepare_weights(g_mix, w_in, ln_b_g, ln_b_b, w_spatial, b_spatial, g_mem, w_mem_kv,
                         g_out, w_out, g_ffn, w_router, w_gate, w_up, w_down, g_final)
    return (_encoder(x_prompt, mem_prompt, w), _encoder(x_sample, mem_sample, w))
```

```python
import functools

import jax
import jax.numpy as jnp
from jax import lax
from jax.experimental import pallas as pl
from jax.experimental.pallas import tpu as pltpu

F32 = jnp.float32
BF16 = jnp.bfloat16

D_MODEL = 1024
N_HEADS_A = 8
HEAD_DIM = 64
D_A = 512
D_B = 256
N_GROUPS_B = 4
CHUNK_B = 128
D_C = 256
N_HEADS_C = 4
N_MEM = 256
N_EXPERTS = 16
EC_CAPACITY_FACTOR = 2
DILATIONS = (1, 4, 16)
RADIUS = 64
Q_BLOCK = 128
ROPE_THETA = 10000.0
EPS = 1e-6
NEG_INF = -1e30
LOG2_E = 1.4426950408889634

LANES = 128
BF16_ROWS = 16
TOKEN_TILE = 512
ATTN_TILE = 1024
MOE_TILE = 256
POSITION_TILES = 4
STACK_ROWS = 896
FILL_ROWS = 64
RUN_ROWS = 128
RUN_PIECES = (8, 4, 2, 1)
WAIT_PIECES = (32, 16, 8, 4, 2, 1)
FFN_BLOCK = 512
FFN_VMEM_BYTES = 52 * 1024 * 1024
UNSELECTED = -4096.0
BINADE_STEPS = (64, 32, 16, 8, 4, 2, 1)
MANTISSA_STEPS = 52


def _rms(x, g):
    return x * lax.rsqrt(jnp.mean(x * x, axis=-1, keepdims=True) + EPS) * g


def _mem_kv_kernel(mem_ref, g_ref, w_ref, k_ref, v_ref):
    h = _rms(mem_ref[0], g_ref[...]).astype(BF16)
    kv = jnp.dot(h, w_ref[...], preferred_element_type=F32)
    k_ref[0] = kv[:, :D_C].astype(BF16)
    v_ref[0] = kv[:, D_C:].astype(BF16)


def _mem_kv(mem, g_mem, w_kv):
    b = mem.shape[0]
    return pl.pallas_call(
        _mem_kv_kernel,
        grid=(b,),
        in_specs=[
            pl.BlockSpec((1, N_MEM, D_MODEL), lambda i: (i, 0, 0)),
            pl.BlockSpec((1, D_MODEL), lambda i: (0, 0)),
            pl.BlockSpec((D_MODEL, 2 * D_C), lambda i: (0, 0)),
        ],
        out_specs=[
            pl.BlockSpec((1, N_MEM, D_C), lambda i: (i, 0, 0)),
            pl.BlockSpec((1, N_MEM, D_C), lambda i: (i, 0, 0)),
        ],
        out_shape=[jax.ShapeDtypeStruct((b, N_MEM, D_C), BF16)] * 2,
        name="mem_kv",
    )(mem, g_mem, w_kv)


def _in_proj_kernel(x_ref, g_ref, w_ref, cos_ref, sin_ref, lng_ref, lnb_ref,
                    q4_ref, k4_ref, v4_ref, q16_ref, k16_ref, v16_ref, u_ref, vb_ref, qc_ref,
                    chunks, chunks4):
    tm = x_ref.shape[0]
    d4 = DILATIONS[1]
    n_chunks = D_A // LANES

    def emit_by_residue(z, out4_ref, out16_ref):
        for c in range(n_chunks):
            chunks[c] = z[:, c * LANES:(c + 1) * LANES]
        quarter = tm // d4
        for r in range(d4):
            rows = [chunks[c, pl.ds(r, quarter, stride=d4), :] for c in range(n_chunks)]
            out4_ref[0, r] = jnp.concatenate(rows, axis=1).astype(BF16)
            for c in range(n_chunks):
                chunks4[c, r * quarter:(r + 1) * quarter, :] = rows[c]
        for r in range(d4):
            for a in range(d4):
                rows = [chunks4[c, pl.ds(r * quarter + a, quarter // d4, stride=d4), :]
                        for c in range(n_chunks)]
                out16_ref[0, r + d4 * a] = jnp.concatenate(rows, axis=1).astype(BF16)

    h = _rms(x_ref[...], g_ref[...]).astype(BF16)
    cos = jnp.concatenate([cos_ref[...]] * (D_A // LANES), axis=1)
    sin = jnp.concatenate([sin_ref[...]] * (D_A // LANES), axis=1)
    lane = lax.broadcasted_iota(jnp.int32, (tm, D_A), 1)
    first_half = (lane & (HEAD_DIM - 1)) < (HEAD_DIM // 2)

    def rope(z):
        rot = jnp.where(first_half, pltpu.roll(z, D_A - HEAD_DIM // 2, 1),
                        pltpu.roll(z, HEAD_DIM // 2, 1))
        return z * cos + rot * sin

    def proj(lo, hi):
        return jnp.dot(h, w_ref[:, lo:hi], preferred_element_type=F32)

    emit_by_residue(rope(proj(0, D_A)), q4_ref, q16_ref)
    emit_by_residue(rope(proj(D_A, 2 * D_A)), k4_ref, k16_ref)
    emit_by_residue(proj(2 * D_A, 3 * D_A), v4_ref, v16_ref)
    u_ref[...] = jax.nn.gelu(proj(3 * D_A, 3 * D_A + D_B)).astype(BF16)
    vb = jax.nn.gelu(proj(3 * D_A + D_B, 3 * D_A + 2 * D_B))
    mu = jnp.mean(vb, axis=-1, keepdims=True)
    var = jnp.mean(jnp.square(vb - mu), axis=-1, keepdims=True)
    vb_ref[...] = ((vb - mu) * lax.rsqrt(var + EPS) * lng_ref[...] + lnb_ref[...]).astype(BF16)
    qc_ref[...] = proj(3 * D_A + 2 * D_B, 3 * D_A + 2 * D_B + D_C).astype(BF16)


def _residue_spec(dil, tm, tiles_per_seq, width):
    return pl.BlockSpec((1, dil, tm // dil, width),
                        lambda i: (i // tiles_per_seq, 0, i % tiles_per_seq, 0))


def _in_proj(x2, g_mix, w_in, cos_t, sin_t, ln_g, ln_b, b, seq):
    t = x2.shape[0]
    tm = TOKEN_TILE
    d_in = w_in.shape[1]
    tiles_per_seq = seq // tm
    row = lambda i: (i, 0)
    const = lambda i: (0, 0)
    pos = lambda i: (i % tiles_per_seq, 0)
    d4, d16 = DILATIONS[1], DILATIONS[2]
    res4 = _residue_spec(d4, tm, tiles_per_seq, D_A)
    res16 = _residue_spec(d16, tm, tiles_per_seq, D_A)
    shape4 = jax.ShapeDtypeStruct((b, d4, seq // d4, D_A), BF16)
    shape16 = jax.ShapeDtypeStruct((b, d16, seq // d16, D_A), BF16)
    return pl.pallas_call(
        _in_proj_kernel,
        grid=(t // tm,),
        in_specs=[
            pl.BlockSpec((tm, D_MODEL), row),
            pl.BlockSpec((1, D_MODEL), const),
            pl.BlockSpec((D_MODEL, d_in), const),
            pl.BlockSpec((tm, LANES), pos),
            pl.BlockSpec((tm, LANES), pos),
            pl.BlockSpec((1, D_B), const),
            pl.BlockSpec((1, D_B), const),
        ],
        out_specs=[res4] * 3 + [res16] * 3 + [pl.BlockSpec((tm, w), row) for w in (D_B, D_B, D_C)],
        out_shape=[shape4] * 3 + [shape16] * 3
                  + [jax.ShapeDtypeStruct((t, w), BF16) for w in (D_B, D_B, D_C)],
        scratch_shapes=[pltpu.VMEM((D_A // LANES, tm, LANES), F32)] * 2,
        name="in_proj",
    )(x2, g_mix, w_in, cos_t, sin_t, ln_g, ln_b)


def _lse_lane(head):
    return head + HEAD_DIM * (1 - head % 2)


def _attn_kernel(q_ref, kp_ref, kc_ref, kn_ref, vp_ref, vc_ref, vn_ref,
                 o_ref, lse_ref, kwin, vwin, bias, scores, probs, dens, *, nres, tr, sub):
    j = pl.program_id(2)
    halo = RADIUS // nres
    qb = Q_BLOCK // nres
    for win, (p_ref, c_ref, n_ref) in ((kwin, (kp_ref, kc_ref, kn_ref)), (vwin, (vp_ref, vc_ref, vn_ref))):
        win[:, 0:halo] = p_ref[0]
        win[:, halo:halo + tr] = c_ref[0]
        win[:, halo + tr:] = n_ref[0]

    t_idx = lax.broadcasted_iota(jnp.int32, (Q_BLOCK, 2 * Q_BLOCK), 0)
    s_idx = lax.broadcasted_iota(jnp.int32, (Q_BLOCK, 2 * Q_BLOCK), 1)
    q_res, q_row = t_idx >> (qb.bit_length() - 1), t_idx & (qb - 1)
    k_res, k_row = s_idx >> qb.bit_length(), s_idx & (2 * qb - 1)
    band = jnp.abs(nres * (q_row - k_row + halo) + q_res - k_res) <= RADIUS
    k_elem = nres * (k_row - halo) + k_res
    lane = lax.broadcasted_iota(jnp.int32, (Q_BLOCK, LANES), 1)
    low_half = lane < HEAD_DIM

    def block(i, carry):
        r0 = pl.multiple_of(i * qb, qb)
        kpos = k_elem + nres * (j * tr + r0)
        bias[...] = jnp.where(band & (kpos >= 0) & (kpos < sub), 0.0, NEG_INF)
        stack = lambda pieces: pieces[0] if nres == 1 else jnp.concatenate(pieces, axis=0)
        pair_cols = lambda hp: slice(hp * LANES, (hp + 1) * LANES)

        for hp in range(N_HEADS_A // 2):
            q_pair = stack([q_ref[0, r, pl.ds(r0, qb), pair_cols(hp)] for r in range(nres)])
            k_pair = stack([kwin[r, pl.ds(r0, 2 * qb), pair_cols(hp)] for r in range(nres)])
            for half in range(2):
                keep = low_half if half == 0 else jnp.logical_not(low_half)
                qm = jnp.where(keep, q_pair, jnp.zeros_like(q_pair))
                scores[2 * hp + half] = lax.dot_general(qm, k_pair, (((1,), (1,)), ((), ())),
                                                        preferred_element_type=F32)

        lse_blk = jnp.zeros((Q_BLOCK, LANES), F32)
        for head in range(N_HEADS_A):
            s = scores[head] + bias[...]
            m = jnp.max(s, axis=-1, keepdims=True)
            p = jnp.exp2(s - m)
            den = jnp.sum(p, axis=-1, keepdims=True)
            probs[head] = p.astype(BF16)
            dens[head] = jnp.broadcast_to(den, (Q_BLOCK, LANES))
            hit = (lane == _lse_lane(head)) | (lane == _lse_lane(head) + N_HEADS_A)
            lse_blk = jnp.where(hit, m + jnp.log(den) * LOG2_E, lse_blk)
        for r in range(nres):
            lse_ref[0, r, pl.ds(r0, qb), :] = lse_blk[r * qb:(r + 1) * qb]

        for hp in range(N_HEADS_A // 2):
            v_pair = stack([vwin[r, pl.ds(r0, 2 * qb), pair_cols(hp)] for r in range(nres)])
            halves = [jnp.dot(probs[2 * hp + half], v_pair, preferred_element_type=F32)
                      / dens[2 * hp + half] for half in range(2)]
            out = jnp.where(low_half, halves[0], halves[1]).astype(BF16)
            for r in range(nres):
                o_ref[0, r, pl.ds(r0, qb), pair_cols(hp)] = out[r * qb:(r + 1) * qb]
        return carry

    lax.fori_loop(0, tr // qb, block, 0)


def _banded_attention(q, k, v, nres, name):
    b, n_res, rows, _ = q.shape
    tr = min(ATTN_TILE // nres, rows)
    halo = RADIUS // nres
    halo_per_tile = tr // halo
    last_halo = rows // halo - 1
    if nres == 1:
        grid = (b, n_res, rows // tr)
        at = lambda row_block: (lambda bi, r, j: (bi, r, row_block(j), 0))
    else:
        grid = (b, 1, rows // tr)
        at = lambda row_block: (lambda bi, r, j: (bi, 0, row_block(j), 0))
    cur = pl.BlockSpec((1, nres, tr, D_A), at(lambda j: j))
    prev = pl.BlockSpec((1, nres, halo, D_A), at(lambda j: jnp.maximum(j * halo_per_tile - 1, 0)))
    nxt = pl.BlockSpec((1, nres, halo, D_A),
                       at(lambda j: jnp.minimum((j + 1) * halo_per_tile, last_halo)))
    return pl.pallas_call(
        functools.partial(_attn_kernel, nres=nres, tr=tr, sub=nres * rows),
        grid=grid,
        in_specs=[cur, prev, cur, nxt, prev, cur, nxt],
        out_specs=[cur, pl.BlockSpec((1, nres, tr, LANES), at(lambda j: j))],
        out_shape=[
            jax.ShapeDtypeStruct((b, n_res, rows, D_A), BF16),
            jax.ShapeDtypeStruct((b, n_res, rows, LANES), F32),
        ],
        scratch_shapes=[pltpu.VMEM((nres, tr + 2 * halo, D_A), BF16)] * 2
                       + [pltpu.VMEM((Q_BLOCK, 2 * Q_BLOCK), F32),
                          pltpu.VMEM((N_HEADS_A, Q_BLOCK, 2 * Q_BLOCK), F32),
                          pltpu.VMEM((N_HEADS_A, Q_BLOCK, 2 * Q_BLOCK), BF16),
                          pltpu.VMEM((N_HEADS_A, Q_BLOCK, LANES), F32)],
        name=name,
    )(q, k, k, k, v, v, v)


def _split_bf16(x):
    hi = x.astype(BF16)
    lo = (x - hi.astype(F32)).astype(BF16)
    return hi, lo


def _mix_kernel(x_ref, o1_ref, o2_ref, o3_ref, l1_ref, l2_ref, l3_ref, u_ref, vb_ref, qc_ref,
                mk_ref, mv_ref, ws_ref, bs_ref, expand_ref, go_ref, wo_ref, gf_ref, wr_ref,
                x1_ref, xn_ref, aff_ref, afft_ref, obuf1, obuf2, obuf3, lbuf1, lbuf2, lbuf3):
    tm = x_ref.shape[0]

    def token_order(ref, buf):
        dil, width = ref.shape[1], ref.shape[3]
        for r in range(dil):
            piece = ref[0, r].astype(F32)
            for c in range(width // LANES):
                buf[c, pl.ds(r, tm // dil, stride=dil), :] = piece[:, c * LANES:(c + 1) * LANES]
        return jnp.concatenate([buf[c] for c in range(width // LANES)], axis=1)

    l1, l2, l3 = token_order(l1_ref, lbuf1), token_order(l2_ref, lbuf2), token_order(l3_ref, lbuf3)
    mx = jnp.maximum(jnp.maximum(l1, l2), l3)
    e1, e2, e3 = jnp.exp2(l1 - mx), jnp.exp2(l2 - mx), jnp.exp2(l3 - mx)
    tot = e1 + e2 + e3
    is_hi_lane = (lax.broadcasted_iota(jnp.int32, (tm, LANES), 1) & N_HEADS_A) == 0
    o_a = jnp.zeros((tm, D_A), F32)
    for e, o_ref, buf in ((e1, o1_ref, obuf1), (e2, o2_ref, obuf2), (e3, o3_ref, obuf3)):
        hi, lo = _split_bf16(e / tot)
        w = jnp.dot(jnp.where(is_hi_lane, hi, lo), expand_ref[...], preferred_element_type=F32)
        o_a = o_a + w * token_order(o_ref, buf)

    lane_b = lax.broadcasted_iota(jnp.int32, (CHUNK_B, D_B), 1)
    group_w = D_B // N_GROUPS_B
    gated = []
    for c in range(tm // CHUNK_B):
        vchunk = vb_ref[c * CHUNK_B:(c + 1) * CHUNK_B, :]
        acc = bs_ref[...]
        for g in range(N_GROUPS_B):
            y = jnp.dot(ws_ref[g], vchunk, preferred_element_type=F32)
            acc = acc + jnp.where(lane_b // group_w == g, y, 0.0)
        gated.append(acc)
    o_b = u_ref[...].astype(F32) * jnp.concatenate(gated, axis=0)

    lane_c = lax.broadcasted_iota(jnp.int32, (tm, LANES), 1)
    low_half = lane_c < HEAD_DIM
    oc_parts = []
    for hp in range(N_HEADS_C // 2):
        cols = slice(hp * LANES, (hp + 1) * LANES)
        q_pair = qc_ref[:, cols]
        k_pair = mk_ref[0, :, cols]
        v_pair = mv_ref[0, :, cols]
        halves = []
        for half in range(2):
            keep = low_half if half == 0 else jnp.logical_not(low_half)
            qm = jnp.where(keep, q_pair, jnp.zeros_like(q_pair))
            s = lax.dot_general(qm, k_pair, (((1,), (1,)), ((), ())), preferred_element_type=F32)
            m = jnp.max(s, axis=-1, keepdims=True)
            p = jnp.exp(s - m)
            p = p / jnp.sum(p, axis=-1, keepdims=True)
            halves.append(jnp.dot(p.astype(BF16), v_pair, preferred_element_type=F32))
        oc_parts.append(jnp.where(low_half, halves[0], halves[1]))
    o_c = jnp.concatenate(oc_parts, axis=1)

    go = go_ref[...]
    o = jnp.concatenate([_rms(o_a, go[:, :D_A]), _rms(o_b, go[:, D_A:D_A + D_B]),
                         _rms(o_c, go[:, D_A + D_B:])], axis=1).astype(BF16)
    x1 = x_ref[...] + jnp.dot(o, wo_ref[...], preferred_element_type=F32)
    x1_ref[...] = x1

    xn = _rms(x1, gf_ref[...]).astype(BF16)
    xn_ref[...] = xn
    logits = jnp.dot(xn, wr_ref[...], preferred_element_type=F32)
    is_expert = lane_c < N_EXPERTS
    logits = jnp.where(is_expert, logits, NEG_INF)
    m = jnp.max(logits, axis=-1, keepdims=True)
    ex = jnp.where(is_expert, jnp.exp(logits - m), 0.0)
    aff = ex / jnp.sum(ex, axis=-1, keepdims=True)
    aff_ref[...] = aff
    afft_ref[...] = jnp.transpose(aff)[:N_EXPERTS, :]


def _mix(x2, o_pats, lse_pats, u, vb, qc, mem_k, mem_v, w_sp, b_sp, expand, g_out, w_out,
         g_ffn, w_router, seq):
    t = x2.shape[0]
    tm = TOKEN_TILE
    tiles_per_seq = seq // tm
    row = lambda i: (i, 0)
    const2 = lambda i: (0, 0)
    const3 = lambda i: (0, 0, 0)
    batch = lambda i: (i // tiles_per_seq, 0, 0)
    dils = [o.shape[1] for o in o_pats]
    return pl.pallas_call(
        _mix_kernel,
        grid=(t // tm,),
        in_specs=[
            pl.BlockSpec((tm, D_MODEL), row),
            *[_residue_spec(d, tm, tiles_per_seq, D_A) for d in dils],
            *[_residue_spec(d, tm, tiles_per_seq, LANES) for d in dils],
            pl.BlockSpec((tm, D_B), row), pl.BlockSpec((tm, D_B), row), pl.BlockSpec((tm, D_C), row),
            pl.BlockSpec((1, N_MEM, D_C), batch), pl.BlockSpec((1, N_MEM, D_C), batch),
            pl.BlockSpec((N_GROUPS_B, CHUNK_B, CHUNK_B), const3),
            pl.BlockSpec((CHUNK_B, D_B), const2),
            pl.BlockSpec((LANES, D_A), const2),
            pl.BlockSpec((1, D_MODEL), const2),
            pl.BlockSpec((D_MODEL, D_MODEL), const2),
            pl.BlockSpec((1, D_MODEL), const2),
            pl.BlockSpec((D_MODEL, LANES), const2),
        ],
        out_specs=[
            pl.BlockSpec((tm, D_MODEL), row),
            pl.BlockSpec((tm, D_MODEL), row),
            pl.BlockSpec((tm, LANES), row),
            pl.BlockSpec((N_EXPERTS, tm), lambda i: (0, i)),
        ],
        out_shape=[
            jax.ShapeDtypeStruct((t, D_MODEL), F32),
            jax.ShapeDtypeStruct((t, D_MODEL), BF16),
            jax.ShapeDtypeStruct((t, LANES), F32),
            jax.ShapeDtypeStruct((N_EXPERTS, t), F32),
        ],
        scratch_shapes=[pltpu.VMEM((D_A // LANES, tm, LANES), F32)] * 3
                       + [pltpu.VMEM((1, tm, LANES), F32)] * 3,
        name="mix",
    )(x2, *o_pats, *lse_pats, u, vb, qc, mem_k, mem_v, w_sp, b_sp, expand, g_out, w_out,
      g_ffn, w_router)


def _threshold_kernel(afft_ref, thr_col_ref, quota_col_ref, thr_row_ref, quota_row_ref, *, cap):
    aff = afft_ref[...]

    def enough(v):
        return jnp.sum(jnp.where(aff >= v, 1.0, 0.0), axis=-1, keepdims=True) >= cap

    hi = jnp.full((N_EXPERTS, 1), 2.0, F32)
    for shift in BINADE_STEPS:
        cand = hi * (2.0 ** -shift)
        hi = jnp.where(enough(cand), hi, cand)
    lo = jnp.where(enough(hi * 0.5), hi * 0.5, 0.0)

    def bisect(_, bracket):
        lo, hi = bracket
        mid = (lo + hi) * 0.5
        ok = enough(mid)
        return jnp.where(ok, mid, lo), jnp.where(ok, hi, mid)

    lo, hi = lax.fori_loop(0, MANTISSA_STEPS, bisect, (lo, hi))
    thr_f = jnp.min(jnp.where(aff >= lo, aff, jnp.inf), axis=-1, keepdims=True)
    n_gt = jnp.sum(jnp.where(aff > thr_f, 1.0, 0.0), axis=-1, keepdims=True)
    quota = cap - n_gt
    thr_col = jnp.broadcast_to(thr_f, (N_EXPERTS, LANES))
    quota_col = jnp.broadcast_to(quota, (N_EXPERTS, LANES))
    thr_col_ref[...] = thr_col
    quota_col_ref[...] = quota_col
    diag = (lax.broadcasted_iota(jnp.int32, (N_EXPERTS, LANES), 0)
            == lax.broadcasted_iota(jnp.int32, (N_EXPERTS, LANES), 1))
    thr_row = jnp.sum(jnp.where(diag, thr_col, 0.0), axis=0, keepdims=True)
    quota_row = jnp.sum(jnp.where(diag, quota_col, 0.0), axis=0, keepdims=True)
    thr_row_ref[...] = jnp.broadcast_to(thr_row, (8, LANES))
    quota_row_ref[...] = jnp.broadcast_to(quota_row, (8, LANES))


def _threshold(afft, cap):
    t = afft.shape[1]
    full = lambda shape: pl.BlockSpec(shape, lambda: (0,) * len(shape))
    return pl.pallas_call(
        functools.partial(_threshold_kernel, cap=float(cap)),
        in_specs=[full((N_EXPERTS, t))],
        out_specs=[full((N_EXPERTS, LANES)), full((N_EXPERTS, LANES)),
                   full((8, LANES)), full((8, LANES))],
        out_shape=[jax.ShapeDtypeStruct((N_EXPERTS, LANES), F32)] * 2
                  + [jax.ShapeDtypeStruct((8, LANES), F32)] * 2,
        name="threshold",
    )(afft)


def _positions_kernel(afft_ref, aff_ref, thr_col_ref, quota_col_ref, thr_row_ref, quota_row_ref,
                      lp_ref, lpt_ref, gate_ref, start_ref, off_ref, span_ref, full_ref, lead_ref,
                      total_ref, eq_col, eq_row, slot_col, slot_row):
    i = pl.program_id(0)
    tm = MOE_TILE

    @pl.when(i == 0)
    def _():
        eq_col[...] = jnp.zeros_like(eq_col)
        eq_row[...] = jnp.zeros_like(eq_row)
        slot_col[...] = jnp.zeros_like(slot_col)
        slot_row[...] = jnp.zeros_like(slot_row)

    r = lax.broadcasted_iota(jnp.int32, (tm, tm), 0)
    c = lax.broadcasted_iota(jnp.int32, (tm, tm), 1)
    before = jnp.where(r < c, 1.0, 0.0).astype(BF16)
    after = jnp.where(c < r, 1.0, 0.0).astype(BF16)
    thr, quota = thr_col_ref[:, 0:1], quota_col_ref[:, 0:1]
    thr_r, quota_r = thr_row_ref[0:1, :], quota_row_ref[0:1, :]
    is_expert = lax.broadcasted_iota(jnp.int32, (tm, LANES), 1) < N_EXPERTS
    earlier_expert = (lax.broadcasted_iota(jnp.int32, (N_EXPERTS, LANES), 1)
                      < lax.broadcasted_iota(jnp.int32, (N_EXPERTS, LANES), 0))
    tile_floor = lambda n: jnp.floor(n / BF16_ROWS) * BF16_ROWS
    tile_ceil = lambda n: jnp.ceil(n / BF16_ROWS) * BF16_ROWS

    eq_seen_col, eq_seen_row = eq_col[:, 0:1], eq_row[0:1, :]
    first_slot, first_slot_row = slot_col[...], slot_row[0:1, :]
    for s in range(aff_ref.shape[0] // tm):
        a = afft_ref[:, s * tm:(s + 1) * tm]
        eq = a == thr
        eq_f = jnp.where(eq, 1.0, 0.0)
        eq_rank = jnp.dot(eq_f.astype(BF16), before, preferred_element_type=F32) + eq_seen_col
        sel = (a > thr) | (eq & (eq_rank < quota))
        sel_f = jnp.where(sel, 1.0, 0.0)
        lp = jnp.dot(sel_f.astype(BF16), before, preferred_element_type=F32)
        lp_ref[:, s * tm:(s + 1) * tm] = jnp.where(sel, lp, UNSELECTED)
        count = jnp.sum(sel_f, axis=-1, keepdims=True)
        start = tile_floor(first_slot)
        lead = first_slot - start
        start_ref[s] = start
        lead_ref[s] = lead
        span_ref[s] = tile_ceil(lead + count)
        full_ref[s] = tile_floor(lead + count)
        first_slot = first_slot + count
        eq_seen_col = eq_seen_col + jnp.sum(eq_f, axis=-1, keepdims=True)

        at = aff_ref[s * tm:(s + 1) * tm, :]
        eq_t = (at == thr_r) & is_expert
        eq_tf = jnp.where(eq_t, 1.0, 0.0)
        eq_rank_t = jnp.dot(after, eq_tf.astype(BF16), preferred_element_type=F32) + eq_seen_row
        sel_t = ((at > thr_r) & is_expert) | (eq_t & (eq_rank_t < quota_r))
        sel_tf = jnp.where(sel_t, 1.0, 0.0)
        lp_t = jnp.dot(after, sel_tf.astype(BF16), preferred_element_type=F32)
        lpt_ref[s * tm:(s + 1) * tm, :] = jnp.where(sel_t, lp_t, UNSELECTED)
        gate_ref[s * tm:(s + 1) * tm, :] = jnp.where(sel_t, at, 0.0)
        eq_seen_row = eq_seen_row + jnp.sum(eq_tf, axis=0, keepdims=True)
        count_row = jnp.sum(sel_tf, axis=0, keepdims=True)
        span_row = tile_ceil(first_slot_row - tile_floor(first_slot_row) + count_row)
        off = jnp.sum(jnp.where(earlier_expert, span_row, 0.0), axis=-1, keepdims=True)
        off_ref[s] = jnp.broadcast_to(off, (N_EXPERTS, LANES))
        first_slot_row = first_slot_row + count_row

    eq_col[...] = jnp.broadcast_to(eq_seen_col, eq_col.shape)
    eq_row[...] = jnp.broadcast_to(eq_seen_row, eq_row.shape)
    slot_col[...] = first_slot
    slot_row[...] = jnp.broadcast_to(first_slot_row, slot_row.shape)
    total_ref[...] = first_slot


def _positions(afft, aff, thr_col, quota_col, thr_row, quota_row):
    t = aff.shape[0]
    tm = MOE_TILE * POSITION_TILES
    n_tiles = t // MOE_TILE
    c2 = lambda i: (0, 0)
    per_tile = pl.BlockSpec((POSITION_TILES, N_EXPERTS, LANES), lambda i: (i, 0, 0))
    per_tile_shape = jax.ShapeDtypeStruct((n_tiles, N_EXPERTS, LANES), F32)
    return pl.pallas_call(
        _positions_kernel,
        grid=(t // tm,),
        in_specs=[
            pl.BlockSpec((N_EXPERTS, tm), lambda i: (0, i)),
            pl.BlockSpec((tm, LANES), lambda i: (i, 0)),
            pl.BlockSpec((N_EXPERTS, LANES), c2), pl.BlockSpec((N_EXPERTS, LANES), c2),
            pl.BlockSpec((8, LANES), c2), pl.BlockSpec((8, LANES), c2),
        ],
        out_specs=[
            pl.BlockSpec((N_EXPERTS, tm), lambda i: (0, i)),
            pl.BlockSpec((tm, LANES), lambda i: (i, 0)),
            pl.BlockSpec((tm, LANES), lambda i: (i, 0)),
            per_tile, per_tile, per_tile, per_tile, per_tile,
            pl.BlockSpec((N_EXPERTS, LANES), c2),
        ],
        out_shape=[
            jax.ShapeDtypeStruct((N_EXPERTS, t), F32),
            jax.ShapeDtypeStruct((t, LANES), F32),
            jax.ShapeDtypeStruct((t, LANES), F32),
            per_tile_shape,
            per_tile_shape,
            per_tile_shape,
            per_tile_shape,
            per_tile_shape,
            jax.ShapeDtypeStruct((N_EXPERTS, LANES), F32),
        ],
        scratch_shapes=[pltpu.VMEM((N_EXPERTS, LANES), F32), pltpu.VMEM((8, LANES), F32),
                        pltpu.VMEM((N_EXPERTS, LANES), F32), pltpu.VMEM((8, LANES), F32)],
        compiler_params=pltpu.CompilerParams(dimension_semantics=("arbitrary",)),
        name="positions",
    )(afft, aff, thr_col, quota_col, thr_row, quota_row)


class _TileTables:
    def __init__(self, start_ref, off_ref, span_ref, full_ref, lead_ref):
        self.start_ref, self.off_ref, self.span_ref = start_ref, off_ref, span_ref
        self.full_ref, self.lead_ref = full_ref, lead_ref

    def slot(self, tile, e, k):
        return pl.multiple_of(self.start_ref[tile * N_EXPERTS + e] + k * BF16_ROWS, BF16_ROWS)

    def stack_rows(self, tile):
        last = tile * N_EXPERTS + N_EXPERTS - 1
        return self.off_ref[last] + self.span_ref[last]

    def fits(self, tile):
        return self.stack_rows(tile) <= STACK_ROWS

    def regular(self, tile):
        longest = self.span_ref[tile * N_EXPERTS]
        for e in range(1, N_EXPERTS):
            longest = jnp.maximum(longest, self.span_ref[tile * N_EXPERTS + e])
        return self.fits(tile) & (longest <= RUN_ROWS)

    def total_rows(self, tile, rows_ref):
        total = rows_ref[tile * N_EXPERTS]
        for e in range(1, N_EXPERTS):
            total = total + rows_ref[tile * N_EXPERTS + e]
        return total

    @staticmethod
    def wait_rows(n_rows, make_copy):
        n_groups = n_rows // BF16_ROWS
        for groups in WAIT_PIECES:
            @pl.when((n_groups & groups) != 0)
            def _(groups=groups):
                make_copy(groups * BF16_ROWS).wait()

    def for_each_piece(self, tile, rows_ref, fn):
        for e in range(N_EXPERTS):
            n_groups = rows_ref[tile * N_EXPERTS + e] // BF16_ROWS
            row = self.off_ref[tile * N_EXPERTS + e]
            slot = self.start_ref[tile * N_EXPERTS + e]
            for groups in RUN_PIECES:
                take = (n_groups & groups) != 0
                n_rows = groups * BF16_ROWS

                @pl.when(take)
                def _(e=e, row=row, slot=slot, n_rows=n_rows):
                    fn(e, pl.multiple_of(row, BF16_ROWS), pl.multiple_of(slot, BF16_ROWS), n_rows)

                step = jnp.where(take, n_rows, 0)
                row, slot = row + step, slot + step

    def for_each_group(self, tile, base, fn):
        for e in range(N_EXPERTS):
            off = self.off_ref[tile * N_EXPERTS + e]
            n_groups = self.span_ref[tile * N_EXPERTS + e] // BF16_ROWS

            def body(k, carry, e=e, off=off):
                s = off + k * BF16_ROWS - base

                @pl.when((s >= 0) & (s < STACK_ROWS))
                def _():
                    fn(e, k, pl.multiple_of(s, BF16_ROWS))
                return carry

            lax.fori_loop(0, n_groups, body, 0)


def _dispatch_kernel(start_ref, off_ref, span_ref, full_ref, lead_ref, xn_ref, lp_ref, xe_ref,
                     onehot, stack, tails, zeros, sem, sem_fill):
    i = pl.program_id(0)
    n = pl.num_programs(0)
    slot = i % 2
    tm = xn_ref.shape[0]
    tables = _TileTables(start_ref, off_ref, span_ref, full_ref, lead_ref)

    def send(buf, e, row, dst, n_rows):
        return pltpu.make_async_copy(stack.at[buf, pl.ds(row, n_rows), :],
                                     xe_ref.at[e, pl.ds(dst, n_rows), :], sem.at[buf])

    def compact():
        stack[slot, 0:STACK_ROWS] = jnp.dot(onehot[0:STACK_ROWS, :], xn_ref[...],
                                            preferred_element_type=F32).astype(BF16)

    def lead_of(e):
        return lead_ref[i * N_EXPERTS + e].astype(F32)

    def has_tail(e):
        return span_ref[i * N_EXPERTS + e] > full_ref[i * N_EXPERTS + e]

    def join_head(e, row):
        stack[slot, pl.ds(row, BF16_ROWS), :] = stack[slot, pl.ds(row, BF16_ROWS), :] + tails[e]

    def keep_tail(e, row):
        tails[e] = stack[slot, pl.ds(row, BF16_ROWS), :]

    @pl.when(i == 0)
    def _():
        onehot[...] = jnp.zeros_like(onehot)
        stack[...] = jnp.zeros_like(stack)
        tails[...] = jnp.zeros_like(tails)

    @pl.when((i >= 2) & tables.regular(jnp.maximum(i - 2, 0)))
    def _():
        tables.wait_rows(tables.total_rows(i - 2, full_ref), lambda m: send(slot, 0, 0, 0, m))

    @pl.when(tables.regular(i))
    def _():
        run_iota = lax.broadcasted_iota(jnp.int32, (RUN_ROWS, tm), 0).astype(F32)
        for e in range(N_EXPERTS):
            row = pl.multiple_of(off_ref[i * N_EXPERTS + e], BF16_ROWS)
            hit = lp_ref[e:e + 1, :] == run_iota - lead_of(e)
            onehot[pl.ds(row, RUN_ROWS), :] = jnp.where(hit, 1.0, 0.0).astype(BF16)
        compact()
        for e in range(N_EXPERTS):
            row = pl.multiple_of(off_ref[i * N_EXPERTS + e], BF16_ROWS)
            join_head(e, row)
            tail_row = pl.multiple_of(row + full_ref[i * N_EXPERTS + e], BF16_ROWS)
            tails[e] = jnp.where(has_tail(e), stack[slot, pl.ds(tail_row, BF16_ROWS), :],
                                 jnp.zeros((BF16_ROWS, D_MODEL), BF16))
        tables.for_each_piece(i, full_ref, lambda e, row, dst, m: send(slot, e, row, dst, m).start())

    @pl.when(jnp.logical_not(tables.regular(i)))
    def _():
        group_iota = lax.broadcasted_iota(jnp.int32, (BF16_ROWS, tm), 0).astype(F32)
        n_rounds = (tables.stack_rows(i) + STACK_ROWS - 1) // STACK_ROWS

        def one_round(r, carry):
            base = r * STACK_ROWS
            onehot[...] = jnp.zeros_like(onehot)

            def mark(e, k, s):
                hit = lp_ref[e:e + 1, :] == group_iota + (k * BF16_ROWS).astype(F32) - lead_of(e)
                onehot[pl.ds(s, BF16_ROWS), :] = jnp.where(hit, 1.0, 0.0).astype(BF16)

            def settle(e, k, s):
                @pl.when(k == 0)
                def _():
                    join_head(e, s)

                @pl.when(k * BF16_ROWS == full_ref[i * N_EXPERTS + e])
                def _():
                    keep_tail(e, s)

            def completed(e, k):
                return k * BF16_ROWS < full_ref[i * N_EXPERTS + e]

            def group_copy(e, k, s):
                return send(slot, e, s, tables.slot(i, e, k), BF16_ROWS)

            def start_completed(e, k, s):
                @pl.when(completed(e, k))
                def _():
                    group_copy(e, k, s).start()

            def wait_completed(e, k, s):
                @pl.when(completed(e, k))
                def _():
                    group_copy(e, k, s).wait()

            tables.for_each_group(i, base, mark)
            compact()
            tables.for_each_group(i, base, settle)
            tables.for_each_group(i, base, start_completed)
            tables.for_each_group(i, base, wait_completed)
            return carry

        lax.fori_loop(0, n_rounds, one_round, 0)
        for e in range(N_EXPERTS):
            @pl.when(jnp.logical_not(has_tail(e)))
            def _():
                tails[e] = jnp.zeros((BF16_ROWS, D_MODEL), BF16)

    @pl.when(i == n - 1)
    def _():
        @pl.when((i >= 1) & tables.regular(jnp.maximum(i - 1, 0)))
        def _():
            tables.wait_rows(tables.total_rows(i - 1, full_ref), lambda m: send(1 - slot, 0, 0, 0, m))

        @pl.when(tables.regular(i))
        def _():
            tables.wait_rows(tables.total_rows(i, full_ref), lambda m: send(slot, 0, 0, 0, m))

        zeros[...] = jnp.zeros_like(zeros)
        cap_rows = xe_ref.shape[1]
        big = zeros.shape[0]

        def fill(e, row0, rows):
            return pltpu.make_async_copy(zeros.at[pl.ds(0, rows), :],
                                         xe_ref.at[e, pl.ds(row0, rows), :], sem_fill.at[0])

        for e in range(N_EXPERTS):
            used = start_ref[i * N_EXPERTS + e] + span_ref[i * N_EXPERTS + e]
            n_small = ((-used) & (big - 1)) // BF16_ROWS
            base = used + n_small * BF16_ROWS
            n_big = (cap_rows - base) // big

            def small_row(k):
                return pl.multiple_of(used + k * BF16_ROWS, BF16_ROWS)

            def big_row(k):
                return pl.multiple_of(base + k * big, big)

            lax.fori_loop(0, n_small, lambda k, c: (fill(e, small_row(k), BF16_ROWS).start(), c)[1], 0)
            lax.fori_loop(0, n_big, lambda k, c: (fill(e, big_row(k), big).start(), c)[1], 0)
            lax.fori_loop(0, n_small, lambda k, c: (fill(e, small_row(k), BF16_ROWS).wait(), c)[1], 0)
            lax.fori_loop(0, n_big, lambda k, c: (fill(e, big_row(k), big).wait(), c)[1], 0)


def _dispatch(tables, xn, lp, cap_rows):
    t = xn.shape[0]
    tm = MOE_TILE
    return pl.pallas_call(
        _dispatch_kernel,
        grid_spec=pltpu.PrefetchScalarGridSpec(
            num_scalar_prefetch=len(tables),
            grid=(t // tm,),
            in_specs=[
                pl.BlockSpec((tm, D_MODEL), lambda i, *_: (i, 0)),
                pl.BlockSpec((N_EXPERTS, tm), lambda i, *_: (0, i)),
            ],
            out_specs=pl.BlockSpec(memory_space=pl.ANY),
            scratch_shapes=[pltpu.VMEM((STACK_ROWS + RUN_ROWS, tm), BF16),
                            pltpu.VMEM((2, STACK_ROWS + BF16_ROWS, D_MODEL), BF16),
                            pltpu.VMEM((N_EXPERTS, BF16_ROWS, D_MODEL), BF16),
                            pltpu.VMEM((FILL_ROWS, D_MODEL), BF16),
                            pltpu.SemaphoreType.DMA((2,)),
                            pltpu.SemaphoreType.DMA((1,))],
        ),
        out_shape=jax.ShapeDtypeStruct((N_EXPERTS, cap_rows, D_MODEL), BF16),
        compiler_params=pltpu.CompilerParams(dimension_semantics=("arbitrary",)),
        name="dispatch",
    )(*tables, xn, lp)


def _ffn_kernel(used_ref, xe_ref, wg_ref, wu_ref, wd_ref, ye_ref, wg, wu, wd):
    e, j = pl.program_id(0), pl.program_id(1)
    bs = xe_ref.shape[1]
    n_valid = used_ref[e] - j * bs

    @pl.when(j == 0)
    def _():
        wg[...] = wg_ref[0].astype(BF16)
        wu[...] = wu_ref[0].astype(BF16)
        wd[...] = wd_ref[0].astype(BF16)

    @pl.when(n_valid > 0)
    def _():
        row = lax.broadcasted_iota(jnp.int32, (bs, D_MODEL), 0)
        x = jnp.where(row < n_valid, xe_ref[0], jnp.zeros((bs, D_MODEL), BF16))
        gate = jnp.dot(x, wg[...], preferred_element_type=F32)
        up = jnp.dot(x, wu[...], preferred_element_type=F32)
        hdn = (jax.nn.silu(gate) * up).astype(BF16)
        ye_ref[0] = jnp.dot(hdn, wd[...], preferred_element_type=F32).astype(BF16)

    @pl.when(n_valid <= 0)
    def _():
        ye_ref[0] = jnp.zeros((bs, D_MODEL), BF16)


def _ffn(used, xe, w_gate, w_up, w_down):
    cap_rows = xe.shape[1]
    bs = FFN_BLOCK

    def x_map(e, j, used_ref):
        last = jnp.maximum((used_ref[e] + bs - 1) // bs - 1, 0)
        return (e, jnp.minimum(j, last), 0)

    w_map = lambda e, j, used_ref: (e, 0, 0)
    return pl.pallas_call(
        _ffn_kernel,
        grid_spec=pltpu.PrefetchScalarGridSpec(
            num_scalar_prefetch=1,
            grid=(N_EXPERTS, cap_rows // bs),
            in_specs=[
                pl.BlockSpec((1, bs, D_MODEL), x_map),
                pl.BlockSpec((1, D_MODEL, D_MODEL), w_map),
                pl.BlockSpec((1, D_MODEL, D_MODEL), w_map),
                pl.BlockSpec((1, D_MODEL, D_MODEL), w_map),
            ],
            out_specs=pl.BlockSpec((1, bs, D_MODEL), lambda e, j, used_ref: (e, j, 0)),
            scratch_shapes=[pltpu.VMEM((D_MODEL, D_MODEL), BF16)] * 3,
        ),
        out_shape=jax.ShapeDtypeStruct((N_EXPERTS, cap_rows, D_MODEL), BF16),
        compiler_params=pltpu.CompilerParams(dimension_semantics=("arbitrary", "arbitrary"),
                                             vmem_limit_bytes=FFN_VMEM_BYTES),
        name="expert_ffn",
    )(used, xe, w_gate, w_up, w_down)


def _gate_matrix(lpt_ref, gate_ref, offv_ref, spanv_ref, leadv_ref, base):
    off = offv_ref[0][:, 0:1]
    end = off + spanv_ref[0][:, 0:1]
    first = off + leadv_ref[0][:, 0:1]
    row = lax.broadcasted_iota(jnp.int32, (N_EXPERTS, STACK_ROWS), 1).astype(F32) + base
    owner = (row >= off) & (row < end)
    owner_pad = jnp.concatenate(
        [jnp.where(owner, 1.0, 0.0), jnp.zeros((LANES - N_EXPERTS, STACK_ROWS), F32)], axis=0).astype(BF16)
    slot_of_row = row[0:1, :] - jnp.sum(jnp.where(owner, first, 0.0), axis=0, keepdims=True)
    slot_of_token = jnp.dot(lpt_ref[...].astype(BF16), owner_pad, preferred_element_type=F32)
    gate_of_token = jnp.dot(gate_ref[...].astype(BF16), owner_pad, preferred_element_type=F32)
    return jnp.where(slot_of_token == slot_of_row, gate_of_token, 0.0).astype(BF16)


def _combine_kernel(start_ref, off_ref, span_ref, full_ref, lead_ref, x1_ref, lpt_ref, gate_ref,
                    offv_ref, spanv_ref, leadv_ref, gfin_ref, ye_ref, y_ref,
                    stack, extra, sem, sem_extra):
    i = pl.program_id(0)
    n = pl.num_programs(0)
    slot = i % 2
    tables = _TileTables(start_ref, off_ref, span_ref, full_ref, lead_ref)

    def fetch(buf, e, row, src, n_rows):
        return pltpu.make_async_copy(ye_ref.at[e, pl.ds(src, n_rows), :],
                                     stack.at[buf, pl.ds(row, n_rows), :], sem.at[buf])

    @pl.when(i == 0)
    def _():
        stack[...] = jnp.zeros_like(stack)
        extra[...] = jnp.zeros_like(extra)

        @pl.when(tables.regular(0))
        def _():
            tables.for_each_piece(0, span_ref, lambda e, row, src, m: fetch(0, e, row, src, m).start())

    nxt = jnp.minimum(i + 1, n - 1)

    @pl.when((i + 1 < n) & tables.regular(nxt))
    def _():
        tables.for_each_piece(nxt, span_ref,
                              lambda e, row, src, m: fetch(1 - slot, e, row, src, m).start())

    regular = tables.regular(i)

    @pl.when(regular)
    def _():
        tables.wait_rows(tables.stack_rows(i), lambda m: fetch(slot, 0, 0, 0, m))

    moe = jnp.dot(_gate_matrix(lpt_ref, gate_ref, offv_ref, spanv_ref, leadv_ref, 0.0), stack[slot],
                  preferred_element_type=F32)
    moe = jnp.where(regular, moe, 0.0)

    n_rounds = jnp.where(regular, 0, (tables.stack_rows(i) + STACK_ROWS - 1) // STACK_ROWS)

    def one_round(r, acc):
        base = r * STACK_ROWS

        def fetch_group(e, k, s):
            return pltpu.make_async_copy(ye_ref.at[e, pl.ds(tables.slot(i, e, k), BF16_ROWS), :],
                                         extra.at[pl.ds(s, BF16_ROWS), :], sem_extra.at[0])

        tables.for_each_group(i, base, lambda e, k, s: fetch_group(e, k, s).start())
        tables.for_each_group(i, base, lambda e, k, s: fetch_group(e, k, s).wait())
        g = _gate_matrix(lpt_ref, gate_ref, offv_ref, spanv_ref, leadv_ref, base.astype(F32))
        return acc + jnp.dot(g, extra[...], preferred_element_type=F32)

    moe = lax.fori_loop(0, n_rounds, one_round, moe)
    y_ref[...] = _rms(x1_ref[...] + moe, gfin_ref[...])


def _combine(tables, x1, lpt, gate, off_vec, span_vec, lead_vec, g_final, ye):
    t = x1.shape[0]
    tm = MOE_TILE
    row = lambda i, *_: (i, 0)
    per_tile = pl.BlockSpec((1, N_EXPERTS, LANES), lambda i, *_: (i, 0, 0))
    return pl.pallas_call(
        _combine_kernel,
        grid_spec=pltpu.PrefetchScalarGridSpec(
            num_scalar_prefetch=len(tables),
            grid=(t // tm,),
            in_specs=[
                pl.BlockSpec((tm, D_MODEL), row),
                pl.BlockSpec((tm, LANES), row),
                pl.BlockSpec((tm, LANES), row),
                per_tile, per_tile, per_tile,
                pl.BlockSpec((1, D_MODEL), lambda i, *_: (0, 0)),
                pl.BlockSpec(memory_space=pl.ANY),
            ],
            out_specs=pl.BlockSpec((tm, D_MODEL), row),
            scratch_shapes=[pltpu.VMEM((2, STACK_ROWS, D_MODEL), BF16),
                            pltpu.VMEM((STACK_ROWS, D_MODEL), BF16),
                            pltpu.SemaphoreType.DMA((2,)),
                            pltpu.SemaphoreType.DMA((1,))],
        ),
        out_shape=jax.ShapeDtypeStruct((t, D_MODEL), F32),
        compiler_params=pltpu.CompilerParams(dimension_semantics=("arbitrary",)),
        name="combine",
    )(*tables, x1, lpt, gate, off_vec, span_vec, lead_vec, g_final, ye)


def _rope_tables(seq):
    half = HEAD_DIM // 2
    inv_freq = jnp.power(ROPE_THETA, -jnp.arange(half, dtype=F32) * 2.0 / HEAD_DIM)
    ang = jnp.arange(seq, dtype=F32)[:, None] * inv_freq[None, :]
    cos, sin = jnp.cos(ang), jnp.sin(ang)
    reps = LANES // HEAD_DIM
    return (jnp.tile(jnp.concatenate([cos, cos], axis=1), (1, reps)),
            jnp.tile(jnp.concatenate([-sin, sin], axis=1), (1, reps)))


def _prepare_weights(g_mix, w_in, ln_b_g, ln_b_b, w_spatial, b_spatial, g_mem, w_mem_kv,
                     g_out, w_out, g_ffn, w_router, w_gate, w_up, w_down, g_final):
    scale = jnp.ones((w_in.shape[-1],), F32)
    scale = scale.at[:D_A].set(HEAD_DIM ** -0.5 * LOG2_E)
    scale = scale.at[3 * D_A + 2 * D_B:].set((D_C // N_HEADS_C) ** -0.5)
    head_of_lane = jnp.arange(D_A) // HEAD_DIM
    return dict(
        g_mix=g_mix[0][None], w_in=(w_in[0] * scale).astype(BF16),
        ln_g=ln_b_g[0][None], ln_b=ln_b_b[0][None],
        w_sp=w_spatial[0].astype(BF16),
        b_sp=jnp.repeat(b_spatial[0].T, D_B // N_GROUPS_B, axis=1),
        g_mem=g_mem[0][None], w_kv=w_mem_kv[0].astype(BF16),
        expand=sum((jnp.arange(LANES)[:, None] == (_lse_lane(h) + dup))
                   & (head_of_lane[None, :] == h)
                   for h in range(N_HEADS_A) for dup in (0, N_HEADS_A)).astype(BF16),
        g_out=g_out[0][None], w_out=w_out[0].astype(BF16),
        g_ffn=g_ffn[0][None],
        w_router=jnp.pad(w_router[0], ((0, 0), (0, LANES - N_EXPERTS))).astype(BF16),
        w_gate=w_gate[0], w_up=w_up[0], w_down=w_down[0],
        g_final=g_final[None],
    )


def _encoder(x, mem, w):
    b, seq, _ = x.shape
    t = b * seq
    x2 = x.reshape(t, D_MODEL)
    cos_t, sin_t = _rope_tables(seq)
    mem_k, mem_v = _mem_kv(mem, w["g_mem"], w["w_kv"])
    q4, k4, v4, q16, k16, v16, u, vb, qc = _in_proj(x2, w["g_mix"], w["w_in"], cos_t, sin_t,
                                                    w["ln_g"], w["ln_b"], b, seq)
    pats = [_banded_attention(q4, k4, v4, DILATIONS[1], "attn_d1"),
            _banded_attention(q4, k4, v4, 1, "attn_d4"),
            _banded_attention(q16, k16, v16, 1, "attn_d16")]
    x1, xn, aff, afft = _mix(x2, [p[0] for p in pats], [p[1] for p in pats], u, vb, qc,
                             mem_k, mem_v, w["w_sp"], w["b_sp"], w["expand"], w["g_out"],
                             w["w_out"], w["g_ffn"], w["w_router"], seq)

    cap = EC_CAPACITY_FACTOR * t // N_EXPERTS
    thr_col, quota_col, thr_row, quota_row = _threshold(afft, cap)
    lp, lpt, gate, start, off, span, full, lead, total = _positions(
        afft, aff, thr_col, quota_col, thr_row, quota_row)
    tables = [a[:, :, 0].astype(jnp.int32).reshape(-1) for a in (start, off, span, full, lead)]
    used = total[:, 0].astype(jnp.int32)
    assert cap % BF16_ROWS == 0
    cap_rows = -(-cap // FFN_BLOCK) * FFN_BLOCK
    xe = _dispatch(tables, xn, lp, cap_rows)
    ye = _ffn(used, xe, w["w_gate"], w["w_up"], w["w_down"])
    y = _combine(tables, x1, lpt, gate, off, span, lead, w["g_final"], ye)
    return y.reshape(b, seq, D_MODEL)


def kernel(x_prompt, x_sample, mem_prompt, mem_sample, g_mix, w_in, ln_b_g, ln_b_b, w_spatial,
           b_spatial, g_mem, w_mem_kv, g_out, w_out, g_ffn, w_router, w_gate, w_up, w_down, g_final):
    w = _prepare_weights(g_mix, w_in, ln_b_g, ln_b_b, w_spatial, b_spatial, g_mem, w_mem_kv,
                         g_out, w_out, g_ffn, w_router, w_gate, w_up, w_down, g_final)
    return (_encoder(x_prompt, mem_prompt, w), _encoder(x_sample, mem_sample, w))
```

```python
import functools

import jax
import jax.numpy as jnp
from jax import lax
from jax.experimental import pallas as pl
from jax.experimental.pallas import tpu as pltpu

F32 = jnp.float32
BF16 = jnp.bfloat16

D_MODEL = 1024
N_HEADS_A = 8
HEAD_DIM = 64
D_A = 512
D_B = 256
N_GROUPS_B = 4
CHUNK_B = 128
D_C = 256
N_HEADS_C = 4
N_MEM = 256
N_EXPERTS = 16
EC_CAPACITY_FACTOR = 2
DILATIONS = (1, 4, 16)
RADIUS = 64
Q_BLOCK = 128
ROPE_THETA = 10000.0
EPS = 1e-6
NEG_INF = -1e30
LOG2_E = 1.4426950408889634

LANES = 128
BF16_ROWS = 16
TOKEN_TILE = 512
ATTN_TILE = 1024
MOE_TILE = 256
POSITION_TILES = 4
STACK_ROWS = 896
FILL_ROWS = 64
RUN_ROWS = 128
RUN_PIECES = (8, 4, 2, 1)
WAIT_PIECES = (32, 16, 8, 4, 2, 1)
FFN_BLOCK = 1024
FFN_VMEM_BYTES = 52 * 1024 * 1024
UNSELECTED = -4096.0
BINADE_STEPS = (64, 32, 16, 8, 4, 2, 1)
MANTISSA_STEPS = 52


def _rms(x, g):
    return x * lax.rsqrt(jnp.mean(x * x, axis=-1, keepdims=True) + EPS) * g


def _mem_kv_kernel(mem_ref, g_ref, w_ref, k_ref, v_ref):
    h = _rms(mem_ref[0], g_ref[...]).astype(BF16)
    kv = jnp.dot(h, w_ref[...], preferred_element_type=F32)
    k_ref[0] = kv[:, :D_C].astype(BF16)
    v_ref[0] = kv[:, D_C:].astype(BF16)


def _mem_kv(mem, g_mem, w_kv):
    b = mem.shape[0]
    return pl.pallas_call(
        _mem_kv_kernel,
        grid=(b,),
        in_specs=[
            pl.BlockSpec((1, N_MEM, D_MODEL), lambda i: (i, 0, 0)),
            pl.BlockSpec((1, D_MODEL), lambda i: (0, 0)),
            pl.BlockSpec((D_MODEL, 2 * D_C), lambda i: (0, 0)),
        ],
        out_specs=[
            pl.BlockSpec((1, N_MEM, D_C), lambda i: (i, 0, 0)),
            pl.BlockSpec((1, N_MEM, D_C), lambda i: (i, 0, 0)),
        ],
        out_shape=[jax.ShapeDtypeStruct((b, N_MEM, D_C), BF16)] * 2,
        name="mem_kv",
    )(mem, g_mem, w_kv)


def _in_proj_kernel(x_ref, g_ref, w_ref, cos_ref, sin_ref, lng_ref, lnb_ref,
                    q4_ref, k4_ref, v4_ref, q16_ref, k16_ref, v16_ref, u_ref, vb_ref, qc_ref,
                    chunks, chunks4):
    tm = x_ref.shape[0]
    d4 = DILATIONS[1]
    n_chunks = D_A // LANES

    def emit_by_residue(z, out4_ref, out16_ref):
        for c in range(n_chunks):
            chunks[c] = z[:, c * LANES:(c + 1) * LANES]
        quarter = tm // d4
        for r in range(d4):
            rows = [chunks[c, pl.ds(r, quarter, stride=d4), :] for c in range(n_chunks)]
            out4_ref[0, r] = jnp.concatenate(rows, axis=1).astype(BF16)
            for c in range(n_chunks):
                chunks4[c, r * quarter:(r + 1) * quarter, :] = rows[c]
        for r in range(d4):
            for a in range(d4):
                rows = [chunks4[c, pl.ds(r * quarter + a, quarter // d4, stride=d4), :]
                        for c in range(n_chunks)]
                out16_ref[0, r + d4 * a] = jnp.concatenate(rows, axis=1).astype(BF16)

    h = _rms(x_ref[...], g_ref[...]).astype(BF16)
    cos = jnp.concatenate([cos_ref[...]] * (D_A // LANES), axis=1)
    sin = jnp.concatenate([sin_ref[...]] * (D_A // LANES), axis=1)
    lane = lax.broadcasted_iota(jnp.int32, (tm, D_A), 1)
    first_half = (lane & (HEAD_DIM - 1)) < (HEAD_DIM // 2)

    def rope(z):
        rot = jnp.where(first_half, pltpu.roll(z, D_A - HEAD_DIM // 2, 1),
                        pltpu.roll(z, HEAD_DIM // 2, 1))
        return z * cos + rot * sin

    def proj(lo, hi):
        return jnp.dot(h, w_ref[:, lo:hi], preferred_element_type=F32)

    emit_by_residue(rope(proj(0, D_A)), q4_ref, q16_ref)
    emit_by_residue(rope(proj(D_A, 2 * D_A)), k4_ref, k16_ref)
    emit_by_residue(proj(2 * D_A, 3 * D_A), v4_ref, v16_ref)
    u_ref[...] = jax.nn.gelu(proj(3 * D_A, 3 * D_A + D_B)).astype(BF16)
    vb = jax.nn.gelu(proj(3 * D_A + D_B, 3 * D_A + 2 * D_B))
    mu = jnp.mean(vb, axis=-1, keepdims=True)
    var = jnp.mean(jnp.square(vb - mu), axis=-1, keepdims=True)
    vb_ref[...] = ((vb - mu) * lax.rsqrt(var + EPS) * lng_ref[...] + lnb_ref[...]).astype(BF16)
    qc_ref[...] = proj(3 * D_A + 2 * D_B, 3 * D_A + 2 * D_B + D_C).astype(BF16)


def _residue_spec(dil, tm, tiles_per_seq, width):
    return pl.BlockSpec((1, dil, tm // dil, width),
                        lambda i: (i // tiles_per_seq, 0, i % tiles_per_seq, 0))


def _in_proj(x2, g_mix, w_in, cos_t, sin_t, ln_g, ln_b, b, seq):
    t = x2.shape[0]
    tm = TOKEN_TILE
    d_in = w_in.shape[1]
    tiles_per_seq = seq // tm
    row = lambda i: (i, 0)
    const = lambda i: (0, 0)
    pos = lambda i: (i % tiles_per_seq, 0)
    d4, d16 = DILATIONS[1], DILATIONS[2]
    res4 = _residue_spec(d4, tm, tiles_per_seq, D_A)
    res16 = _residue_spec(d16, tm, tiles_per_seq, D_A)
    shape4 = jax.ShapeDtypeStruct((b, d4, seq // d4, D_A), BF16)
    shape16 = jax.ShapeDtypeStruct((b, d16, seq // d16, D_A), BF16)
    return pl.pallas_call(
        _in_proj_kernel,
        grid=(t // tm,),
        in_specs=[
            pl.BlockSpec((tm, D_MODEL), row),
            pl.BlockSpec((1, D_MODEL), const),
            pl.BlockSpec((D_MODEL, d_in), const),
            pl.BlockSpec((tm, LANES), pos),
            pl.BlockSpec((tm, LANES), pos),
            pl.BlockSpec((1, D_B), const),
            pl.BlockSpec((1, D_B), const),
        ],
        out_specs=[res4] * 3 + [res16] * 3 + [pl.BlockSpec((tm, w), row) for w in (D_B, D_B, D_C)],
        out_shape=[shape4] * 3 + [shape16] * 3
                  + [jax.ShapeDtypeStruct((t, w), BF16) for w in (D_B, D_B, D_C)],
        scratch_shapes=[pltpu.VMEM((D_A // LANES, tm, LANES), F32)] * 2,
        name="in_proj",
    )(x2, g_mix, w_in, cos_t, sin_t, ln_g, ln_b)


def _lse_lane(head):
    return head + HEAD_DIM * (1 - head % 2)


def _attn_kernel(q_ref, kp_ref, kc_ref, kn_ref, vp_ref, vc_ref, vn_ref,
                 o_ref, lse_ref, kwin, vwin, bias, scores, probs, dens, *, nres, tr, sub):
    j = pl.program_id(2)
    halo = RADIUS // nres
    qb = Q_BLOCK // nres
    for win, (p_ref, c_ref, n_ref) in ((kwin, (kp_ref, kc_ref, kn_ref)), (vwin, (vp_ref, vc_ref, vn_ref))):
        win[:, 0:halo] = p_ref[0]
        win[:, halo:halo + tr] = c_ref[0]
        win[:, halo + tr:] = n_ref[0]

    t_idx = lax.broadcasted_iota(jnp.int32, (Q_BLOCK, 2 * Q_BLOCK), 0)
    s_idx = lax.broadcasted_iota(jnp.int32, (Q_BLOCK, 2 * Q_BLOCK), 1)
    q_res, q_row = t_idx >> (qb.bit_length() - 1), t_idx & (qb - 1)
    k_res, k_row = s_idx >> qb.bit_length(), s_idx & (2 * qb - 1)
    band = jnp.abs(nres * (q_row - k_row + halo) + q_res - k_res) <= RADIUS
    k_elem = nres * (k_row - halo) + k_res
    lane = lax.broadcasted_iota(jnp.int32, (Q_BLOCK, LANES), 1)
    low_half = lane < HEAD_DIM

    def block(i, carry):
        r0 = pl.multiple_of(i * qb, qb)
        kpos = k_elem + nres * (j * tr + r0)
        bias[...] = jnp.where(band & (kpos >= 0) & (kpos < sub), 0.0, NEG_INF)
        stack = lambda pieces: pieces[0] if nres == 1 else jnp.concatenate(pieces, axis=0)
        pair_cols = lambda hp: slice(hp * LANES, (hp + 1) * LANES)

        for hp in range(N_HEADS_A // 2):
            q_pair = stack([q_ref[0, r, pl.ds(r0, qb), pair_cols(hp)] for r in range(nres)])
            k_pair = stack([kwin[r, pl.ds(r0, 2 * qb), pair_cols(hp)] for r in range(nres)])
            for half in range(2):
                keep = low_half if half == 0 else jnp.logical_not(low_half)
                qm = jnp.where(keep, q_pair, jnp.zeros_like(q_pair))
                scores[2 * hp + half] = lax.dot_general(qm, k_pair, (((1,), (1,)), ((), ())),
                                                        preferred_element_type=F32)

        lse_blk = jnp.zeros((Q_BLOCK, LANES), F32)
        for head in range(N_HEADS_A):
            s = scores[head] + bias[...]
            m = jnp.max(s, axis=-1, keepdims=True)
            p = jnp.exp2(s - m)
            den = jnp.sum(p, axis=-1, keepdims=True)
            probs[head] = p.astype(BF16)
            dens[head] = jnp.broadcast_to(den, (Q_BLOCK, LANES))
            hit = (lane == _lse_lane(head)) | (lane == _lse_lane(head) + N_HEADS_A)
            lse_blk = jnp.where(hit, m + jnp.log(den) * LOG2_E, lse_blk)
        for r in range(nres):
            lse_ref[0, r, pl.ds(r0, qb), :] = lse_blk[r * qb:(r + 1) * qb]

        for hp in range(N_HEADS_A // 2):
            v_pair = stack([vwin[r, pl.ds(r0, 2 * qb), pair_cols(hp)] for r in range(nres)])
            halves = [jnp.dot(probs[2 * hp + half], v_pair, preferred_element_type=F32)
                      / dens[2 * hp + half] for half in range(2)]
            out = jnp.where(low_half, halves[0], halves[1]).astype(BF16)
            for r in range(nres):
                o_ref[0, r, pl.ds(r0, qb), pair_cols(hp)] = out[r * qb:(r + 1) * qb]
        return carry

    lax.fori_loop(0, tr // qb, block, 0)


def _banded_attention(q, k, v, nres, name):
    b, n_res, rows, _ = q.shape
    tr = min(ATTN_TILE // nres, rows)
    halo = RADIUS // nres
    halo_per_tile = tr // halo
    last_halo = rows // halo - 1
    if nres == 1:
        grid = (b, n_res, rows // tr)
        at = lambda row_block: (lambda bi, r, j: (bi, r, row_block(j), 0))
    else:
        grid = (b, 1, rows // tr)
        at = lambda row_block: (lambda bi, r, j: (bi, 0, row_block(j), 0))
    cur = pl.BlockSpec((1, nres, tr, D_A), at(lambda j: j))
    prev = pl.BlockSpec((1, nres, halo, D_A), at(lambda j: jnp.maximum(j * halo_per_tile - 1, 0)))
    nxt = pl.BlockSpec((1, nres, halo, D_A),
                       at(lambda j: jnp.minimum((j + 1) * halo_per_tile, last_halo)))
    return pl.pallas_call(
        functools.partial(_attn_kernel, nres=nres, tr=tr, sub=nres * rows),
        grid=grid,
        in_specs=[cur, prev, cur, nxt, prev, cur, nxt],
        out_specs=[cur, pl.BlockSpec((1, nres, tr, LANES), at(lambda j: j))],
        out_shape=[
            jax.ShapeDtypeStruct((b, n_res, rows, D_A), BF16),
            jax.ShapeDtypeStruct((b, n_res, rows, LANES), F32),
        ],
        scratch_shapes=[pltpu.VMEM((nres, tr + 2 * halo, D_A), BF16)] * 2
                       + [pltpu.VMEM((Q_BLOCK, 2 * Q_BLOCK), F32),
                          pltpu.VMEM((N_HEADS_A, Q_BLOCK, 2 * Q_BLOCK), F32),
                          pltpu.VMEM((N_HEADS_A, Q_BLOCK, 2 * Q_BLOCK), BF16),
                          pltpu.VMEM((N_HEADS_A, Q_BLOCK, LANES), F32)],
        name=name,
    )(q, k, k, k, v, v, v)


def _split_bf16(x):
    hi = x.astype(BF16)
    lo = (x - hi.astype(F32)).astype(BF16)
    return hi, lo


def _mix_kernel(x_ref, o1_ref, o2_ref, o3_ref, l1_ref, l2_ref, l3_ref, u_ref, vb_ref, qc_ref,
                mk_ref, mv_ref, ws_ref, bs_ref, expand_ref, go_ref, wo_ref, gf_ref, wr_ref,
                x1_ref, xn_ref, aff_ref, afft_ref, obuf1, obuf2, obuf3, lbuf1, lbuf2, lbuf3):
    tm = x_ref.shape[0]

    def token_order(ref, buf):
        dil, width = ref.shape[1], ref.shape[3]
        for r in range(dil):
            piece = ref[0, r].astype(F32)
            for c in range(width // LANES):
                buf[c, pl.ds(r, tm // dil, stride=dil), :] = piece[:, c * LANES:(c + 1) * LANES]
        return jnp.concatenate([buf[c] for c in range(width // LANES)], axis=1)

    l1, l2, l3 = token_order(l1_ref, lbuf1), token_order(l2_ref, lbuf2), token_order(l3_ref, lbuf3)
    mx = jnp.maximum(jnp.maximum(l1, l2), l3)
    e1, e2, e3 = jnp.exp2(l1 - mx), jnp.exp2(l2 - mx), jnp.exp2(l3 - mx)
    tot = e1 + e2 + e3
    is_hi_lane = (lax.broadcasted_iota(jnp.int32, (tm, LANES), 1) & N_HEADS_A) == 0
    o_a = jnp.zeros((tm, D_A), F32)
    for e, o_ref, buf in ((e1, o1_ref, obuf1), (e2, o2_ref, obuf2), (e3, o3_ref, obuf3)):
        hi, lo = _split_bf16(e / tot)
        w = jnp.dot(jnp.where(is_hi_lane, hi, lo), expand_ref[...], preferred_element_type=F32)
        o_a = o_a + w * token_order(o_ref, buf)

    lane_b = lax.broadcasted_iota(jnp.int32, (CHUNK_B, D_B), 1)
    group_w = D_B // N_GROUPS_B
    gated = []
    for c in range(tm // CHUNK_B):
        vchunk = vb_ref[c * CHUNK_B:(c + 1) * CHUNK_B, :]
        acc = bs_ref[...]
        for g in range(N_GROUPS_B):
            y = jnp.dot(ws_ref[g], vchunk, preferred_element_type=F32)
            acc = acc + jnp.where(lane_b // group_w == g, y, 0.0)
        gated.append(acc)
    o_b = u_ref[...].astype(F32) * jnp.concatenate(gated, axis=0)

    lane_c = lax.broadcasted_iota(jnp.int32, (tm, LANES), 1)
    low_half = lane_c < HEAD_DIM
    oc_parts = []
    for hp in range(N_HEADS_C // 2):
        cols = slice(hp * LANES, (hp + 1) * LANES)
        q_pair = qc_ref[:, cols]
        k_pair = mk_ref[0, :, cols]
        v_pair = mv_ref[0, :, cols]
        halves = []
        for half in range(2):
            keep = low_half if half == 0 else jnp.logical_not(low_half)
            qm = jnp.where(keep, q_pair, jnp.zeros_like(q_pair))
            s = lax.dot_general(qm, k_pair, (((1,), (1,)), ((), ())), preferred_element_type=F32)
            m = jnp.max(s, axis=-1, keepdims=True)
            p = jnp.exp(s - m)
            p = p / jnp.sum(p, axis=-1, keepdims=True)
            halves.append(jnp.dot(p.astype(BF16), v_pair, preferred_element_type=F32))
        oc_parts.append(jnp.where(low_half, halves[0], halves[1]))
    o_c = jnp.concatenate(oc_parts, axis=1)

    go = go_ref[...]
    o = jnp.concatenate([_rms(o_a, go[:, :D_A]), _rms(o_b, go[:, D_A:D_A + D_B]),
                         _rms(o_c, go[:, D_A + D_B:])], axis=1).astype(BF16)
    x1 = x_ref[...] + jnp.dot(o, wo_ref[...], preferred_element_type=F32)
    x1_ref[...] = x1

    xn = _rms(x1, gf_ref[...]).astype(BF16)
    xn_ref[...] = xn
    logits = jnp.dot(xn, wr_ref[...], preferred_element_type=F32)
    is_expert = lane_c < N_EXPERTS
    logits = jnp.where(is_expert, logits, NEG_INF)
    m = jnp.max(logits, axis=-1, keepdims=True)
    ex = jnp.where(is_expert, jnp.exp(logits - m), 0.0)
    aff = ex / jnp.sum(ex, axis=-1, keepdims=True)
    aff_ref[...] = aff
    afft_ref[...] = jnp.transpose(aff)[:N_EXPERTS, :]


def _mix(x2, o_pats, lse_pats, u, vb, qc, mem_k, mem_v, w_sp, b_sp, expand, g_out, w_out,
         g_ffn, w_router, seq):
    t = x2.shape[0]
    tm = TOKEN_TILE
    tiles_per_seq = seq // tm
    row = lambda i: (i, 0)
    const2 = lambda i: (0, 0)
    const3 = lambda i: (0, 0, 0)
    batch = lambda i: (i // tiles_per_seq, 0, 0)
    dils = [o.shape[1] for o in o_pats]
    return pl.pallas_call(
        _mix_kernel,
        grid=(t // tm,),
        in_specs=[
            pl.BlockSpec((tm, D_MODEL), row),
            *[_residue_spec(d, tm, tiles_per_seq, D_A) for d in dils],
            *[_residue_spec(d, tm, tiles_per_seq, LANES) for d in dils],
            pl.BlockSpec((tm, D_B), row), pl.BlockSpec((tm, D_B), row), pl.BlockSpec((tm, D_C), row),
            pl.BlockSpec((1, N_MEM, D_C), batch), pl.BlockSpec((1, N_MEM, D_C), batch),
            pl.BlockSpec((N_GROUPS_B, CHUNK_B, CHUNK_B), const3),
            pl.BlockSpec((CHUNK_B, D_B), const2),
            pl.BlockSpec((LANES, D_A), const2),
            pl.BlockSpec((1, D_MODEL), const2),
            pl.BlockSpec((D_MODEL, D_MODEL), const2),
            pl.BlockSpec((1, D_MODEL), const2),
            pl.BlockSpec((D_MODEL, LANES), const2),
        ],
        out_specs=[
            pl.BlockSpec((tm, D_MODEL), row),
            pl.BlockSpec((tm, D_MODEL), row),
            pl.BlockSpec((tm, LANES), row),
            pl.BlockSpec((N_EXPERTS, tm), lambda i: (0, i)),
        ],
        out_shape=[
            jax.ShapeDtypeStruct((t, D_MODEL), F32),
            jax.ShapeDtypeStruct((t, D_MODEL), BF16),
            jax.ShapeDtypeStruct((t, LANES), F32),
            jax.ShapeDtypeStruct((N_EXPERTS, t), F32),
        ],
        scratch_shapes=[pltpu.VMEM((D_A // LANES, tm, LANES), F32)] * 3
                       + [pltpu.VMEM((1, tm, LANES), F32)] * 3,
        name="mix",
    )(x2, *o_pats, *lse_pats, u, vb, qc, mem_k, mem_v, w_sp, b_sp, expand, g_out, w_out,
      g_ffn, w_router)


def _threshold_kernel(afft_ref, thr_col_ref, quota_col_ref, thr_row_ref, quota_row_ref, *, cap):
    aff = afft_ref[...]

    def enough(v):
        return jnp.sum(jnp.where(aff >= v, 1.0, 0.0), axis=-1, keepdims=True) >= cap

    hi = jnp.full((N_EXPERTS, 1), 2.0, F32)
    for shift in BINADE_STEPS:
        cand = hi * (2.0 ** -shift)
        hi = jnp.where(enough(cand), hi, cand)
    lo = jnp.where(enough(hi * 0.5), hi * 0.5, 0.0)

    def bisect(_, bracket):
        lo, hi = bracket
        mid = (lo + hi) * 0.5
        ok = enough(mid)
        return jnp.where(ok, mid, lo), jnp.where(ok, hi, mid)

    lo, hi = lax.fori_loop(0, MANTISSA_STEPS, bisect, (lo, hi))
    thr_f = jnp.min(jnp.where(aff >= lo, aff, jnp.inf), axis=-1, keepdims=True)
    n_gt = jnp.sum(jnp.where(aff > thr_f, 1.0, 0.0), axis=-1, keepdims=True)
    quota = cap - n_gt
    thr_col = jnp.broadcast_to(thr_f, (N_EXPERTS, LANES))
    quota_col = jnp.broadcast_to(quota, (N_EXPERTS, LANES))
    thr_col_ref[...] = thr_col
    quota_col_ref[...] = quota_col
    diag = (lax.broadcasted_iota(jnp.int32, (N_EXPERTS, LANES), 0)
            == lax.broadcasted_iota(jnp.int32, (N_EXPERTS, LANES), 1))
    thr_row = jnp.sum(jnp.where(diag, thr_col, 0.0), axis=0, keepdims=True)
    quota_row = jnp.sum(jnp.where(diag, quota_col, 0.0), axis=0, keepdims=True)
    thr_row_ref[...] = jnp.broadcast_to(thr_row, (8, LANES))
    quota_row_ref[...] = jnp.broadcast_to(quota_row, (8, LANES))


def _threshold(afft, cap):
    t = afft.shape[1]
    full = lambda shape: pl.BlockSpec(shape, lambda: (0,) * len(shape))
    return pl.pallas_call(
        functools.partial(_threshold_kernel, cap=float(cap)),
        in_specs=[full((N_EXPERTS, t))],
        out_specs=[full((N_EXPERTS, LANES)), full((N_EXPERTS, LANES)),
                   full((8, LANES)), full((8, LANES))],
        out_shape=[jax.ShapeDtypeStruct((N_EXPERTS, LANES), F32)] * 2
                  + [jax.ShapeDtypeStruct((8, LANES), F32)] * 2,
        name="threshold",
    )(afft)


def _positions_kernel(afft_ref, aff_ref, thr_col_ref, quota_col_ref, thr_row_ref, quota_row_ref,
                      lp_ref, lpt_ref, gate_ref, start_ref, off_ref, span_ref, full_ref, lead_ref,
                      total_ref, eq_col, eq_row, slot_col, slot_row):
    i = pl.program_id(0)
    tm = MOE_TILE

    @pl.when(i == 0)
    def _():
        eq_col[...] = jnp.zeros_like(eq_col)
        eq_row[...] = jnp.zeros_like(eq_row)
        slot_col[...] = jnp.zeros_like(slot_col)
        slot_row[...] = jnp.zeros_like(slot_row)

    r = lax.broadcasted_iota(jnp.int32, (tm, tm), 0)
    c = lax.broadcasted_iota(jnp.int32, (tm, tm), 1)
    before = jnp.where(r < c, 1.0, 0.0).astype(BF16)
    after = jnp.where(c < r, 1.0, 0.0).astype(BF16)
    thr, quota = thr_col_ref[:, 0:1], quota_col_ref[:, 0:1]
    thr_r, quota_r = thr_row_ref[0:1, :], quota_row_ref[0:1, :]
    is_expert = lax.broadcasted_iota(jnp.int32, (tm, LANES), 1) < N_EXPERTS
    earlier_expert = (lax.broadcasted_iota(jnp.int32, (N_EXPERTS, LANES), 1)
                      < lax.broadcasted_iota(jnp.int32, (N_EXPERTS, LANES), 0))
    tile_floor = lambda n: jnp.floor(n / BF16_ROWS) * BF16_ROWS
    tile_ceil = lambda n: jnp.ceil(n / BF16_ROWS) * BF16_ROWS

    eq_seen_col, eq_seen_row = eq_col[:, 0:1], eq_row[0:1, :]
    first_slot, first_slot_row = slot_col[...], slot_row[0:1, :]
    for s in range(aff_ref.shape[0] // tm):
        a = afft_ref[:, s * tm:(s + 1) * tm]
        eq = a == thr
        eq_f = jnp.where(eq, 1.0, 0.0)
        eq_rank = jnp.dot(eq_f.astype(BF16), before, preferred_element_type=F32) + eq_seen_col
        sel = (a > thr) | (eq & (eq_rank < quota))
        sel_f = jnp.where(sel, 1.0, 0.0)
        lp = jnp.dot(sel_f.astype(BF16), before, preferred_element_type=F32)
        lp_ref[:, s * tm:(s + 1) * tm] = jnp.where(sel, lp, UNSELECTED)
        count = jnp.sum(sel_f, axis=-1, keepdims=True)
        start = tile_floor(first_slot)
        lead = first_slot - start
        start_ref[s] = start
        lead_ref[s] = lead
        span_ref[s] = tile_ceil(lead + count)
        full_ref[s] = tile_floor(lead + count)
        first_slot = first_slot + count
        eq_seen_col = eq_seen_col + jnp.sum(eq_f, axis=-1, keepdims=True)

        at = aff_ref[s * tm:(s + 1) * tm, :]
        eq_t = (at == thr_r) & is_expert
        eq_tf = jnp.where(eq_t, 1.0, 0.0)
        eq_rank_t = jnp.dot(after, eq_tf.astype(BF16), preferred_element_type=F32) + eq_seen_row
        sel_t = ((at > thr_r) & is_expert) | (eq_t & (eq_rank_t < quota_r))
        sel_tf = jnp.where(sel_t, 1.0, 0.0)
        lp_t = jnp.dot(after, sel_tf.astype(BF16), preferred_element_type=F32)
        lpt_ref[s * tm:(s + 1) * tm, :] = jnp.where(sel_t, lp_t, UNSELECTED)
        gate_ref[s * tm:(s + 1) * tm, :] = jnp.where(sel_t, at, 0.0)
        eq_seen_row = eq_seen_row + jnp.sum(eq_tf, axis=0, keepdims=True)
        count_row = jnp.sum(sel_tf, axis=0, keepdims=True)
        span_row = tile_ceil(first_slot_row - tile_floor(first_slot_row) + count_row)
        off = jnp.sum(jnp.where(earlier_expert, span_row, 0.0), axis=-1, keepdims=True)
        off_ref[s] = jnp.broadcast_to(off, (N_EXPERTS, LANES))
        first_slot_row = first_slot_row + count_row

    eq_col[...] = jnp.broadcast_to(eq_seen_col, eq_col.shape)
    eq_row[...] = jnp.broadcast_to(eq_seen_row, eq_row.shape)
    slot_col[...] = first_slot
    slot_row[...] = jnp.broadcast_to(first_slot_row, slot_row.shape)
    total_ref[...] = first_slot


def _positions(afft, aff, thr_col, quota_col, thr_row, quota_row):
    t = aff.shape[0]
    tm = MOE_TILE * POSITION_TILES
    n_tiles = t // MOE_TILE
    c2 = lambda i: (0, 0)
    per_tile = pl.BlockSpec((POSITION_TILES, N_EXPERTS, LANES), lambda i: (i, 0, 0))
    per_tile_shape = jax.ShapeDtypeStruct((n_tiles, N_EXPERTS, LANES), F32)
    return pl.pallas_call(
        _positions_kernel,
        grid=(t // tm,),
        in_specs=[
            pl.BlockSpec((N_EXPERTS, tm), lambda i: (0, i)),
            pl.BlockSpec((tm, LANES), lambda i: (i, 0)),
            pl.BlockSpec((N_EXPERTS, LANES), c2), pl.BlockSpec((N_EXPERTS, LANES), c2),
            pl.BlockSpec((8, LANES), c2), pl.BlockSpec((8, LANES), c2),
        ],
        out_specs=[
            pl.BlockSpec((N_EXPERTS, tm), lambda i: (0, i)),
            pl.BlockSpec((tm, LANES), lambda i: (i, 0)),
            pl.BlockSpec((tm, LANES), lambda i: (i, 0)),
            per_tile, per_tile, per_tile, per_tile, per_tile,
            pl.BlockSpec((N_EXPERTS, LANES), c2),
        ],
        out_shape=[
            jax.ShapeDtypeStruct((N_EXPERTS, t), F32),
            jax.ShapeDtypeStruct((t, LANES), F32),
            jax.ShapeDtypeStruct((t, LANES), F32),
            per_tile_shape,
            per_tile_shape,
            per_tile_shape,
            per_tile_shape,
            per_tile_shape,
            jax.ShapeDtypeStruct((N_EXPERTS, LANES), F32),
        ],
        scratch_shapes=[pltpu.VMEM((N_EXPERTS, LANES), F32), pltpu.VMEM((8, LANES), F32),
                        pltpu.VMEM((N_EXPERTS, LANES), F32), pltpu.VMEM((8, LANES), F32)],
        compiler_params=pltpu.CompilerParams(dimension_semantics=("arbitrary",)),
        name="positions",
    )(afft, aff, thr_col, quota_col, thr_row, quota_row)


class _TileTables:
    def __init__(self, start_ref, off_ref, span_ref, full_ref, lead_ref):
        self.start_ref, self.off_ref, self.span_ref = start_ref, off_ref, span_ref
        self.full_ref, self.lead_ref = full_ref, lead_ref

    def slot(self, tile, e, k):
        return pl.multiple_of(self.start_ref[tile * N_EXPERTS + e] + k * BF16_ROWS, BF16_ROWS)

    def stack_rows(self, tile):
        last = tile * N_EXPERTS + N_EXPERTS - 1
        return self.off_ref[last] + self.span_ref[last]

    def fits(self, tile):
        return self.stack_rows(tile) <= STACK_ROWS

    def regular(self, tile):
        longest = self.span_ref[tile * N_EXPERTS]
        for e in range(1, N_EXPERTS):
            longest = jnp.maximum(longest, self.span_ref[tile * N_EXPERTS + e])
        return self.fits(tile) & (longest <= RUN_ROWS)

    def total_rows(self, tile, rows_ref):
        total = rows_ref[tile * N_EXPERTS]
        for e in range(1, N_EXPERTS):
            total = total + rows_ref[tile * N_EXPERTS + e]
        return total

    @staticmethod
    def wait_rows(n_rows, make_copy):
        n_groups = n_rows // BF16_ROWS
        for groups in WAIT_PIECES:
            @pl.when((n_groups & groups) != 0)
            def _(groups=groups):
                make_copy(groups * BF16_ROWS).wait()

    def for_each_piece(self, tile, rows_ref, fn):
        for e in range(N_EXPERTS):
            n_groups = rows_ref[tile * N_EXPERTS + e] // BF16_ROWS
            row = self.off_ref[tile * N_EXPERTS + e]
            slot = self.start_ref[tile * N_EXPERTS + e]
            for groups in RUN_PIECES:
                take = (n_groups & groups) != 0
                n_rows = groups * BF16_ROWS

                @pl.when(take)
                def _(e=e, row=row, slot=slot, n_rows=n_rows):
                    fn(e, pl.multiple_of(row, BF16_ROWS), pl.multiple_of(slot, BF16_ROWS), n_rows)

                step = jnp.where(take, n_rows, 0)
                row, slot = row + step, slot + step

    def for_each_group(self, tile, base, fn):
        for e in range(N_EXPERTS):
            off = self.off_ref[tile * N_EXPERTS + e]
            n_groups = self.span_ref[tile * N_EXPERTS + e] // BF16_ROWS

            def body(k, carry, e=e, off=off):
                s = off + k * BF16_ROWS - base

                @pl.when((s >= 0) & (s < STACK_ROWS))
                def _():
                    fn(e, k, pl.multiple_of(s, BF16_ROWS))
                return carry

            lax.fori_loop(0, n_groups, body, 0)


def _dispatch_kernel(start_ref, off_ref, span_ref, full_ref, lead_ref, xn_ref, lp_ref, xe_ref,
                     onehot, stack, tails, zeros, sem, sem_fill):
    i = pl.program_id(0)
    n = pl.num_programs(0)
    slot = i % 2
    tm = xn_ref.shape[0]
    tables = _TileTables(start_ref, off_ref, span_ref, full_ref, lead_ref)

    def send(buf, e, row, dst, n_rows):
        return pltpu.make_async_copy(stack.at[buf, pl.ds(row, n_rows), :],
                                     xe_ref.at[e, pl.ds(dst, n_rows), :], sem.at[buf])

    def compact():
        stack[slot, 0:STACK_ROWS] = jnp.dot(onehot[0:STACK_ROWS, :], xn_ref[...],
                                            preferred_element_type=F32).astype(BF16)

    def lead_of(e):
        return lead_ref[i * N_EXPERTS + e].astype(F32)

    def has_tail(e):
        return span_ref[i * N_EXPERTS + e] > full_ref[i * N_EXPERTS + e]

    def join_head(e, row):
        stack[slot, pl.ds(row, BF16_ROWS), :] = stack[slot, pl.ds(row, BF16_ROWS), :] + tails[e]

    def keep_tail(e, row):
        tails[e] = stack[slot, pl.ds(row, BF16_ROWS), :]

    @pl.when(i == 0)
    def _():
        onehot[...] = jnp.zeros_like(onehot)
        stack[...] = jnp.zeros_like(stack)
        tails[...] = jnp.zeros_like(tails)

    @pl.when((i >= 2) & tables.regular(jnp.maximum(i - 2, 0)))
    def _():
        tables.wait_rows(tables.total_rows(i - 2, full_ref), lambda m: send(slot, 0, 0, 0, m))

    regular = tables.regular(i)

    @pl.when(regular)
    def _():
        run_iota = lax.broadcasted_iota(jnp.int32, (RUN_ROWS, tm), 0).astype(F32)
        for e in range(N_EXPERTS):
            row = pl.multiple_of(off_ref[i * N_EXPERTS + e], BF16_ROWS)
            hit = lp_ref[e:e + 1, :] == run_iota - lead_of(e)
            onehot[pl.ds(row, RUN_ROWS), :] = jnp.where(hit, 1.0, 0.0).astype(BF16)
        compact()
        for e in range(N_EXPERTS):
            row = pl.multiple_of(off_ref[i * N_EXPERTS + e], BF16_ROWS)
            join_head(e, row)
            tail_row = pl.multiple_of(row + full_ref[i * N_EXPERTS + e], BF16_ROWS)
            tails[e] = jnp.where(has_tail(e), stack[slot, pl.ds(tail_row, BF16_ROWS), :],
                                 jnp.zeros((BF16_ROWS, D_MODEL), BF16))
        tables.for_each_piece(i, full_ref, lambda e, row, dst, m: send(slot, e, row, dst, m).start())

    @pl.when(jnp.logical_not(regular))
    def _():
        group_iota = lax.broadcasted_iota(jnp.int32, (BF16_ROWS, tm), 0).astype(F32)
        n_rounds = (tables.stack_rows(i) + STACK_ROWS - 1) // STACK_ROWS

        def one_round(r, carry):
            base = r * STACK_ROWS
            onehot[...] = jnp.zeros_like(onehot)

            def mark(e, k, s):
                hit = lp_ref[e:e + 1, :] == group_iota + (k * BF16_ROWS).astype(F32) - lead_of(e)
                onehot[pl.ds(s, BF16_ROWS), :] = jnp.where(hit, 1.0, 0.0).astype(BF16)

            def settle(e, k, s):
                @pl.when(k == 0)
                def _():
                    join_head(e, s)

                @pl.when(k * BF16_ROWS == full_ref[i * N_EXPERTS + e])
                def _():
                    keep_tail(e, s)

            def completed(e, k):
                return k * BF16_ROWS < full_ref[i * N_EXPERTS + e]

            def group_copy(e, k, s):
                return send(slot, e, s, tables.slot(i, e, k), BF16_ROWS)

            def start_completed(e, k, s):
                @pl.when(completed(e, k))
                def _():
                    group_copy(e, k, s).start()

            def wait_completed(e, k, s):
                @pl.when(completed(e, k))
                def _():
                    group_copy(e, k, s).wait()

            tables.for_each_group(i, base, mark)
            compact()
            tables.for_each_group(i, base, settle)
            tables.for_each_group(i, base, start_completed)
            tables.for_each_group(i, base, wait_completed)
            return carry

        lax.fori_loop(0, n_rounds, one_round, 0)
        for e in range(N_EXPERTS):
            @pl.when(jnp.logical_not(has_tail(e)))
            def _():
                tails[e] = jnp.zeros((BF16_ROWS, D_MODEL), BF16)

    @pl.when(i == n - 1)
    def _():
        @pl.when((i >= 1) & tables.regular(jnp.maximum(i - 1, 0)))
        def _():
            tables.wait_rows(tables.total_rows(i - 1, full_ref), lambda m: send(1 - slot, 0, 0, 0, m))

        @pl.when(regular)
        def _():
            tables.wait_rows(tables.total_rows(i, full_ref), lambda m: send(slot, 0, 0, 0, m))

        zeros[...] = jnp.zeros_like(zeros)
        cap_rows = xe_ref.shape[1]
        big = zeros.shape[0]

        def fill(e, row0, rows):
            return pltpu.make_async_copy(zeros.at[pl.ds(0, rows), :],
                                         xe_ref.at[e, pl.ds(row0, rows), :], sem_fill.at[0])

        for e in range(N_EXPERTS):
            used = start_ref[i * N_EXPERTS + e] + span_ref[i * N_EXPERTS + e]
            n_small = ((-used) & (big - 1)) // BF16_ROWS
            base = used + n_small * BF16_ROWS
            n_big = (cap_rows - base) // big

            def small_row(k):
                return pl.multiple_of(used + k * BF16_ROWS, BF16_ROWS)

            def big_row(k):
                return pl.multiple_of(base + k * big, big)

            lax.fori_loop(0, n_small, lambda k, c: (fill(e, small_row(k), BF16_ROWS).start(), c)[1], 0)
            lax.fori_loop(0, n_big, lambda k, c: (fill(e, big_row(k), big).start(), c)[1], 0)
            lax.fori_loop(0, n_small, lambda k, c: (fill(e, small_row(k), BF16_ROWS).wait(), c)[1], 0)
            lax.fori_loop(0, n_big, lambda k, c: (fill(e, big_row(k), big).wait(), c)[1], 0)


def _dispatch(tables, xn, lp, cap_rows):
    t = xn.shape[0]
    tm = MOE_TILE
    return pl.pallas_call(
        _dispatch_kernel,
        grid_spec=pltpu.PrefetchScalarGridSpec(
            num_scalar_prefetch=len(tables),
            grid=(t // tm,),
            in_specs=[
                pl.BlockSpec((tm, D_MODEL), lambda i, *_: (i, 0)),
                pl.BlockSpec((N_EXPERTS, tm), lambda i, *_: (0, i)),
            ],
            out_specs=pl.BlockSpec(memory_space=pl.ANY),
            scratch_shapes=[pltpu.VMEM((STACK_ROWS + RUN_ROWS, tm), BF16),
                            pltpu.VMEM((2, STACK_ROWS + BF16_ROWS, D_MODEL), BF16),
                            pltpu.VMEM((N_EXPERTS, BF16_ROWS, D_MODEL), BF16),
                            pltpu.VMEM((FILL_ROWS, D_MODEL), BF16),
                            pltpu.SemaphoreType.DMA((2,)),
                            pltpu.SemaphoreType.DMA((1,))],
        ),
        out_shape=jax.ShapeDtypeStruct((N_EXPERTS, cap_rows, D_MODEL), BF16),
        compiler_params=pltpu.CompilerParams(dimension_semantics=("arbitrary",)),
        name="dispatch",
    )(*tables, xn, lp)


def _ffn_kernel(used_ref, xe_ref, wg_ref, wu_ref, wd_ref, ye_ref, wg, wu, wd):
    e, j = pl.program_id(0), pl.program_id(1)
    bs = xe_ref.shape[1]
    n_valid = used_ref[e] - j * bs

    @pl.when(j == 0)
    def _():
        wg[...] = wg_ref[0].astype(BF16)
        wu[...] = wu_ref[0].astype(BF16)
        wd[...] = wd_ref[0].astype(BF16)

    @pl.when(n_valid > 0)
    def _():
        row = lax.broadcasted_iota(jnp.int32, (bs, D_MODEL), 0)
        x = jnp.where(row < n_valid, xe_ref[0], jnp.zeros((bs, D_MODEL), BF16))
        gate = jnp.dot(x, wg[...], preferred_element_type=F32)
        up = jnp.dot(x, wu[...], preferred_element_type=F32)
        hdn = (jax.nn.silu(gate) * up).astype(BF16)
        ye_ref[0] = jnp.dot(hdn, wd[...], preferred_element_type=F32).astype(BF16)

    @pl.when(n_valid <= 0)
    def _():
        ye_ref[0] = jnp.zeros((bs, D_MODEL), BF16)


def _ffn(used, xe, w_gate, w_up, w_down):
    cap_rows = xe.shape[1]
    bs = FFN_BLOCK

    def x_map(e, j, used_ref):
        last = jnp.maximum((used_ref[e] + bs - 1) // bs - 1, 0)
        return (e, jnp.minimum(j, last), 0)

    w_map = lambda e, j, used_ref: (e, 0, 0)
    return pl.pallas_call(
        _ffn_kernel,
        grid_spec=pltpu.PrefetchScalarGridSpec(
            num_scalar_prefetch=1,
            grid=(N_EXPERTS, cap_rows // bs),
            in_specs=[
                pl.BlockSpec((1, bs, D_MODEL), x_map),
                pl.BlockSpec((1, D_MODEL, D_MODEL), w_map),
                pl.BlockSpec((1, D_MODEL, D_MODEL), w_map),
                pl.BlockSpec((1, D_MODEL, D_MODEL), w_map),
            ],
            out_specs=pl.BlockSpec((1, bs, D_MODEL), lambda e, j, used_ref: (e, j, 0)),
            scratch_shapes=[pltpu.VMEM((D_MODEL, D_MODEL), BF16)] * 3,
        ),
        out_shape=jax.ShapeDtypeStruct((N_EXPERTS, cap_rows, D_MODEL), BF16),
        compiler_params=pltpu.CompilerParams(dimension_semantics=("arbitrary", "arbitrary"),
                                             vmem_limit_bytes=FFN_VMEM_BYTES),
        name="expert_ffn",
    )(used, xe, w_gate, w_up, w_down)


def _gate_matrix(lpt_ref, gate_ref, offv_ref, spanv_ref, leadv_ref, base):
    off = offv_ref[0][:, 0:1]
    end = off + spanv_ref[0][:, 0:1]
    first = off + leadv_ref[0][:, 0:1]
    row = lax.broadcasted_iota(jnp.int32, (N_EXPERTS, STACK_ROWS), 1).astype(F32) + base
    owner = (row >= off) & (row < end)
    owner_pad = jnp.concatenate(
        [jnp.where(owner, 1.0, 0.0), jnp.zeros((LANES - N_EXPERTS, STACK_ROWS), F32)], axis=0).astype(BF16)
    slot_of_row = row[0:1, :] - jnp.sum(jnp.where(owner, first, 0.0), axis=0, keepdims=True)
    slot_of_token = jnp.dot(lpt_ref[...].astype(BF16), owner_pad, preferred_element_type=F32)
    gate_of_token = jnp.dot(gate_ref[...].astype(BF16), owner_pad, preferred_element_type=F32)
    return jnp.where(slot_of_token == slot_of_row, gate_of_token, 0.0).astype(BF16)


def _combine_kernel(start_ref, off_ref, span_ref, full_ref, lead_ref, x1_ref, lpt_ref, gate_ref,
                    offv_ref, spanv_ref, leadv_ref, gfin_ref, ye_ref, y_ref,
                    stack, extra, sem, sem_extra):
    i = pl.program_id(0)
    n = pl.num_programs(0)
    slot = i % 2
    tables = _TileTables(start_ref, off_ref, span_ref, full_ref, lead_ref)

    def fetch(buf, e, row, src, n_rows):
        return pltpu.make_async_copy(ye_ref.at[e, pl.ds(src, n_rows), :],
                                     stack.at[buf, pl.ds(row, n_rows), :], sem.at[buf])

    @pl.when(i == 0)
    def _():
        stack[...] = jnp.zeros_like(stack)
        extra[...] = jnp.zeros_like(extra)

        @pl.when(tables.regular(0))
        def _():
            tables.for_each_piece(0, span_ref, lambda e, row, src, m: fetch(0, e, row, src, m).start())

    nxt = jnp.minimum(i + 1, n - 1)

    @pl.when((i + 1 < n) & tables.regular(nxt))
    def _():
        tables.for_each_piece(nxt, span_ref,
                              lambda e, row, src, m: fetch(1 - slot, e, row, src, m).start())

    regular = tables.regular(i)

    @pl.when(regular)
    def _():
        tables.wait_rows(tables.stack_rows(i), lambda m: fetch(slot, 0, 0, 0, m))

    moe = jnp.dot(_gate_matrix(lpt_ref, gate_ref, offv_ref, spanv_ref, leadv_ref, 0.0), stack[slot],
                  preferred_element_type=F32)
    moe = jnp.where(regular, moe, 0.0)

    n_rounds = jnp.where(regular, 0, (tables.stack_rows(i) + STACK_ROWS - 1) // STACK_ROWS)

    def one_round(r, acc):
        base = r * STACK_ROWS

        def fetch_group(e, k, s):
            return pltpu.make_async_copy(ye_ref.at[e, pl.ds(tables.slot(i, e, k), BF16_ROWS), :],
                                         extra.at[pl.ds(s, BF16_ROWS), :], sem_extra.at[0])

        tables.for_each_group(i, base, lambda e, k, s: fetch_group(e, k, s).start())
        tables.for_each_group(i, base, lambda e, k, s: fetch_group(e, k, s).wait())
        g = _gate_matrix(lpt_ref, gate_ref, offv_ref, spanv_ref, leadv_ref, base.astype(F32))
        return acc + jnp.dot(g, extra[...], preferred_element_type=F32)

    moe = lax.fori_loop(0, n_rounds, one_round, moe)
    y_ref[...] = _rms(x1_ref[...] + moe, gfin_ref[...])


def _combine(tables, x1, lpt, gate, off_vec, span_vec, lead_vec, g_final, ye):
    t = x1.shape[0]
    tm = MOE_TILE
    row = lambda i, *_: (i, 0)
    per_tile = pl.BlockSpec((1, N_EXPERTS, LANES), lambda i, *_: (i, 0, 0))
    return pl.pallas_call(
        _combine_kernel,
        grid_spec=pltpu.PrefetchScalarGridSpec(
            num_scalar_prefetch=len(tables),
            grid=(t // tm,),
            in_specs=[
                pl.BlockSpec((tm, D_MODEL), row),
                pl.BlockSpec((tm, LANES), row),
                pl.BlockSpec((tm, LANES), row),
                per_tile, per_tile, per_tile,
                pl.BlockSpec((1, D_MODEL), lambda i, *_: (0, 0)),
                pl.BlockSpec(memory_space=pl.ANY),
            ],
            out_specs=pl.BlockSpec((tm, D_MODEL), row),
            scratch_shapes=[pltpu.VMEM((2, STACK_ROWS, D_MODEL), BF16),
                            pltpu.VMEM((STACK_ROWS, D_MODEL), BF16),
                            pltpu.SemaphoreType.DMA((2,)),
                            pltpu.SemaphoreType.DMA((1,))],
        ),
        out_shape=jax.ShapeDtypeStruct((t, D_MODEL), F32),
        compiler_params=pltpu.CompilerParams(dimension_semantics=("arbitrary",)),
        name="combine",
    )(*tables, x1, lpt, gate, off_vec, span_vec, lead_vec, g_final, ye)


def _rope_tables(seq):
    half = HEAD_DIM // 2
    inv_freq = jnp.power(ROPE_THETA, -jnp.arange(half, dtype=F32) * 2.0 / HEAD_DIM)
    ang = jnp.arange(seq, dtype=F32)[:, None] * inv_freq[None, :]
    cos, sin = jnp.cos(ang), jnp.sin(ang)
    reps = LANES // HEAD_DIM
    return (jnp.tile(jnp.concatenate([cos, cos], axis=1), (1, reps)),
            jnp.tile(jnp.concatenate([-sin, sin], axis=1), (1, reps)))


def _prepare_weights(g_mix, w_in, ln_b_g, ln_b_b, w_spatial, b_spatial, g_mem, w_mem_kv,
                     g_out, w_out, g_ffn, w_router, w_gate, w_up, w_down, g_final):
    scale = jnp.ones((w_in.shape[-1],), F32)
    scale = scale.at[:D_A].set(HEAD_DIM ** -0.5 * LOG2_E)
    scale = scale.at[3 * D_A + 2 * D_B:].set((D_C // N_HEADS_C) ** -0.5)
    head_of_lane = jnp.arange(D_A) // HEAD_DIM
    return dict(
        g_mix=g_mix[0][None], w_in=(w_in[0] * scale).astype(BF16),
        ln_g=ln_b_g[0][None], ln_b=ln_b_b[0][None],
        w_sp=w_spatial[0].astype(BF16),
        b_sp=jnp.repeat(b_spatial[0].T, D_B // N_GROUPS_B, axis=1),
        g_mem=g_mem[0][None], w_kv=w_mem_kv[0].astype(BF16),
        expand=sum((jnp.arange(LANES)[:, None] == (_lse_lane(h) + dup))
                   & (head_of_lane[None, :] == h)
                   for h in range(N_HEADS_A) for dup in (0, N_HEADS_A)).astype(BF16),
        g_out=g_out[0][None], w_out=w_out[0].astype(BF16),
        g_ffn=g_ffn[0][None],
        w_router=jnp.pad(w_router[0], ((0, 0), (0, LANES - N_EXPERTS))).astype(BF16),
        w_gate=w_gate[0], w_up=w_up[0], w_down=w_down[0],
        g_final=g_final[None],
    )


def _encoder(x, mem, w):
    b, seq, _ = x.shape
    t = b * seq
    x2 = x.reshape(t, D_MODEL)
    cos_t, sin_t = _rope_tables(seq)
    mem_k, mem_v = _mem_kv(mem, w["g_mem"], w["w_kv"])
    q4, k4, v4, q16, k16, v16, u, vb, qc = _in_proj(x2, w["g_mix"], w["w_in"], cos_t, sin_t,
                                                    w["ln_g"], w["ln_b"], b, seq)
    pats = [_banded_attention(q4, k4, v4, DILATIONS[1], "attn_d1"),
            _banded_attention(q4, k4, v4, 1, "attn_d4"),
            _banded_attention(q16, k16, v16, 1, "attn_d16")]
    x1, xn, aff, afft = _mix(x2, [p[0] for p in pats], [p[1] for p in pats], u, vb, qc,
                             mem_k, mem_v, w["w_sp"], w["b_sp"], w["expand"], w["g_out"],
                             w["w_out"], w["g_ffn"], w["w_router"], seq)

    cap = EC_CAPACITY_FACTOR * t // N_EXPERTS
    thr_col, quota_col, thr_row, quota_row = _threshold(afft, cap)
    lp, lpt, gate, start, off, span, full, lead, total = _positions(
        afft, aff, thr_col, quota_col, thr_row, quota_row)
    tables = [a[:, :, 0].astype(jnp.int32).reshape(-1) for a in (start, off, span, full, lead)]
    used = total[:, 0].astype(jnp.int32)
    assert cap % BF16_ROWS == 0
    cap_rows = -(-cap // FFN_BLOCK) * FFN_BLOCK
    xe = _dispatch(tables, xn, lp, cap_rows)
    ye = _ffn(used, xe, w["w_gate"], w["w_up"], w["w_down"])
    y = _combine(tables, x1, lpt, gate, off, span, lead, w["g_final"], ye)
    return y.reshape(b, seq, D_MODEL)


def kernel(x_prompt, x_sample, mem_prompt, mem_sample, g_mix, w_in, ln_b_g, ln_b_b, w_spatial,
           b_spatial, g_mem, w_mem_kv, g_out, w_out, g_ffn, w_router, w_gate, w_up, w_down, g_final):
    w = _prepare_weights(g_mix, w_in, ln_b_g, ln_b_b, w_spatial, b_spatial, g_mem, w_mem_kv,
                         g_out, w_out, g_ffn, w_router, w_gate, w_up, w_down, g_final)
    return (_encoder(x_prompt, mem_prompt, w), _encoder(x_sample, mem_sample, w))
```

```python
import functools

import jax
import jax.numpy as jnp
from jax import lax
from jax.experimental import pallas as pl
from jax.experimental.pallas import tpu as pltpu

F32 = jnp.float32
BF16 = jnp.bfloat16

D_MODEL = 1024
N_HEADS_A = 8
HEAD_DIM = 64
D_A = 512
D_B = 256
N_GROUPS_B = 4
CHUNK_B = 128
D_C = 256
N_HEADS_C = 4
N_MEM = 256
N_EXPERTS = 16
EC_CAPACITY_FACTOR = 2
DILATIONS = (1, 4, 16)
RADIUS = 64
Q_BLOCK = 128
ROPE_THETA = 10000.0
EPS = 1e-6
NEG_INF = -1e30
LOG2_E = 1.4426950408889634

LANES = 128
BF16_ROWS = 16
TOKEN_TILE = 512
ATTN_TILE = 1024
MOE_TILE = 256
POSITION_TILES = 4
STACK_ROWS = 896
FILL_ROWS = 64
RUN_ROWS = 128
RUN_PIECES = (8, 4, 2, 1)
WAIT_PIECES = (32, 16, 8, 4, 2, 1)
FFN_BLOCK = 1024
FFN_VMEM_BYTES = 52 * 1024 * 1024
UNSELECTED = -4096.0
BINADE_STEPS = (64, 32, 16, 8, 4, 2, 1)
MANTISSA_STEPS = 52


def _rms(x, g):
    return x * lax.rsqrt(jnp.mean(x * x, axis=-1, keepdims=True) + EPS) * g


def _mem_kv_kernel(mem_ref, g_ref, w_ref, k_ref, v_ref):
    h = _rms(mem_ref[0], g_ref[...]).astype(BF16)
    kv = jnp.dot(h, w_ref[...], preferred_element_type=F32)
    k_ref[0] = kv[:, :D_C].astype(BF16)
    v_ref[0] = kv[:, D_C:].astype(BF16)


def _mem_kv(mem, g_mem, w_kv):
    b = mem.shape[0]
    return pl.pallas_call(
        _mem_kv_kernel,
        grid=(b,),
        in_specs=[
            pl.BlockSpec((1, N_MEM, D_MODEL), lambda i: (i, 0, 0)),
            pl.BlockSpec((1, D_MODEL), lambda i: (0, 0)),
            pl.BlockSpec((D_MODEL, 2 * D_C), lambda i: (0, 0)),
        ],
        out_specs=[
            pl.BlockSpec((1, N_MEM, D_C), lambda i: (i, 0, 0)),
            pl.BlockSpec((1, N_MEM, D_C), lambda i: (i, 0, 0)),
        ],
        out_shape=[jax.ShapeDtypeStruct((b, N_MEM, D_C), BF16)] * 2,
        name="mem_kv",
    )(mem, g_mem, w_kv)


def _in_proj_kernel(x_ref, g_ref, w_ref, cos_ref, sin_ref, lng_ref, lnb_ref,
                    q4_ref, k4_ref, v4_ref, q16_ref, k16_ref, v16_ref, u_ref, vb_ref, qc_ref,
                    chunks, chunks4):
    tm = x_ref.shape[0]
    d4 = DILATIONS[1]
    n_chunks = D_A // LANES

    def emit_by_residue(z, out4_ref, out16_ref):
        for c in range(n_chunks):
            chunks[c] = z[:, c * LANES:(c + 1) * LANES]
        quarter = tm // d4
        for r in range(d4):
            rows = [chunks[c, pl.ds(r, quarter, stride=d4), :] for c in range(n_chunks)]
            out4_ref[0, r] = jnp.concatenate(rows, axis=1).astype(BF16)
            for c in range(n_chunks):
                chunks4[c, r * quarter:(r + 1) * quarter, :] = rows[c]
        for r in range(d4):
            for a in range(d4):
                rows = [chunks4[c, pl.ds(r * quarter + a, quarter // d4, stride=d4), :]
                        for c in range(n_chunks)]
                out16_ref[0, r + d4 * a] = jnp.concatenate(rows, axis=1).astype(BF16)

    h = _rms(x_ref[...], g_ref[...]).astype(BF16)
    cos = jnp.concatenate([cos_ref[...]] * (D_A // LANES), axis=1)
    sin = jnp.concatenate([sin_ref[...]] * (D_A // LANES), axis=1)
    lane = lax.broadcasted_iota(jnp.int32, (tm, D_A), 1)
    first_half = (lane & (HEAD_DIM - 1)) < (HEAD_DIM // 2)

    def rope(z):
        rot = jnp.where(first_half, pltpu.roll(z, D_A - HEAD_DIM // 2, 1),
                        pltpu.roll(z, HEAD_DIM // 2, 1))
        return z * cos + rot * sin

    def proj(lo, hi):
        return jnp.dot(h, w_ref[:, lo:hi], preferred_element_type=F32)

    emit_by_residue(rope(proj(0, D_A)), q4_ref, q16_ref)
    emit_by_residue(rope(proj(D_A, 2 * D_A)), k4_ref, k16_ref)
    emit_by_residue(proj(2 * D_A, 3 * D_A), v4_ref, v16_ref)
    u_ref[...] = jax.nn.gelu(proj(3 * D_A, 3 * D_A + D_B)).astype(BF16)
    vb = jax.nn.gelu(proj(3 * D_A + D_B, 3 * D_A + 2 * D_B))
    mu = jnp.mean(vb, axis=-1, keepdims=True)
    var = jnp.mean(jnp.square(vb - mu), axis=-1, keepdims=True)
    vb_ref[...] = ((vb - mu) * lax.rsqrt(var + EPS) * lng_ref[...] + lnb_ref[...]).astype(BF16)
    qc_ref[...] = proj(3 * D_A + 2 * D_B, 3 * D_A + 2 * D_B + D_C).astype(BF16)


def _residue_spec(dil, tm, tiles_per_seq, width):
    return pl.BlockSpec((1, dil, tm // dil, width),
                        lambda i: (i // tiles_per_seq, 0, i % tiles_per_seq, 0))


def _in_proj(x2, g_mix, w_in, cos_t, sin_t, ln_g, ln_b, b, seq):
    t = x2.shape[0]
    tm = TOKEN_TILE
    d_in = w_in.shape[1]
    tiles_per_seq = seq // tm
    row = lambda i: (i, 0)
    const = lambda i: (0, 0)
    pos = lambda i: (i % tiles_per_seq, 0)
    d4, d16 = DILATIONS[1], DILATIONS[2]
    res4 = _residue_spec(d4, tm, tiles_per_seq, D_A)
    res16 = _residue_spec(d16, tm, tiles_per_seq, D_A)
    shape4 = jax.ShapeDtypeStruct((b, d4, seq // d4, D_A), BF16)
    shape16 = jax.ShapeDtypeStruct((b, d16, seq // d16, D_A), BF16)
    return pl.pallas_call(
        _in_proj_kernel,
        grid=(t // tm,),
        in_specs=[
            pl.BlockSpec((tm, D_MODEL), row),
            pl.BlockSpec((1, D_MODEL), const),
            pl.BlockSpec((D_MODEL, d_in), const),
            pl.BlockSpec((tm, LANES), pos),
            pl.BlockSpec((tm, LANES), pos),
            pl.BlockSpec((1, D_B), const),
            pl.BlockSpec((1, D_B), const),
        ],
        out_specs=[res4] * 3 + [res16] * 3 + [pl.BlockSpec((tm, w), row) for w in (D_B, D_B, D_C)],
        out_shape=[shape4] * 3 + [shape16] * 3
                  + [jax.ShapeDtypeStruct((t, w), BF16) for w in (D_B, D_B, D_C)],
        scratch_shapes=[pltpu.VMEM((D_A // LANES, tm, LANES), F32)] * 2,
        name="in_proj",
    )(x2, g_mix, w_in, cos_t, sin_t, ln_g, ln_b)


def _lse_lane(head):
    return head + HEAD_DIM * (1 - head % 2)


def _attn_kernel(q_ref, kp_ref, kc_ref, kn_ref, vp_ref, vc_ref, vn_ref,
                 o_ref, lse_ref, kwin, vwin, bias, scores, probs, dens, *, nres, tr, sub):
    j = pl.program_id(2)
    halo = RADIUS // nres
    qb = Q_BLOCK // nres
    for win, (p_ref, c_ref, n_ref) in ((kwin, (kp_ref, kc_ref, kn_ref)), (vwin, (vp_ref, vc_ref, vn_ref))):
        win[:, 0:halo] = p_ref[0]
        win[:, halo:halo + tr] = c_ref[0]
        win[:, halo + tr:] = n_ref[0]

    t_idx = lax.broadcasted_iota(jnp.int32, (Q_BLOCK, 2 * Q_BLOCK), 0)
    s_idx = lax.broadcasted_iota(jnp.int32, (Q_BLOCK, 2 * Q_BLOCK), 1)
    q_res, q_row = t_idx >> (qb.bit_length() - 1), t_idx & (qb - 1)
    k_res, k_row = s_idx >> qb.bit_length(), s_idx & (2 * qb - 1)
    band = jnp.abs(nres * (q_row - k_row + halo) + q_res - k_res) <= RADIUS
    k_elem = nres * (k_row - halo) + k_res
    lane = lax.broadcasted_iota(jnp.int32, (Q_BLOCK, LANES), 1)
    low_half = lane < HEAD_DIM

    def block(i, carry):
        r0 = pl.multiple_of(i * qb, qb)
        kpos = k_elem + nres * (j * tr + r0)
        bias[...] = jnp.where(band & (kpos >= 0) & (kpos < sub), 0.0, NEG_INF)
        stack = lambda pieces: pieces[0] if nres == 1 else jnp.concatenate(pieces, axis=0)
        pair_cols = lambda hp: slice(hp * LANES, (hp + 1) * LANES)

        for hp in range(N_HEADS_A // 2):
            q_pair = stack([q_ref[0, r, pl.ds(r0, qb), pair_cols(hp)] for r in range(nres)])
            k_pair = stack([kwin[r, pl.ds(r0, 2 * qb), pair_cols(hp)] for r in range(nres)])
            for half in range(2):
                keep = low_half if half == 0 else jnp.logical_not(low_half)
                qm = jnp.where(keep, q_pair, jnp.zeros_like(q_pair))
                scores[2 * hp + half] = lax.dot_general(qm, k_pair, (((1,), (1,)), ((), ())),
                                                        preferred_element_type=F32)

        lse_blk = jnp.zeros((Q_BLOCK, LANES), F32)
        for head in range(N_HEADS_A):
            s = scores[head] + bias[...]
            m = jnp.max(s, axis=-1, keepdims=True)
            p = jnp.exp2(s - m)
            den = jnp.sum(p, axis=-1, keepdims=True)
            probs[head] = p.astype(BF16)
            dens[head] = jnp.broadcast_to(den, (Q_BLOCK, LANES))
            hit = (lane == _lse_lane(head)) | (lane == _lse_lane(head) + N_HEADS_A)
            lse_blk = jnp.where(hit, m + jnp.log(den) * LOG2_E, lse_blk)
        for r in range(nres):
            lse_ref[0, r, pl.ds(r0, qb), :] = lse_blk[r * qb:(r + 1) * qb]

        for hp in range(N_HEADS_A // 2):
            v_pair = stack([vwin[r, pl.ds(r0, 2 * qb), pair_cols(hp)] for r in range(nres)])
            halves = [jnp.dot(probs[2 * hp + half], v_pair, preferred_element_type=F32)
                      / dens[2 * hp + half] for half in range(2)]
            out = jnp.where(low_half, halves[0], halves[1]).astype(BF16)
            for r in range(nres):
                o_ref[0, r, pl.ds(r0, qb), pair_cols(hp)] = out[r * qb:(r + 1) * qb]
        return carry

    lax.fori_loop(0, tr // qb, block, 0)


def _banded_attention(q, k, v, nres, name):
    b, n_res, rows, _ = q.shape
    tr = min(ATTN_TILE // nres, rows)
    halo = RADIUS // nres
    halo_per_tile = tr // halo
    last_halo = rows // halo - 1
    if nres == 1:
        grid = (b, n_res, rows // tr)
        at = lambda row_block: (lambda bi, r, j: (bi, r, row_block(j), 0))
    else:
        grid = (b, 1, rows // tr)
        at = lambda row_block: (lambda bi, r, j: (bi, 0, row_block(j), 0))
    cur = pl.BlockSpec((1, nres, tr, D_A), at(lambda j: j))
    prev = pl.BlockSpec((1, nres, halo, D_A), at(lambda j: jnp.maximum(j * halo_per_tile - 1, 0)))
    nxt = pl.BlockSpec((1, nres, halo, D_A),
                       at(lambda j: jnp.minimum((j + 1) * halo_per_tile, last_halo)))
    return pl.pallas_call(
        functools.partial(_attn_kernel, nres=nres, tr=tr, sub=nres * rows),
        grid=grid,
        in_specs=[cur, prev, cur, nxt, prev, cur, nxt],
        out_specs=[cur, pl.BlockSpec((1, nres, tr, LANES), at(lambda j: j))],
        out_shape=[
            jax.ShapeDtypeStruct((b, n_res, rows, D_A), BF16),
            jax.ShapeDtypeStruct((b, n_res, rows, LANES), F32),
        ],
        scratch_shapes=[pltpu.VMEM((nres, tr + 2 * halo, D_A), BF16)] * 2
                       + [pltpu.VMEM((Q_BLOCK, 2 * Q_BLOCK), F32),
                          pltpu.VMEM((N_HEADS_A, Q_BLOCK, 2 * Q_BLOCK), F32),
                          pltpu.VMEM((N_HEADS_A, Q_BLOCK, 2 * Q_BLOCK), BF16),
                          pltpu.VMEM((N_HEADS_A, Q_BLOCK, LANES), F32)],
        name=name,
    )(q, k, k, k, v, v, v)


def _split_bf16(x):
    hi = x.astype(BF16)
    lo = (x - hi.astype(F32)).astype(BF16)
    return hi, lo


def _mix_kernel(x_ref, o1_ref, o2_ref, o3_ref, l1_ref, l2_ref, l3_ref, u_ref, vb_ref, qc_ref,
                mk_ref, mv_ref, ws_ref, bs_ref, expand_ref, go_ref, wo_ref, gf_ref, wr_ref,
                x1_ref, xn_ref, aff_ref, afft_ref, obuf1, obuf2, obuf3, lbuf1, lbuf2, lbuf3):
    tm = x_ref.shape[0]

    def token_order(ref, buf):
        dil, width = ref.shape[1], ref.shape[3]
        for r in range(dil):
            piece = ref[0, r].astype(F32)
            for c in range(width // LANES):
                buf[c, pl.ds(r, tm // dil, stride=dil), :] = piece[:, c * LANES:(c + 1) * LANES]
        return jnp.concatenate([buf[c] for c in range(width // LANES)], axis=1)

    l1, l2, l3 = token_order(l1_ref, lbuf1), token_order(l2_ref, lbuf2), token_order(l3_ref, lbuf3)
    mx = jnp.maximum(jnp.maximum(l1, l2), l3)
    e1, e2, e3 = jnp.exp2(l1 - mx), jnp.exp2(l2 - mx), jnp.exp2(l3 - mx)
    tot = e1 + e2 + e3
    is_hi_lane = (lax.broadcasted_iota(jnp.int32, (tm, LANES), 1) & N_HEADS_A) == 0
    o_a = jnp.zeros((tm, D_A), F32)
    for e, o_ref, buf in ((e1, o1_ref, obuf1), (e2, o2_ref, obuf2), (e3, o3_ref, obuf3)):
        hi, lo = _split_bf16(e / tot)
        w = jnp.dot(jnp.where(is_hi_lane, hi, lo), expand_ref[...], preferred_element_type=F32)
        o_a = o_a + w * token_order(o_ref, buf)

    lane_b = lax.broadcasted_iota(jnp.int32, (CHUNK_B, D_B), 1)
    group_w = D_B // N_GROUPS_B
    gated = []
    for c in range(tm // CHUNK_B):
        vchunk = vb_ref[c * CHUNK_B:(c + 1) * CHUNK_B, :]
        acc = bs_ref[...]
        for g in range(N_GROUPS_B):
            y = jnp.dot(ws_ref[g], vchunk, preferred_element_type=F32)
            acc = acc + jnp.where(lane_b // group_w == g, y, 0.0)
        gated.append(acc)
    o_b = u_ref[...].astype(F32) * jnp.concatenate(gated, axis=0)

    lane_c = lax.broadcasted_iota(jnp.int32, (tm, LANES), 1)
    low_half = lane_c < HEAD_DIM
    oc_parts = []
    for hp in range(N_HEADS_C // 2):
        cols = slice(hp * LANES, (hp + 1) * LANES)
        q_pair = qc_ref[:, cols]
        k_pair = mk_ref[0, :, cols]
        v_pair = mv_ref[0, :, cols]
        halves = []
        for half in range(2):
            keep = low_half if half == 0 else jnp.logical_not(low_half)
            qm = jnp.where(keep, q_pair, jnp.zeros_like(q_pair))
            s = lax.dot_general(qm, k_pair, (((1,), (1,)), ((), ())), preferred_element_type=F32)
            m = jnp.max(s, axis=-1, keepdims=True)
            p = jnp.exp(s - m)
            p = p / jnp.sum(p, axis=-1, keepdims=True)
            halves.append(jnp.dot(p.astype(BF16), v_pair, preferred_element_type=F32))
        oc_parts.append(jnp.where(low_half, halves[0], halves[1]))
    o_c = jnp.concatenate(oc_parts, axis=1)

    go = go_ref[...]
    o = jnp.concatenate([_rms(o_a, go[:, :D_A]), _rms(o_b, go[:, D_A:D_A + D_B]),
                         _rms(o_c, go[:, D_A + D_B:])], axis=1).astype(BF16)
    x1 = x_ref[...] + jnp.dot(o, wo_ref[...], preferred_element_type=F32)
    x1_ref[...] = x1

    xn = _rms(x1, gf_ref[...]).astype(BF16)
    xn_ref[...] = xn
    logits = jnp.dot(xn, wr_ref[...], preferred_element_type=F32)
    is_expert = lane_c < N_EXPERTS
    logits = jnp.where(is_expert, logits, NEG_INF)
    m = jnp.max(logits, axis=-1, keepdims=True)
    ex = jnp.where(is_expert, jnp.exp(logits - m), 0.0)
    aff = ex / jnp.sum(ex, axis=-1, keepdims=True)
    aff_ref[...] = aff
    afft_ref[...] = jnp.transpose(aff)[:N_EXPERTS, :]


def _mix(x2, o_pats, lse_pats, u, vb, qc, mem_k, mem_v, w_sp, b_sp, expand, g_out, w_out,
         g_ffn, w_router, seq):
    t = x2.shape[0]
    tm = TOKEN_TILE
    tiles_per_seq = seq // tm
    row = lambda i: (i, 0)
    const2 = lambda i: (0, 0)
    const3 = lambda i: (0, 0, 0)
    batch = lambda i: (i // tiles_per_seq, 0, 0)
    dils = [o.shape[1] for o in o_pats]
    return pl.pallas_call(
        _mix_kernel,
        grid=(t // tm,),
        in_specs=[
            pl.BlockSpec((tm, D_MODEL), row),
            *[_residue_spec(d, tm, tiles_per_seq, D_A) for d in dils],
            *[_residue_spec(d, tm, tiles_per_seq, LANES) for d in dils],
            pl.BlockSpec((tm, D_B), row), pl.BlockSpec((tm, D_B), row), pl.BlockSpec((tm, D_C), row),
            pl.BlockSpec((1, N_MEM, D_C), batch), pl.BlockSpec((1, N_MEM, D_C), batch),
            pl.BlockSpec((N_GROUPS_B, CHUNK_B, CHUNK_B), const3),
            pl.BlockSpec((CHUNK_B, D_B), const2),
            pl.BlockSpec((LANES, D_A), const2),
            pl.BlockSpec((1, D_MODEL), const2),
            pl.BlockSpec((D_MODEL, D_MODEL), const2),
            pl.BlockSpec((1, D_MODEL), const2),
            pl.BlockSpec((D_MODEL, LANES), const2),
        ],
        out_specs=[
            pl.BlockSpec((tm, D_MODEL), row),
            pl.BlockSpec((tm, D_MODEL), row),
            pl.BlockSpec((tm, LANES), row),
            pl.BlockSpec((N_EXPERTS, tm), lambda i: (0, i)),
        ],
        out_shape=[
            jax.ShapeDtypeStruct((t, D_MODEL), F32),
            jax.ShapeDtypeStruct((t, D_MODEL), BF16),
            jax.ShapeDtypeStruct((t, LANES), F32),
            jax.ShapeDtypeStruct((N_EXPERTS, t), F32),
        ],
        scratch_shapes=[pltpu.VMEM((D_A // LANES, tm, LANES), F32)] * 3
                       + [pltpu.VMEM((1, tm, LANES), F32)] * 3,
        name="mix",
    )(x2, *o_pats, *lse_pats, u, vb, qc, mem_k, mem_v, w_sp, b_sp, expand, g_out, w_out,
      g_ffn, w_router)


def _threshold_kernel(afft_ref, thr_col_ref, quota_col_ref, thr_row_ref, quota_row_ref, *, cap):
    aff = afft_ref[...]

    def enough(v):
        return jnp.sum(jnp.where(aff >= v, 1.0, 0.0), axis=-1, keepdims=True) >= cap

    hi = jnp.full((N_EXPERTS, 1), 2.0, F32)
    for shift in BINADE_STEPS:
        cand = hi * (2.0 ** -shift)
        hi = jnp.where(enough(cand), hi, cand)
    lo = jnp.where(enough(hi * 0.5), hi * 0.5, 0.0)

    def bisect(_, bracket):
        lo, hi = bracket
        mid = (lo + hi) * 0.5
        ok = enough(mid)
        return jnp.where(ok, mid, lo), jnp.where(ok, hi, mid)

    lo, hi = lax.fori_loop(0, MANTISSA_STEPS, bisect, (lo, hi))
    thr_f = jnp.min(jnp.where(aff >= lo, aff, jnp.inf), axis=-1, keepdims=True)
    n_gt = jnp.sum(jnp.where(aff > thr_f, 1.0, 0.0), axis=-1, keepdims=True)
    quota = cap - n_gt
    thr_col = jnp.broadcast_to(thr_f, (N_EXPERTS, LANES))
    quota_col = jnp.broadcast_to(quota, (N_EXPERTS, LANES))
    thr_col_ref[...] = thr_col
    quota_col_ref[...] = quota_col
    diag = (lax.broadcasted_iota(jnp.int32, (N_EXPERTS, LANES), 0)
            == lax.broadcasted_iota(jnp.int32, (N_EXPERTS, LANES), 1))
    thr_row = jnp.sum(jnp.where(diag, thr_col, 0.0), axis=0, keepdims=True)
    quota_row = jnp.sum(jnp.where(diag, quota_col, 0.0), axis=0, keepdims=True)
    thr_row_ref[...] = jnp.broadcast_to(thr_row, (8, LANES))
    quota_row_ref[...] = jnp.broadcast_to(quota_row, (8, LANES))


def _threshold(afft, cap):
    t = afft.shape[1]
    full = lambda shape: pl.BlockSpec(shape, lambda: (0,) * len(shape))
    return pl.pallas_call(
        functools.partial(_threshold_kernel, cap=float(cap)),
        in_specs=[full((N_EXPERTS, t))],
        out_specs=[full((N_EXPERTS, LANES)), full((N_EXPERTS, LANES)),
                   full((8, LANES)), full((8, LANES))],
        out_shape=[jax.ShapeDtypeStruct((N_EXPERTS, LANES), F32)] * 2
                  + [jax.ShapeDtypeStruct((8, LANES), F32)] * 2,
        name="threshold",
    )(afft)


def _positions_kernel(afft_ref, aff_ref, thr_col_ref, quota_col_ref, thr_row_ref, quota_row_ref,
                      lp_ref, lpt_ref, gate_ref, start_ref, off_ref, span_ref, full_ref, lead_ref,
                      total_ref, eq_col, eq_row, slot_col, slot_row):
    i = pl.program_id(0)
    tm = MOE_TILE

    @pl.when(i == 0)
    def _():
        eq_col[...] = jnp.zeros_like(eq_col)
        eq_row[...] = jnp.zeros_like(eq_row)
        slot_col[...] = jnp.zeros_like(slot_col)
        slot_row[...] = jnp.zeros_like(slot_row)

    r = lax.broadcasted_iota(jnp.int32, (tm, tm), 0)
    c = lax.broadcasted_iota(jnp.int32, (tm, tm), 1)
    before = jnp.where(r < c, 1.0, 0.0).astype(BF16)
    after = jnp.where(c < r, 1.0, 0.0).astype(BF16)
    thr, quota = thr_col_ref[:, 0:1], quota_col_ref[:, 0:1]
    thr_r, quota_r = thr_row_ref[0:1, :], quota_row_ref[0:1, :]
    is_expert = lax.broadcasted_iota(jnp.int32, (tm, LANES), 1) < N_EXPERTS
    earlier_expert = (lax.broadcasted_iota(jnp.int32, (N_EXPERTS, LANES), 1)
                      < lax.broadcasted_iota(jnp.int32, (N_EXPERTS, LANES), 0))
    tile_floor = lambda n: jnp.floor(n / BF16_ROWS) * BF16_ROWS
    tile_ceil = lambda n: jnp.ceil(n / BF16_ROWS) * BF16_ROWS

    eq_seen_col, eq_seen_row = eq_col[:, 0:1], eq_row[0:1, :]
    first_slot, first_slot_row = slot_col[...], slot_row[0:1, :]
    for s in range(aff_ref.shape[0] // tm):
        a = afft_ref[:, s * tm:(s + 1) * tm]
        eq = a == thr
        eq_f = jnp.where(eq, 1.0, 0.0)
        eq_rank = jnp.dot(eq_f.astype(BF16), before, preferred_element_type=F32) + eq_seen_col
        sel = (a > thr) | (eq & (eq_rank < quota))
        sel_f = jnp.where(sel, 1.0, 0.0)
        lp = jnp.dot(sel_f.astype(BF16), before, preferred_element_type=F32)
        lp_ref[:, s * tm:(s + 1) * tm] = jnp.where(sel, lp, UNSELECTED)
        count = jnp.sum(sel_f, axis=-1, keepdims=True)
        start = tile_floor(first_slot)
        lead = first_slot - start
        start_ref[s] = start
        lead_ref[s] = lead
        span_ref[s] = tile_ceil(lead + count)
        full_ref[s] = tile_floor(lead + count)
        first_slot = first_slot + count
        eq_seen_col = eq_seen_col + jnp.sum(eq_f, axis=-1, keepdims=True)

        at = aff_ref[s * tm:(s + 1) * tm, :]
        eq_t = (at == thr_r) & is_expert
        eq_tf = jnp.where(eq_t, 1.0, 0.0)
        eq_rank_t = jnp.dot(after, eq_tf.astype(BF16), preferred_element_type=F32) + eq_seen_row
        sel_t = ((at > thr_r) & is_expert) | (eq_t & (eq_rank_t < quota_r))
        sel_tf = jnp.where(sel_t, 1.0, 0.0)
        lp_t = jnp.dot(after, sel_tf.astype(BF16), preferred_element_type=F32)
        lpt_ref[s * tm:(s + 1) * tm, :] = jnp.where(sel_t, lp_t, UNSELECTED)
        gate_ref[s * tm:(s + 1) * tm, :] = jnp.where(sel_t, at, 0.0)
        eq_seen_row = eq_seen_row + jnp.sum(eq_tf, axis=0, keepdims=True)
        count_row = jnp.sum(sel_tf, axis=0, keepdims=True)
        span_row = tile_ceil(first_slot_row - tile_floor(first_slot_row) + count_row)
        off = jnp.sum(jnp.where(earlier_expert, span_row, 0.0), axis=-1, keepdims=True)
        off_ref[s] = jnp.broadcast_to(off, (N_EXPERTS, LANES))
        first_slot_row = first_slot_row + count_row

    eq_col[...] = jnp.broadcast_to(eq_seen_col, eq_col.shape)
    eq_row[...] = jnp.broadcast_to(eq_seen_row, eq_row.shape)
    slot_col[...] = first_slot
    slot_row[...] = jnp.broadcast_to(first_slot_row, slot_row.shape)
    total_ref[...] = first_slot


def _positions(afft, aff, thr_col, quota_col, thr_row, quota_row):
    t = aff.shape[0]
    tm = MOE_TILE * POSITION_TILES
    n_tiles = t // MOE_TILE
    c2 = lambda i: (0, 0)
    per_tile = pl.BlockSpec((POSITION_TILES, N_EXPERTS, LANES), lambda i: (i, 0, 0))
    per_tile_shape = jax.ShapeDtypeStruct((n_tiles, N_EXPERTS, LANES), F32)
    return pl.pallas_call(
        _positions_kernel,
        grid=(t // tm,),
        in_specs=[
            pl.BlockSpec((N_EXPERTS, tm), lambda i: (0, i)),
            pl.BlockSpec((tm, LANES), lambda i: (i, 0)),
            pl.BlockSpec((N_EXPERTS, LANES), c2), pl.BlockSpec((N_EXPERTS, LANES), c2),
            pl.BlockSpec((8, LANES), c2), pl.BlockSpec((8, LANES), c2),
        ],
        out_specs=[
            pl.BlockSpec((N_EXPERTS, tm), lambda i: (0, i)),
            pl.BlockSpec((tm, LANES), lambda i: (i, 0)),
            pl.BlockSpec((tm, LANES), lambda i: (i, 0)),
            per_tile, per_tile, per_tile, per_tile, per_tile,
            pl.BlockSpec((N_EXPERTS, LANES), c2),
        ],
        out_shape=[
            jax.ShapeDtypeStruct((N_EXPERTS, t), F32),
            jax.ShapeDtypeStruct((t, LANES), F32),
            jax.ShapeDtypeStruct((t, LANES), F32),
            per_tile_shape,
            per_tile_shape,
            per_tile_shape,
            per_tile_shape,
            per_tile_shape,
            jax.ShapeDtypeStruct((N_EXPERTS, LANES), F32),
        ],
        scratch_shapes=[pltpu.VMEM((N_EXPERTS, LANES), F32), pltpu.VMEM((8, LANES), F32),
                        pltpu.VMEM((N_EXPERTS, LANES), F32), pltpu.VMEM((8, LANES), F32)],
        compiler_params=pltpu.CompilerParams(dimension_semantics=("arbitrary",)),
        name="positions",
    )(afft, aff, thr_col, quota_col, thr_row, quota_row)


class _TileTables:
    def __init__(self, start_ref, off_ref, span_ref, full_ref, lead_ref):
        self.start_ref, self.off_ref, self.span_ref = start_ref, off_ref, span_ref
        self.full_ref, self.lead_ref = full_ref, lead_ref

    def slot(self, tile, e, k):
        return pl.multiple_of(self.start_ref[tile * N_EXPERTS + e] + k * BF16_ROWS, BF16_ROWS)

    def stack_rows(self, tile):
        last = tile * N_EXPERTS + N_EXPERTS - 1
        return self.off_ref[last] + self.span_ref[last]

    def fits(self, tile):
        return self.stack_rows(tile) <= STACK_ROWS

    def regular(self, tile):
        longest = self.span_ref[tile * N_EXPERTS]
        for e in range(1, N_EXPERTS):
            longest = jnp.maximum(longest, self.span_ref[tile * N_EXPERTS + e])
        return self.fits(tile) & (longest <= RUN_ROWS)

    def total_rows(self, tile, rows_ref):
        total = rows_ref[tile * N_EXPERTS]
        for e in range(1, N_EXPERTS):
            total = total + rows_ref[tile * N_EXPERTS + e]
        return total

    @staticmethod
    def wait_rows(n_rows, make_copy):
        n_groups = n_rows // BF16_ROWS
        for groups in WAIT_PIECES:
            @pl.when((n_groups & groups) != 0)
            def _(groups=groups):
                make_copy(groups * BF16_ROWS).wait()

    def for_each_piece(self, tile, rows_ref, fn):
        for e in range(N_EXPERTS):
            n_groups = rows_ref[tile * N_EXPERTS + e] // BF16_ROWS
            row = self.off_ref[tile * N_EXPERTS + e]
            slot = self.start_ref[tile * N_EXPERTS + e]
            for groups in RUN_PIECES:
                take = (n_groups & groups) != 0
                n_rows = groups * BF16_ROWS

                @pl.when(take)
                def _(e=e, row=row, slot=slot, n_rows=n_rows):
                    fn(e, pl.multiple_of(row, BF16_ROWS), pl.multiple_of(slot, BF16_ROWS), n_rows)

                step = jnp.where(take, n_rows, 0)
                row, slot = row + step, slot + step

    def for_each_group(self, tile, base, fn):
        for e in range(N_EXPERTS):
            off = self.off_ref[tile * N_EXPERTS + e]
            n_groups = self.span_ref[tile * N_EXPERTS + e] // BF16_ROWS

            def body(k, carry, e=e, off=off):
                s = off + k * BF16_ROWS - base

                @pl.when((s >= 0) & (s < STACK_ROWS))
                def _():
                    fn(e, k, pl.multiple_of(s, BF16_ROWS))
                return carry

            lax.fori_loop(0, n_groups, body, 0)


def _dispatch_kernel(start_ref, off_ref, span_ref, full_ref, lead_ref, xn_ref, lp_ref, xe_ref,
                     onehot, stack, tails, zeros, sem, sem_fill):
    i = pl.program_id(0)
    n = pl.num_programs(0)
    slot = i % 2
    tm = xn_ref.shape[0]
    tables = _TileTables(start_ref, off_ref, span_ref, full_ref, lead_ref)

    def send(buf, e, row, dst, n_rows):
        return pltpu.make_async_copy(stack.at[buf, pl.ds(row, n_rows), :],
                                     xe_ref.at[e, pl.ds(dst, n_rows), :], sem.at[buf])

    def compact():
        stack[slot, 0:STACK_ROWS] = jnp.dot(onehot[0:STACK_ROWS, :], xn_ref[...],
                                            preferred_element_type=F32).astype(BF16)

    def lead_of(e):
        return lead_ref[i * N_EXPERTS + e].astype(F32)

    def has_tail(e):
        return span_ref[i * N_EXPERTS + e] > full_ref[i * N_EXPERTS + e]

    def join_head(e, row):
        stack[slot, pl.ds(row, BF16_ROWS), :] = stack[slot, pl.ds(row, BF16_ROWS), :] + tails[e]

    def keep_tail(e, row):
        tails[e] = stack[slot, pl.ds(row, BF16_ROWS), :]

    @pl.when(i == 0)
    def _():
        onehot[...] = jnp.zeros_like(onehot)
        stack[...] = jnp.zeros_like(stack)
        tails[...] = jnp.zeros_like(tails)

    @pl.when((i >= 2) & tables.regular(jnp.maximum(i - 2, 0)))
    def _():
        tables.wait_rows(tables.total_rows(i - 2, full_ref), lambda m: send(slot, 0, 0, 0, m))

    regular = tables.regular(i)

    @pl.when(regular)
    def _():
        run_iota = lax.broadcasted_iota(jnp.int32, (RUN_ROWS, tm), 0).astype(F32)
        for e in range(N_EXPERTS):
            row = pl.multiple_of(off_ref[i * N_EXPERTS + e], BF16_ROWS)
            hit = lp_ref[e:e + 1, :] == run_iota - lead_of(e)
            onehot[pl.ds(row, RUN_ROWS), :] = jnp.where(hit, 1.0, 0.0).astype(BF16)
        compact()
        for e in range(N_EXPERTS):
            row = pl.multiple_of(off_ref[i * N_EXPERTS + e], BF16_ROWS)
            join_head(e, row)
            tail_row = pl.multiple_of(row + full_ref[i * N_EXPERTS + e], BF16_ROWS)
            tails[e] = jnp.where(has_tail(e), stack[slot, pl.ds(tail_row, BF16_ROWS), :],
                                 jnp.zeros((BF16_ROWS, D_MODEL), BF16))
        tables.for_each_piece(i, full_ref,
                              lambda e, row, dst, m: send(slot, e, row, dst, m).start(priority=e % 2))

    @pl.when(jnp.logical_not(regular))
    def _():
        group_iota = lax.broadcasted_iota(jnp.int32, (BF16_ROWS, tm), 0).astype(F32)
        n_rounds = (tables.stack_rows(i) + STACK_ROWS - 1) // STACK_ROWS

        def one_round(r, carry):
            base = r * STACK_ROWS
            onehot[...] = jnp.zeros_like(onehot)

            def mark(e, k, s):
                hit = lp_ref[e:e + 1, :] == group_iota + (k * BF16_ROWS).astype(F32) - lead_of(e)
                onehot[pl.ds(s, BF16_ROWS), :] = jnp.where(hit, 1.0, 0.0).astype(BF16)

            def settle(e, k, s):
                @pl.when(k == 0)
                def _():
                    join_head(e, s)

                @pl.when(k * BF16_ROWS == full_ref[i * N_EXPERTS + e])
                def _():
                    keep_tail(e, s)

            def completed(e, k):
                return k * BF16_ROWS < full_ref[i * N_EXPERTS + e]

            def group_copy(e, k, s):
                return send(slot, e, s, tables.slot(i, e, k), BF16_ROWS)

            def start_completed(e, k, s):
                @pl.when(completed(e, k))
                def _():
                    group_copy(e, k, s).start()

            def wait_completed(e, k, s):
                @pl.when(completed(e, k))
                def _():
                    group_copy(e, k, s).wait()

            tables.for_each_group(i, base, mark)
            compact()
            tables.for_each_group(i, base, settle)
            tables.for_each_group(i, base, start_completed)
            tables.for_each_group(i, base, wait_completed)
            return carry

        lax.fori_loop(0, n_rounds, one_round, 0)
        for e in range(N_EXPERTS):
            @pl.when(jnp.logical_not(has_tail(e)))
            def _():
                tails[e] = jnp.zeros((BF16_ROWS, D_MODEL), BF16)

    @pl.when(i == n - 1)
    def _():
        @pl.when((i >= 1) & tables.regular(jnp.maximum(i - 1, 0)))
        def _():
            tables.wait_rows(tables.total_rows(i - 1, full_ref), lambda m: send(1 - slot, 0, 0, 0, m))

        @pl.when(regular)
        def _():
            tables.wait_rows(tables.total_rows(i, full_ref), lambda m: send(slot, 0, 0, 0, m))

        zeros[...] = jnp.zeros_like(zeros)
        cap_rows = xe_ref.shape[1]
        big = zeros.shape[0]

        def fill(e, row0, rows):
            return pltpu.make_async_copy(zeros.at[pl.ds(0, rows), :],
                                         xe_ref.at[e, pl.ds(row0, rows), :], sem_fill.at[0])

        for e in range(N_EXPERTS):
            used = start_ref[i * N_EXPERTS + e] + span_ref[i * N_EXPERTS + e]
            n_small = ((-used) & (big - 1)) // BF16_ROWS
            base = used + n_small * BF16_ROWS
            n_big = (cap_rows - base) // big

            def small_row(k):
                return pl.multiple_of(used + k * BF16_ROWS, BF16_ROWS)

            def big_row(k):
                return pl.multiple_of(base + k * big, big)

            lax.fori_loop(0, n_small, lambda k, c: (fill(e, small_row(k), BF16_ROWS).start(), c)[1], 0)
            lax.fori_loop(0, n_big, lambda k, c: (fill(e, big_row(k), big).start(), c)[1], 0)
            lax.fori_loop(0, n_small, lambda k, c: (fill(e, small_row(k), BF16_ROWS).wait(), c)[1], 0)
            lax.fori_loop(0, n_big, lambda k, c: (fill(e, big_row(k), big).wait(), c)[1], 0)


def _dispatch(tables, xn, lp, cap_rows):
    t = xn.shape[0]
    tm = MOE_TILE
    return pl.pallas_call(
        _dispatch_kernel,
        grid_spec=pltpu.PrefetchScalarGridSpec(
            num_scalar_prefetch=len(tables),
            grid=(t // tm,),
            in_specs=[
                pl.BlockSpec((tm, D_MODEL), lambda i, *_: (i, 0)),
                pl.BlockSpec((N_EXPERTS, tm), lambda i, *_: (0, i)),
            ],
            out_specs=pl.BlockSpec(memory_space=pl.ANY),
            scratch_shapes=[pltpu.VMEM((STACK_ROWS + RUN_ROWS, tm), BF16),
                            pltpu.VMEM((2, STACK_ROWS + BF16_ROWS, D_MODEL), BF16),
                            pltpu.VMEM((N_EXPERTS, BF16_ROWS, D_MODEL), BF16),
                            pltpu.VMEM((FILL_ROWS, D_MODEL), BF16),
                            pltpu.SemaphoreType.DMA((2,)),
                            pltpu.SemaphoreType.DMA((1,))],
        ),
        out_shape=jax.ShapeDtypeStruct((N_EXPERTS, cap_rows, D_MODEL), BF16),
        compiler_params=pltpu.CompilerParams(dimension_semantics=("arbitrary",)),
        name="dispatch",
    )(*tables, xn, lp)


def _ffn_kernel(used_ref, xe_ref, wg_ref, wu_ref, wd_ref, ye_ref, wg, wu, wd):
    e, j = pl.program_id(0), pl.program_id(1)
    bs = xe_ref.shape[1]
    n_valid = used_ref[e] - j * bs

    @pl.when(j == 0)
    def _():
        wg[...] = wg_ref[0].astype(BF16)
        wu[...] = wu_ref[0].astype(BF16)
        wd[...] = wd_ref[0].astype(BF16)

    @pl.when(n_valid > 0)
    def _():
        row = lax.broadcasted_iota(jnp.int32, (bs, D_MODEL), 0)
        x = jnp.where(row < n_valid, xe_ref[0], jnp.zeros((bs, D_MODEL), BF16))
        gate = jnp.dot(x, wg[...], preferred_element_type=F32)
        up = jnp.dot(x, wu[...], preferred_element_type=F32)
        hdn = (jax.nn.silu(gate) * up).astype(BF16)
        ye_ref[0] = jnp.dot(hdn, wd[...], preferred_element_type=F32).astype(BF16)

    @pl.when(n_valid <= 0)
    def _():
        ye_ref[0] = jnp.zeros((bs, D_MODEL), BF16)


def _ffn(used, xe, w_gate, w_up, w_down):
    cap_rows = xe.shape[1]
    bs = FFN_BLOCK

    def x_map(e, j, used_ref):
        last = jnp.maximum((used_ref[e] + bs - 1) // bs - 1, 0)
        return (e, jnp.minimum(j, last), 0)

    w_map = lambda e, j, used_ref: (e, 0, 0)
    return pl.pallas_call(
        _ffn_kernel,
        grid_spec=pltpu.PrefetchScalarGridSpec(
            num_scalar_prefetch=1,
            grid=(N_EXPERTS, cap_rows // bs),
            in_specs=[
                pl.BlockSpec((1, bs, D_MODEL), x_map),
                pl.BlockSpec((1, D_MODEL, D_MODEL), w_map),
                pl.BlockSpec((1, D_MODEL, D_MODEL), w_map),
                pl.BlockSpec((1, D_MODEL, D_MODEL), w_map),
            ],
            out_specs=pl.BlockSpec((1, bs, D_MODEL), lambda e, j, used_ref: (e, j, 0)),
            scratch_shapes=[pltpu.VMEM((D_MODEL, D_MODEL), BF16)] * 3,
        ),
        out_shape=jax.ShapeDtypeStruct((N_EXPERTS, cap_rows, D_MODEL), BF16),
        compiler_params=pltpu.CompilerParams(dimension_semantics=("arbitrary", "arbitrary"),
                                             vmem_limit_bytes=FFN_VMEM_BYTES),
        name="expert_ffn",
    )(used, xe, w_gate, w_up, w_down)


def _gate_matrix(lpt_ref, gate_ref, offv_ref, spanv_ref, leadv_ref, base):
    off = offv_ref[0][:, 0:1]
    end = off + spanv_ref[0][:, 0:1]
    first = off + leadv_ref[0][:, 0:1]
    row = lax.broadcasted_iota(jnp.int32, (N_EXPERTS, STACK_ROWS), 1).astype(F32) + base
    owner = (row >= off) & (row < end)
    owner_pad = jnp.concatenate(
        [jnp.where(owner, 1.0, 0.0), jnp.zeros((LANES - N_EXPERTS, STACK_ROWS), F32)], axis=0).astype(BF16)
    slot_of_row = row[0:1, :] - jnp.sum(jnp.where(owner, first, 0.0), axis=0, keepdims=True)
    slot_of_token = jnp.dot(lpt_ref[...].astype(BF16), owner_pad, preferred_element_type=F32)
    gate_of_token = jnp.dot(gate_ref[...].astype(BF16), owner_pad, preferred_element_type=F32)
    return jnp.where(slot_of_token == slot_of_row, gate_of_token, 0.0).astype(BF16)


def _combine_kernel(start_ref, off_ref, span_ref, full_ref, lead_ref, x1_ref, lpt_ref, gate_ref,
                    offv_ref, spanv_ref, leadv_ref, gfin_ref, ye_ref, y_ref,
                    stack, extra, sem, sem_extra):
    i = pl.program_id(0)
    n = pl.num_programs(0)
    slot = i % 2
    tables = _TileTables(start_ref, off_ref, span_ref, full_ref, lead_ref)

    def fetch(buf, e, row, src, n_rows):
        return pltpu.make_async_copy(ye_ref.at[e, pl.ds(src, n_rows), :],
                                     stack.at[buf, pl.ds(row, n_rows), :], sem.at[buf])

    @pl.when(i == 0)
    def _():
        stack[...] = jnp.zeros_like(stack)
        extra[...] = jnp.zeros_like(extra)

        @pl.when(tables.regular(0))
        def _():
            tables.for_each_piece(0, span_ref, lambda e, row, src, m: fetch(0, e, row, src, m).start())

    nxt = jnp.minimum(i + 1, n - 1)

    @pl.when((i + 1 < n) & tables.regular(nxt))
    def _():
        tables.for_each_piece(nxt, span_ref,
                              lambda e, row, src, m: fetch(1 - slot, e, row, src, m).start())

    regular = tables.regular(i)

    @pl.when(regular)
    def _():
        tables.wait_rows(tables.stack_rows(i), lambda m: fetch(slot, 0, 0, 0, m))

    moe = jnp.dot(_gate_matrix(lpt_ref, gate_ref, offv_ref, spanv_ref, leadv_ref, 0.0), stack[slot],
                  preferred_element_type=F32)
    moe = jnp.where(regular, moe, 0.0)

    n_rounds = jnp.where(regular, 0, (tables.stack_rows(i) + STACK_ROWS - 1) // STACK_ROWS)

    def one_round(r, acc):
        base = r * STACK_ROWS

        def fetch_group(e, k, s):
            return pltpu.make_async_copy(ye_ref.at[e, pl.ds(tables.slot(i, e, k), BF16_ROWS), :],
                                         extra.at[pl.ds(s, BF16_ROWS), :], sem_extra.at[0])

        tables.for_each_group(i, base, lambda e, k, s: fetch_group(e, k, s).start())
        tables.for_each_group(i, base, lambda e, k, s: fetch_group(e, k, s).wait())
        g = _gate_matrix(lpt_ref, gate_ref, offv_ref, spanv_ref, leadv_ref, base.astype(F32))
        return acc + jnp.dot(g, extra[...], preferred_element_type=F32)

    moe = lax.fori_loop(0, n_rounds, one_round, moe)
    y_ref[...] = _rms(x1_ref[...] + moe, gfin_ref[...])


def _combine(tables, x1, lpt, gate, off_vec, span_vec, lead_vec, g_final, ye):
    t = x1.shape[0]
    tm = MOE_TILE
    row = lambda i, *_: (i, 0)
    per_tile = pl.BlockSpec((1, N_EXPERTS, LANES), lambda i, *_: (i, 0, 0))
    return pl.pallas_call(
        _combine_kernel,
        grid_spec=pltpu.PrefetchScalarGridSpec(
            num_scalar_prefetch=len(tables),
            grid=(t // tm,),
            in_specs=[
                pl.BlockSpec((tm, D_MODEL), row),
                pl.BlockSpec((tm, LANES), row),
                pl.BlockSpec((tm, LANES), row),
                per_tile, per_tile, per_tile,
                pl.BlockSpec((1, D_MODEL), lambda i, *_: (0, 0)),
                pl.BlockSpec(memory_space=pl.ANY),
            ],
            out_specs=pl.BlockSpec((tm, D_MODEL), row),
            scratch_shapes=[pltpu.VMEM((2, STACK_ROWS, D_MODEL), BF16),
                            pltpu.VMEM((STACK_ROWS, D_MODEL), BF16),
                            pltpu.SemaphoreType.DMA((2,)),
                            pltpu.SemaphoreType.DMA((1,))],
        ),
        out_shape=jax.ShapeDtypeStruct((t, D_MODEL), F32),
        compiler_params=pltpu.CompilerParams(dimension_semantics=("arbitrary",)),
        name="combine",
    )(*tables, x1, lpt, gate, off_vec, span_vec, lead_vec, g_final, ye)


def _rope_tables(seq):
    half = HEAD_DIM // 2
    inv_freq = jnp.power(ROPE_THETA, -jnp.arange(half, dtype=F32) * 2.0 / HEAD_DIM)
    ang = jnp.arange(seq, dtype=F32)[:, None] * inv_freq[None, :]
    cos, sin = jnp.cos(ang), jnp.sin(ang)
    reps = LANES // HEAD_DIM
    return (jnp.tile(jnp.concatenate([cos, cos], axis=1), (1, reps)),
            jnp.tile(jnp.concatenate([-sin, sin], axis=1), (1, reps)))


def _prepare_weights(g_mix, w_in, ln_b_g, ln_b_b, w_spatial, b_spatial, g_mem, w_mem_kv,
                     g_out, w_out, g_ffn, w_router, w_gate, w_up, w_down, g_final):
    scale = jnp.ones((w_in.shape[-1],), F32)
    scale = scale.at[:D_A].set(HEAD_DIM ** -0.5 * LOG2_E)
    scale = scale.at[3 * D_A + 2 * D_B:].set((D_C // N_HEADS_C) ** -0.5)
    head_of_lane = jnp.arange(D_A) // HEAD_DIM
    return dict(
        g_mix=g_mix[0][None], w_in=(w_in[0] * scale).astype(BF16),
        ln_g=ln_b_g[0][None], ln_b=ln_b_b[0][None],
        w_sp=w_spatial[0].astype(BF16),
        b_sp=jnp.repeat(b_spatial[0].T, D_B // N_GROUPS_B, axis=1),
        g_mem=g_mem[0][None], w_kv=w_mem_kv[0].astype(BF16),
        expand=sum((jnp.arange(LANES)[:, None] == (_lse_lane(h) + dup))
                   & (head_of_lane[None, :] == h)
                   for h in range(N_HEADS_A) for dup in (0, N_HEADS_A)).astype(BF16),
        g_out=g_out[0][None], w_out=w_out[0].astype(BF16),
        g_ffn=g_ffn[0][None],
        w_router=jnp.pad(w_router[0], ((0, 0), (0, LANES - N_EXPERTS))).astype(BF16),
        w_gate=w_gate[0], w_up=w_up[0], w_down=w_down[0],
        g_final=g_final[None],
    )


def _encoder(x, mem, w):
    b, seq, _ = x.shape
    t = b * seq
    x2 = x.reshape(t, D_MODEL)
    cos_t, sin_t = _rope_tables(seq)
    mem_k, mem_v = _mem_kv(mem, w["g_mem"], w["w_kv"])
    q4, k4, v4, q16, k16, v16, u, vb, qc = _in_proj(x2, w["g_mix"], w["w_in"], cos_t, sin_t,
                                                    w["ln_g"], w["ln_b"], b, seq)
    pats = [_banded_attention(q4, k4, v4, DILATIONS[1], "attn_d1"),
            _banded_attention(q4, k4, v4, 1, "attn_d4"),
            _banded_attention(q16, k16, v16, 1, "attn_d16")]
    x1, xn, aff, afft = _mix(x2, [p[0] for p in pats], [p[1] for p in pats], u, vb, qc,
                             mem_k, mem_v, w["w_sp"], w["b_sp"], w["expand"], w["g_out"],
                             w["w_out"], w["g_ffn"], w["w_router"], seq)

    cap = EC_CAPACITY_FACTOR * t // N_EXPERTS
    thr_col, quota_col, thr_row, quota_row = _threshold(afft, cap)
    lp, lpt, gate, start, off, span, full, lead, total = _positions(
        afft, aff, thr_col, quota_col, thr_row, quota_row)
    tables = [a[:, :, 0].astype(jnp.int32).reshape(-1) for a in (start, off, span, full, lead)]
    used = total[:, 0].astype(jnp.int32)
    assert cap % BF16_ROWS == 0
    cap_rows = -(-cap // FFN_BLOCK) * FFN_BLOCK
    xe = _dispatch(tables, xn, lp, cap_rows)
    ye = _ffn(used, xe, w["w_gate"], w["w_up"], w["w_down"])
    y = _combine(tables, x1, lpt, gate, off, span, lead, w["g_final"], ye)
    return y.reshape(b, seq, D_MODEL)


def kernel(x_prompt, x_sample, mem_prompt, mem_sample, g_mix, w_in, ln_b_g, ln_b_b, w_spatial,
           b_spatial, g_mem, w_mem_kv, g_out, w_out, g_ffn, w_router, w_gate, w_up, w_down, g_final):
    w = _prepare_weights(g_mix, w_in, ln_b_g, ln_b_b, w_spatial, b_spatial, g_mem, w_mem_kv,
                         g_out, w_out, g_ffn, w_router, w_gate, w_up, w_down, g_final)
    return (_encoder(x_prompt, mem_prompt, w), _encoder(x_sample, mem_sample, w))
```
